```python
import jax, jax.numpy as jnp
from jax import lax
import numpy as np

D_MODEL = 2048
BATCH = 8
SEQ = 2048
DEPTH = 2

GRID_W = 64
CTX_LEN = 256
EPS = 1e-6
NEG_INF = -1e30
HEAD_DIM = 128
A_Q_HEADS = (D_MODEL // 2) // HEAD_DIM
A_KV_HEADS = 2
A_GROUP = A_Q_HEADS // A_KV_HEADS
WINDOW = 128
BLOCK_Q = 128
ROPE_BASE = 10000.0
B_HEADS = 4
B_DV = (D_MODEL // 2) // B_HEADS
B_DK = B_DV // 2
B_GATE_RANK = 16
B_GATE_NORM = 16.0
B_CHUNK = 64
C_GROUPS = 8
C_GROUP_DIM = D_MODEL // C_GROUPS
D_FF = 4 * D_MODEL

N_EVEN = (DEPTH + 1) // 2
N_ODD = DEPTH // 2
A_Q_W = A_Q_HEADS * HEAD_DIM
A_KV_W = A_KV_HEADS * HEAD_DIM
B_QK_W = B_HEADS * B_DK
B_V_W = B_HEADS * B_DV
IN_SPLITS = (A_Q_W, A_KV_W, A_KV_W, B_QK_W, B_QK_W, B_V_W, B_V_W, B_GATE_RANK, B_GATE_RANK)
IN_COLS = sum(IN_SPLITS)
MIX_OUT = A_Q_W + B_V_W

kernel_name = "hybrid_dit_window_gla_fourier"


def rms_norm(x, g):
    xf = x.astype(jnp.float32)
    y = xf * lax.rsqrt(jnp.mean(xf * xf, axis=-1, keepdims=True) + EPS)
    return (y * g.astype(jnp.float32)).astype(x.dtype)


def modulate(h, shift, scale):
    return h * (1 + scale) + shift


def heads(t, n):
    b, l, _ = t.shape
    return t.reshape(b, l, n, -1).transpose(0, 2, 1, 3)


def merge_heads(t):
    b, n, l, d = t.shape
    return t.transpose(0, 2, 1, 3).reshape(b, l, n * d)


def flip(t):
    return t[:, :, ::-1]


def axial_rope_angles(n_rows):
    row = jnp.repeat(jnp.arange(n_rows), GRID_W).astype(jnp.float32)
    col = jnp.tile(jnp.arange(GRID_W), n_rows).astype(jnp.float32)
    n_freq = HEAD_DIM // 4
    inv_freq = ROPE_BASE ** (-jnp.arange(n_freq, dtype=jnp.float32) / n_freq)
    return row[:, None] * inv_freq, col[:, None] * inv_freq


def rope_1d(x, ang):
    n = ang.shape[-1]
    cos, sin = jnp.cos(ang).astype(x.dtype), jnp.sin(ang).astype(x.dtype)
    x1, x2 = x[..., :n], x[..., n:]
    return jnp.concatenate([x1 * cos - x2 * sin, x2 * cos + x1 * sin], axis=-1)


def apply_axial_rope(x, ang_r, ang_c):
    half = HEAD_DIM // 2
    return jnp.concatenate([rope_1d(x[..., :half], ang_r), rope_1d(x[..., half:], ang_c)], axis=-1)


def ab_project(h, w_in, q_norm, k_norm, gk_f, gk_f_b, gk_b, gk_b_b):
    parts = jnp.split(h @ w_in, np.cumsum(IN_SPLITS)[:-1].tolist(), axis=-1)
    aq, ak, av, bq, bk, bv, bgate, lr_f, lr_b = parts
    aq = rms_norm(heads(aq, A_Q_HEADS), q_norm)
    ak = rms_norm(heads(ak, A_KV_HEADS), k_norm)
    av = heads(av, A_KV_HEADS)

    def log_decay(lr, w, b):
        return jax.nn.log_sigmoid((lr @ w + b).astype(jnp.float32)) / B_GATE_NORM

    gf = heads(log_decay(lr_f, gk_f, gk_f_b), B_HEADS)
    gb = heads(log_decay(lr_b, gk_b, gk_b_b), B_HEADS)
    bq = heads(bq.astype(jnp.float32), B_HEADS) * B_DK ** -0.5
    bk = heads(bk.astype(jnp.float32), B_HEADS)
    bv = heads(bv.astype(jnp.float32), B_HEADS)
    return aq, ak, av, bq, bk, bv, gf, gb, bgate


def window_attention(q, k, v, kc, vc, sink):
    bsz, _, n_lat, d = q.shape
    nb = n_lat // BLOCK_Q
    qb = q.reshape(bsz, A_KV_HEADS, A_GROUP, nb, BLOCK_Q, d)
    pad = ((0, 0), (0, 0), (BLOCK_Q, BLOCK_Q), (0, 0))
    kp = jnp.pad(k, pad).reshape(bsz, A_KV_HEADS, nb + 2, BLOCK_Q, d)
    vp = jnp.pad(v, pad).reshape(bsz, A_KV_HEADS, nb + 2, BLOCK_Q, d)

    def band(t):
        return jnp.concatenate([t[:, :, :-2], t[:, :, 1:-1], t[:, :, 2:]], axis=3)

    kw, vw = band(kp), band(vp)
    scale = d ** -0.5
    s_lat = jnp.einsum("bkgnqd,bknsd->bkgnqs", qb, kw, preferred_element_type=jnp.float32) * scale
    s_ctx = jnp.einsum("bkgnqd,bkcd->bkgnqc", qb, kc, preferred_element_type=jnp.float32) * scale
    blk = jnp.arange(nb)[:, None, None]
    qpos = blk * BLOCK_Q + jnp.arange(BLOCK_Q)[None, :, None]
    kpos = (blk - 1) * BLOCK_Q + jnp.arange(3 * BLOCK_Q)[None, None, :]
    valid = (jnp.abs(qpos - kpos) <= WINDOW) & (kpos >= 0) & (kpos < n_lat)
    s_lat = jnp.where(valid, s_lat, NEG_INF)
    s_sink = jnp.broadcast_to(sink.astype(jnp.float32).reshape(A_KV_HEADS, A_GROUP, 1, 1, 1),
                              s_lat.shape[:-1] + (1,))
    p = jax.nn.softmax(jnp.concatenate([s_lat, s_ctx, s_sink], axis=-1), axis=-1).astype(v.dtype)
    n_w = 3 * BLOCK_Q
    n_ctx = kc.shape[2]
    o = (jnp.einsum("bkgnqs,bknsd->bkgnqd", p[..., :n_w], vw)
         + jnp.einsum("bkgnqc,bkcd->bkgnqd", p[..., n_w:n_w + n_ctx], vc))
    return o.reshape(bsz, A_Q_HEADS, n_lat, d)


def context_attention(qc, kc, vc, sink):
    bsz, _, n_ctx, d = qc.shape
    qg = qc.reshape(bsz, A_KV_HEADS, A_GROUP, n_ctx, d)
    s = jnp.einsum("bkgqd,bkcd->bkgqc", qg, kc, preferred_element_type=jnp.float32) * d ** -0.5
    s_sink = jnp.broadcast_to(sink.astype(jnp.float32).reshape(A_KV_HEADS, A_GROUP, 1, 1), s.shape[:-1] + (1,))
    p = jax.nn.softmax(jnp.concatenate([s, s_sink], axis=-1), axis=-1)[..., :n_ctx].astype(vc.dtype)
    o = jnp.einsum("bkgqc,bkcd->bkgqd", p, vc)
    return o.reshape(bsz, A_Q_HEADS, n_ctx, d)


def gla_chunked(q, k, v, g, s0):
    bsz, nh, n_tok, dk = q.shape
    dv = v.shape[-1]
    n = n_tok // B_CHUNK

    def rs(t):
        return t.reshape(bsz, nh, n, B_CHUNK, t.shape[-1])

    q, k, v, g = rs(q), rs(k), rs(v), rs(g)
    b = jnp.cumsum(g, axis=3)
    b_last = b[:, :, :, -1:]
    qe = q * jnp.exp(b)
    ke = k * jnp.exp(-b)
    kd = k * jnp.exp(b_last - b)
    causal = jnp.tril(jnp.ones((B_CHUNK, B_CHUNK), dtype=bool))
    a = jnp.where(causal, jnp.einsum("bhncd,bhnsd->bhncs", qe, ke), 0.0)
    o_intra = jnp.einsum("bhncs,bhnse->bhnce", a, v)

    def step(state, xs):
        qe_c, kd_c, v_c, dec = xs
        o = jnp.einsum("bhcd,bhde->bhce", qe_c, state)
        state = dec[..., None] * state + jnp.einsum("bhcd,bhce->bhde", kd_c, v_c)
        return state, o

    xs = (jnp.moveaxis(qe, 2, 0), jnp.moveaxis(kd, 2, 0), jnp.moveaxis(v, 2, 0),
          jnp.moveaxis(jnp.exp(b[:, :, :, -1]), 2, 0))
    s_fin, o_inter = lax.scan(step, s0, xs)
    o = o_intra + jnp.moveaxis(o_inter, 0, 2)
    return o.reshape(bsz, nh, n_tok, dv), s_fin


def gla_final_state(k, v, g):
    b = jnp.cumsum(g, axis=2)
    return jnp.einsum("bhtd,bhte->bhde", k * jnp.exp(b[:, :, -1:] - b), v)


def ab_output(o_a, o_b, gate, gla_norm, w_out):
    o_b = merge_heads(rms_norm(o_b, gla_norm)).astype(gate.dtype) * jax.nn.silu(gate)
    return jnp.concatenate([merge_heads(o_a), o_b], axis=-1) @ w_out


def ab_mixer(h, hc, need_ctx_out, w_in, q_norm, k_norm, sink, gk_f, gk_f_b, gk_b, gk_b_b,
             gla_norm, w_out, ang_r, ang_c):
    aq, ak, av, bq, bk, bv, gf, gb, bgate = ab_project(h, w_in, q_norm, k_norm, gk_f, gk_f_b, gk_b, gk_b_b)
    cq, ck, cv, cbq, cbk, cbv, cgf, cgb, cgate = ab_project(hc, w_in, q_norm, k_norm, gk_f, gk_f_b, gk_b, gk_b_b)
    aq = apply_axial_rope(aq, ang_r, ang_c)
    ak = apply_axial_rope(ak, ang_r, ang_c)
    o_a = window_attention(aq, ak, av, ck, cv, sink)
    yc = None
    if need_ctx_out:
        zero = jnp.zeros((hc.shape[0], B_HEADS, B_DK, B_DV), jnp.float32)
        oc_f, s_f = gla_chunked(cbq, cbk, cbv, cgf, zero)
        oc_b, s_b = gla_chunked(flip(cbq), flip(cbk), flip(cbv), flip(cgb), zero)
        oc_a = context_attention(cq, ck, cv, sink)
        yc = ab_output(oc_a, oc_f + flip(oc_b), cgate, gla_norm, w_out)
    else:
        s_f = gla_final_state(cbk, cbv, cgf)
        s_b = gla_final_state(flip(cbk), flip(cbv), flip(cgb))
    o_f, _ = gla_chunked(bq, bk, bv, gf, s_f)
    o_b, _ = gla_chunked(flip(bq), flip(bk), flip(bv), flip(gb), s_b)
    y = ab_output(o_a, o_f + flip(o_b), bgate, gla_norm, w_out)
    return y, yc


def fourier_mixer(h, w_out, b_out):
    bsz, n_tok, _ = h.shape
    hg = h.astype(jnp.float32).reshape(bsz, n_tok, C_GROUPS, C_GROUP_DIM)
    f = jnp.fft.fft2(hg, axes=(1, 3), norm="ortho").real.reshape(bsz, n_tok, D_MODEL)
    return f.astype(h.dtype) @ w_out + b_out


def sq_relu_mlp(h, w1, w2):
    return jnp.square(jax.nn.relu(h @ w1)) @ w2


def setup_inputs(seed: int = 0) -> dict:
    key = jax.random.key(seed)
    ks = jax.random.split(key, 22)

    def nrm(k, shape, scale):
        return jax.random.normal(k, shape, jnp.float32) * scale

    def gain(k, shape):
        return 1.0 + 0.02 * jax.random.normal(k, shape, jnp.float32)

    return {
        "x": nrm(ks[0], (BATCH, SEQ, D_MODEL), 1.0),
        "c": nrm(ks[1], (BATCH, D_MODEL), 1.0),
        "ctx": nrm(ks[2], (BATCH, CTX_LEN, D_MODEL), 1.0),
        "c_ctx": nrm(ks[3], (D_MODEL,), 1.0),
        "ada_w": nrm(ks[4], (DEPTH, D_MODEL, 6 * D_MODEL), D_MODEL ** -0.5),
        "ada_b": nrm(ks[5], (DEPTH, 6 * D_MODEL), 0.02),
        "norm_mix": gain(ks[6], (DEPTH, D_MODEL)),
        "norm_mlp": gain(ks[7], (DEPTH, D_MODEL)),
        "mlp_w1": nrm(ks[8], (DEPTH, D_MODEL, D_FF), D_MODEL ** -0.5),
        "mlp_w2": nrm(ks[9], (DEPTH, D_FF, D_MODEL), D_FF ** -0.5),
        "ab_w_in": nrm(ks[10], (N_EVEN, D_MODEL, IN_COLS), D_MODEL ** -0.5),
        "ab_q_norm": gain(ks[11], (N_EVEN, HEAD_DIM)),
        "ab_k_norm": gain(ks[12], (N_EVEN, HEAD_DIM)),
        "ab_sink": nrm(ks[13], (N_EVEN, A_Q_HEADS), 0.5),
        "ab_gk_f": nrm(ks[14], (N_EVEN, B_GATE_RANK, B_QK_W), B_GATE_RANK ** -0.5),
        "ab_gk_f_bias": nrm(ks[15], (N_EVEN, B_QK_W), 0.1),
        "ab_gk_b": nrm(ks[16], (N_EVEN, B_GATE_RANK, B_QK_W), B_GATE_RANK ** -0.5),
        "ab_gk_b_bias": nrm(ks[17], (N_EVEN, B_QK_W), 0.1),
        "ab_gla_norm": gain(ks[18], (N_EVEN, B_DV)),
        "ab_w_out": nrm(ks[19], (N_EVEN, MIX_OUT, D_MODEL), MIX_OUT ** -0.5),
        "c_w_out": nrm(ks[20], (N_ODD, D_MODEL, D_MODEL), D_MODEL ** -0.5),
        "c_b_out": nrm(ks[21], (N_ODD, D_MODEL), 0.02),
    }


def reference(x, c, ctx, c_ctx, ada_w, ada_b, norm_mix, norm_mlp, mlp_w1, mlp_w2,
              ab_w_in, ab_q_norm, ab_k_norm, ab_sink, ab_gk_f, ab_gk_f_bias, ab_gk_b, ab_gk_b_bias,
              ab_gla_norm, ab_w_out, c_w_out, c_b_out):
    n_lat = x.shape[1]
    rows = n_lat // GRID_W
    ang_r, ang_c = axial_rope_angles(rows)
    silu_c = jax.nn.silu(c)
    silu_cc = jax.nn.silu(c_ctx)
    for layer in range(DEPTH):
        ctx_live = any(k % 2 == 0 for k in range(layer + 1, DEPTH))
        sh1, sc1, g1, sh2, sc2, g2 = jnp.split((silu_c @ ada_w[layer] + ada_b[layer])[:, None, :], 6, axis=-1)
        csh1, csc1, cg1, csh2, csc2, cg2 = jnp.split(silu_cc @ ada_w[layer] + ada_b[layer], 6, axis=-1)
        h = modulate(rms_norm(x, norm_mix[layer]), sh1, sc1)
        if layer % 2 == 0:
            i = layer // 2
            hc = modulate(rms_norm(ctx, norm_mix[layer]), csh1, csc1)
            y, yc = ab_mixer(h, hc, ctx_live, ab_w_in[i], ab_q_norm[i], ab_k_norm[i], ab_sink[i],
                             ab_gk_f[i], ab_gk_f_bias[i], ab_gk_b[i], ab_gk_b_bias[i],
                             ab_gla_norm[i], ab_w_out[i], ang_r, ang_c)
            if ctx_live:
                ctx = ctx + cg1 * yc
        else:
            j = layer // 2
            y = fourier_mixer(h, c_w_out[j], c_b_out[j])
            if ctx_live:
                hc = modulate(rms_norm(ctx, norm_mix[layer]), csh1, csc1)
                ctx = ctx + cg1 * fourier_mixer(hc, c_w_out[j], c_b_out[j])
        x = x + g1 * y
        x = x + g2 * sq_relu_mlp(modulate(rms_norm(x, norm_mlp[layer]), sh2, sc2), mlp_w1[layer], mlp_w2[layer])
        if ctx_live:
            ctx = ctx + cg2 * sq_relu_mlp(modulate(rms_norm(ctx, norm_mlp[layer]), csh2, csc2),
                                          mlp_w1[layer], mlp_w2[layer])
    return x
```

```python
import functools

import numpy as np
import jax
import jax.numpy as jnp
from jax import lax
from jax.experimental import pallas as pl
from jax.experimental.pallas import tpu as pltpu

D_MODEL = 2048
GRID_W = 64
EPS = 1e-6
NEG_INF = -1e30
HEAD_DIM = 128
A_Q_HEADS = 8
A_KV_HEADS = 2
A_GROUP = A_Q_HEADS // A_KV_HEADS
WINDOW = 128
BLOCK_Q = 128
ROPE_BASE = 10000.0
B_HEADS = 4
B_DV = 256
B_DK = 128
B_GATE_RANK = 16
B_GATE_NORM = 16.0
B_CHUNK = 64
C_GROUPS = 8
C_GROUP_DIM = D_MODEL // C_GROUPS
D_FF = 4 * D_MODEL

A_Q_W = A_Q_HEADS * HEAD_DIM
A_KV_W = A_KV_HEADS * HEAD_DIM
B_QK_W = B_HEADS * B_DK
B_V_W = B_HEADS * B_DV
OFF_AQ = 0
OFF_AK = OFF_AQ + A_Q_W
OFF_AV = OFF_AK + A_KV_W
OFF_BQ = OFF_AV + A_KV_W
OFF_BK = OFF_BQ + B_QK_W
OFF_BV = OFF_BK + B_QK_W
OFF_GATE = OFF_BV + B_V_W
OFF_LR = OFF_GATE + B_V_W
IN_COLS = OFF_LR + 2 * B_GATE_RANK
LANE = 128
IN_COLS_PAD = OFF_LR + LANE
MOD_ROWS = 16

VMEM_LIMIT = 56 * 1024 * 1024

BF16 = jnp.bfloat16
F32 = jnp.float32


def _cparams(sem):
    return pltpu.CompilerParams(dimension_semantics=sem, vmem_limit_bytes=VMEM_LIMIT)


def _dot(a, b):
    return jnp.dot(a, b, preferred_element_type=F32)


def _dot_nt(a, b):
    return lax.dot_general(a, b, (((1,), (1,)), ((), ())), preferred_element_type=F32)


def _dot_tn(a, b):
    return lax.dot_general(a, b, (((0,), (0,)), ((), ())), preferred_element_type=F32)


def _split3(v):
    v1 = v.astype(BF16)
    r1 = v - v1.astype(F32)
    v2 = r1.astype(BF16)
    v3 = (r1 - v2.astype(F32)).astype(BF16)
    return v1, v2, v3


def _tri_sum(tri, g):
    g1, g2, g3 = _split3(g)
    return _dot(tri, g1) + _dot(tri, g2) + _dot(tri, g3)


def _rms(xf, gain):
    return xf * lax.rsqrt(jnp.mean(xf * xf, axis=-1, keepdims=True) + EPS) * gain


def _ada_kernel(c_ref, w_ref, b_ref, o_ref):
    cv = c_ref[...]
    s = (cv * jax.nn.sigmoid(cv)).astype(BF16)
    o_ref[...] = _dot(s, w_ref[...].astype(BF16)) + b_ref[...]


def _ada(cvec, ada_w, ada_b):
    depth = ada_w.shape[0]
    n = ada_w.shape[2]
    tn = 1024
    return pl.pallas_call(
        _ada_kernel,
        out_shape=jax.ShapeDtypeStruct((depth, MOD_ROWS, n), F32),
        grid=(depth, n // tn),
        in_specs=[
            pl.BlockSpec((MOD_ROWS, D_MODEL), lambda l, j: (0, 0)),
            pl.BlockSpec((None, D_MODEL, tn), lambda l, j: (l, 0, j)),
            pl.BlockSpec((None, 1, tn), lambda l, j: (l, 0, j)),
        ],
        out_specs=pl.BlockSpec((None, MOD_ROWS, tn), lambda l, j: (l, 0, j)),
        compiler_params=_cparams(("parallel", "parallel")),
        name="ada",
    )(cvec, ada_w, ada_b.reshape(depth, 1, n))


def _proj_kernel(*refs, latent):
    if latent:
        (x_ref, gn_ref, sh_ref, sc_ref, w_ref, qn_ref, kn_ref, cos_ref, sa_ref, sb_ref,
         gkf_ref, gkfb_ref, gkb_ref, gkbb_ref,
         q_ref, k_ref, v_ref, bq_ref, bk_ref, bv_ref, gate_ref, gf_ref, gb_ref) = refs
    else:
        (x_ref, gn_ref, sh_ref, sc_ref, w_ref, kn_ref,
         gkf_ref, gkfb_ref, gkb_ref, gkbb_ref,
         k_ref, v_ref, bk_ref, bv_ref, gf_ref, gb_ref) = refs

    xt = x_ref[...]
    h = _rms(xt, gn_ref[...]) * (1.0 + sc_ref[...]) + sh_ref[...]
    hb = h.astype(BF16)

    def proj(off, width):
        return _dot(hb, w_ref[:, off:off + width])

    def rope(t):
        return (t * cos_ref[...] + pltpu.roll(t, HEAD_DIM - HEAD_DIM // 4, 1) * sa_ref[...]
                + pltpu.roll(t, HEAD_DIM // 4, 1) * sb_ref[...])

    if latent:
        for half in range(2):
            seg = proj(OFF_AQ + half * (A_Q_W // 2), A_Q_W // 2)
            for j in range(A_Q_HEADS // 2):
                t = _rms(seg[:, j * HEAD_DIM:(j + 1) * HEAD_DIM], qn_ref[...])
                t = rope(t) * HEAD_DIM ** -0.5
                hd = half * (A_Q_HEADS // 2) + j
                q_ref[:, hd * HEAD_DIM:(hd + 1) * HEAD_DIM] = t.astype(BF16)

    seg = proj(OFF_AK, 2 * A_KV_W)
    for j in range(A_KV_HEADS):
        t = _rms(seg[:, j * HEAD_DIM:(j + 1) * HEAD_DIM], kn_ref[...])
        if latent:
            t = rope(t)
        k_ref[:, j * HEAD_DIM:(j + 1) * HEAD_DIM] = t.astype(BF16)
    v_ref[...] = seg[:, A_KV_W:].astype(BF16)

    if latent:
        bq_ref[...] = proj(OFF_BQ, B_QK_W) * B_DK ** -0.5
    bk_ref[...] = proj(OFF_BK, B_QK_W)
    for half in range(2):
        bv_ref[:, half * (B_V_W // 2):(half + 1) * (B_V_W // 2)] = proj(OFF_BV + half * (B_V_W // 2), B_V_W // 2)
    if latent:
        for half in range(2):
            gate_ref[:, half * (B_V_W // 2):(half + 1) * (B_V_W // 2)] = proj(
                OFF_GATE + half * (B_V_W // 2), B_V_W // 2)

    lr = proj(OFF_LR, LANE).astype(BF16)

    def log_decay(gk_ref, gkb_ref, out_ref):
        z = _dot(lr, gk_ref[...]) + gkb_ref[...]
        out_ref[...] = (jnp.minimum(z, 0.0) - jnp.log1p(jnp.exp(-jnp.abs(z)))) / B_GATE_NORM

    log_decay(gkf_ref, gkfb_ref, gf_ref)
    log_decay(gkb_ref, gkbb_ref, gb_ref)


def _proj(x2, gn, mod3, mod_row_fn, tiles_per_batch, tm, w_in, qn, kn, rope_tabs, gkf, gkfb, gkb, gkbb, latent):
    t_tok = x2.shape[0]
    row = lambda width: pl.BlockSpec((1, width), lambda i: (0, 0))
    tok = lambda width: pl.BlockSpec((tm, width), lambda i: (i, 0))
    mod = lambda col: pl.BlockSpec((None, 1, D_MODEL), lambda i: (mod_row_fn(i), 0, col))
    in_specs = [tok(D_MODEL), row(D_MODEL), mod(0), mod(1),
                pl.BlockSpec((D_MODEL, IN_COLS_PAD), lambda i: (0, 0), pipeline_mode=pl.Buffered(1))]
    args = [x2, gn, mod3, mod3, w_in]
    if latent:
        in_specs += [row(HEAD_DIM), row(HEAD_DIM)]
        args += [qn, kn]
        in_specs += [pl.BlockSpec((tm, HEAD_DIM), lambda i: (i % tiles_per_batch, 0))] * 3
        args += list(rope_tabs)
    else:
        in_specs += [row(HEAD_DIM)]
        args += [kn]
    in_specs += [pl.BlockSpec((LANE, B_QK_W), lambda i: (0, 0)), row(B_QK_W)] * 2
    args += [gkf, gkfb, gkb, gkbb]

    def out(width, dtype):
        return jax.ShapeDtypeStruct((t_tok, width), dtype), tok(width)

    if latent:
        outs = [out(A_Q_W, BF16), out(A_KV_W, BF16), out(A_KV_W, BF16), out(B_QK_W, F32), out(B_QK_W, F32),
                out(B_V_W, F32), out(B_V_W, F32), out(B_QK_W, F32), out(B_QK_W, F32)]
    else:
        outs = [out(A_KV_W, BF16), out(A_KV_W, BF16), out(B_QK_W, F32), out(B_V_W, F32),
                out(B_QK_W, F32), out(B_QK_W, F32)]
    return pl.pallas_call(
        functools.partial(_proj_kernel, latent=latent),
        out_shape=[o[0] for o in outs],
        grid=(t_tok // tm,),
        in_specs=in_specs,
        out_specs=[o[1] for o in outs],
        compiler_params=_cparams(("parallel",)),
        name="proj_lat" if latent else "proj_ctx",
    )(*args)


def _attn_kernel(sink_ref, q_ref, k_ref, v_ref, kc_ref, vc_ref, o_ref, *, n_lat):
    kvh = pl.program_id(1)
    n = pl.program_id(2)
    n_win = 3 * BLOCK_Q
    start = jnp.clip(n * BLOCK_Q - BLOCK_Q, 0, n_lat - n_win)
    start = pl.multiple_of(start, BLOCK_Q)
    kw = k_ref[pl.ds(start, n_win), :]
    vw = v_ref[pl.ds(start, n_win), :]
    kc = kc_ref[...]
    vc = vc_ref[...]
    qpos = n * BLOCK_Q + lax.broadcasted_iota(jnp.int32, (BLOCK_Q, n_win), 0)
    kpos = start + lax.broadcasted_iota(jnp.int32, (BLOCK_Q, n_win), 1)
    valid = jnp.abs(qpos - kpos) <= WINDOW
    for g in range(A_GROUP):
        q = q_ref[:, g * HEAD_DIM:(g + 1) * HEAD_DIM]
        s_lat = jnp.where(valid, _dot_nt(q, kw), NEG_INF)
        s_ctx = _dot_nt(q, kc)
        sk = sink_ref[kvh * A_GROUP + g]
        m = jnp.maximum(jnp.maximum(jnp.max(s_lat, axis=-1, keepdims=True),
                                    jnp.max(s_ctx, axis=-1, keepdims=True)), sk)
        p_lat = jnp.exp(s_lat - m)
        p_ctx = jnp.exp(s_ctx - m)
        den = (jnp.sum(p_lat, axis=-1, keepdims=True) + jnp.sum(p_ctx, axis=-1, keepdims=True)
               + jnp.exp(sk - m))
        o = _dot(p_lat.astype(BF16), vw) + _dot(p_ctx.astype(BF16), vc)
        o_ref[:, g * HEAD_DIM:(g + 1) * HEAD_DIM] = (o / den).astype(BF16)


def _attn(sink, q, k, v, kc, vc, bsz, n_lat, n_ctx):
    nb = n_lat // BLOCK_Q
    gw = A_GROUP * HEAD_DIM
    return pl.pallas_call(
        functools.partial(_attn_kernel, n_lat=n_lat),
        out_shape=jax.ShapeDtypeStruct((bsz * n_lat, A_Q_W), BF16),
        grid=(bsz, A_KV_HEADS, nb),
        in_specs=[
            pl.BlockSpec(memory_space=pltpu.SMEM),
            pl.BlockSpec((BLOCK_Q, gw), lambda b, h, n: (b * nb + n, h)),
            pl.BlockSpec((n_lat, HEAD_DIM), lambda b, h, n: (b, h)),
            pl.BlockSpec((n_lat, HEAD_DIM), lambda b, h, n: (b, h)),
            pl.BlockSpec((n_ctx, HEAD_DIM), lambda b, h, n: (b, h)),
            pl.BlockSpec((n_ctx, HEAD_DIM), lambda b, h, n: (b, h)),
        ],
        out_specs=pl.BlockSpec((BLOCK_Q, gw), lambda b, h, n: (b * nb + n, h)),
        compiler_params=_cparams(("parallel", "parallel", "parallel")),
        name="attn",
    )(sink, q, k, v, kc, vc)


def _gla_kernel(q_ref, k_ref, v_ref, gf_ref, gb_ref, kc_ref, vc_ref, gfc_ref, gbc_ref, o_ref,
                sf_ref, sb_ref, *, n_lat, n_ctx):
    c = B_CHUNK
    nc = n_lat // c

    def tri(nn, fn):
        r = lax.broadcasted_iota(jnp.int32, (nn, nn), 0)
        s = lax.broadcasted_iota(jnp.int32, (nn, nn), 1)
        return fn(r, s)

    kc = kc_ref[...]
    vcb = vc_ref[...].astype(BF16)
    up_strict = tri(n_ctx, lambda r, s: s > r).astype(BF16)
    lo_strict = tri(n_ctx, lambda r, s: s < r).astype(BF16)
    kd_f = (kc * jnp.exp(_tri_sum(up_strict, gfc_ref[...]))).astype(BF16)
    kd_b = (kc * jnp.exp(_tri_sum(lo_strict, gbc_ref[...]))).astype(BF16)
    sf_ref[...] = _dot_tn(vcb, kd_f)
    sb_ref[...] = _dot_tn(vcb, kd_b)

    o_ref[...] = jnp.zeros_like(o_ref)
    lo_mask = tri(c, lambda r, s: s <= r)
    up_mask = tri(c, lambda r, s: s >= r)
    lo_incl = lo_mask.astype(BF16)
    up_incl = up_mask.astype(BF16)

    def chunk(rows, g_ref, tri_incl, mask, last_row, s_ref):
        g = g_ref[rows, :]
        b = _tri_sum(tri_incl, g)
        b_last = b[last_row:last_row + 1, :]
        q = q_ref[rows, :]
        k = k_ref[rows, :]
        vb = v_ref[rows, :].astype(BF16)
        qe = (q * jnp.exp(b)).astype(BF16)
        ke = (k * jnp.exp(-b)).astype(BF16)
        kd = (k * jnp.exp(b_last - b)).astype(BF16)
        a = jnp.where(mask, _dot_nt(qe, ke), 0.0).astype(BF16)
        st = s_ref[...]
        o = _dot(a, vb) + _dot_nt(qe, st.astype(BF16))
        s_ref[...] = st * jnp.exp(b_last) + _dot_tn(vb, kd)
        o_ref[rows, :] += o

    def body(i, carry):
        rf = pl.ds(pl.multiple_of(i * c, c), c)
        rb = pl.ds(pl.multiple_of((nc - 1 - i) * c, c), c)
        chunk(rf, gf_ref, lo_incl, lo_mask, c - 1, sf_ref)
        chunk(rb, gb_ref, up_incl, up_mask, 0, sb_ref)
        return carry

    lax.fori_loop(0, nc, body, 0)


def _gla(bq, bk, bv, gf, gb, bkc, bvc, gfc, gbc, bsz, n_lat, n_ctx):
    lat = lambda width: pl.BlockSpec((n_lat, width), lambda b, h: (b, h))
    cx = lambda width: pl.BlockSpec((n_ctx, width), lambda b, h: (b, h))
    return pl.pallas_call(
        functools.partial(_gla_kernel, n_lat=n_lat, n_ctx=n_ctx),
        out_shape=jax.ShapeDtypeStruct((bsz * n_lat, B_V_W), F32),
        grid=(bsz, B_HEADS),
        in_specs=[lat(B_DK), lat(B_DK), lat(B_DV), lat(B_DK), lat(B_DK),
                  cx(B_DK), cx(B_DV), cx(B_DK), cx(B_DK)],
        out_specs=lat(B_DV),
        scratch_shapes=[pltpu.VMEM((B_DV, B_DK), F32), pltpu.VMEM((B_DV, B_DK), F32)],
        compiler_params=_cparams(("parallel", "parallel")),
        name="gla",
    )(bq, bk, bv, gf, gb, bkc, bvc, gfc, gbc)


def _mixout_kernel(oa_ref, ob_ref, gate_ref, gn_ref, wa_ref, wb_ref, x_ref, g1_ref, o_ref):
    y = _dot(oa_ref[...], wa_ref[...])
    for hd in range(B_HEADS):
        cols = slice(hd * B_DV, (hd + 1) * B_DV)
        gt = gate_ref[:, cols]
        t = _rms(ob_ref[:, cols], gn_ref[...]) * (gt * jax.nn.sigmoid(gt))
        y = y + _dot(t.astype(BF16), wb_ref[cols, :])
    o_ref[...] = x_ref[...] + g1_ref[...] * y


def _mixout(oa, ob, gate, gla_norm, w_a, w_b, x2, mod3, mod_row_fn, tm):
    t_tok = x2.shape[0]
    tok = lambda width: pl.BlockSpec((tm, width), lambda i: (i, 0))
    const = lambda shape: pl.BlockSpec(shape, lambda i: (0, 0))
    return pl.pallas_call(
        _mixout_kernel,
        out_shape=jax.ShapeDtypeStruct((t_tok, D_MODEL), F32),
        grid=(t_tok // tm,),
        in_specs=[tok(A_Q_W), tok(B_V_W), tok(B_V_W), const((1, B_DV)),
                  const((A_Q_W, D_MODEL)), const((B_V_W, D_MODEL)), tok(D_MODEL),
                  pl.BlockSpec((None, 1, D_MODEL), lambda i: (mod_row_fn(i), 0, 2))],
        out_specs=tok(D_MODEL),
        compiler_params=_cparams(("parallel",)),
        name="mixout",
    )(oa, ob, gate, gla_norm, w_a, w_b, x2, mod3)


def _mlp_kernel(x_ref, gn_ref, sh_ref, sc_ref, g2_ref, w1_ref, w2_ref, o_ref, h_ref):
    j = pl.program_id(1)

    @pl.when(j == 0)
    def _():
        h = _rms(x_ref[...], gn_ref[...]) * (1.0 + sc_ref[...]) + sh_ref[...]
        h_ref[...] = h.astype(BF16)

    u = jnp.maximum(_dot(h_ref[...], w1_ref[...]), 0.0)
    part = _dot((u * u).astype(BF16), w2_ref[...])

    @pl.when(j == 0)
    def _():
        o_ref[...] = part

    @pl.when(j > 0)
    def _():
        o_ref[...] += part

    @pl.when(j == pl.num_programs(1) - 1)
    def _():
        o_ref[...] = x_ref[...] + g2_ref[...] * o_ref[...]


def _mlp(x2, gn, mod3, mod_row_fn, w1, w2, layer, tm, tf):
    t_tok = x2.shape[0]
    mod = lambda col: pl.BlockSpec((None, 1, D_MODEL), lambda i, j: (mod_row_fn(i), 0, col))
    return pl.pallas_call(
        _mlp_kernel,
        out_shape=jax.ShapeDtypeStruct((t_tok, D_MODEL), F32),
        grid=(t_tok // tm, D_FF // tf),
        in_specs=[pl.BlockSpec((tm, D_MODEL), lambda i, j: (i, 0)),
                  pl.BlockSpec((1, D_MODEL), lambda i, j: (0, 0)),
                  mod(3), mod(4), mod(5),
                  pl.BlockSpec((None, D_MODEL, tf), lambda i, j: (layer, 0, j)),
                  pl.BlockSpec((None, tf, D_MODEL), lambda i, j: (layer, j, 0))],
        out_specs=pl.BlockSpec((tm, D_MODEL), lambda i, j: (i, 0)),
        scratch_shapes=[pltpu.VMEM((tm, D_MODEL), BF16)],
        compiler_params=_cparams(("parallel", "arbitrary")),
        name="mlp",
    )(x2, gn, mod3, mod3, mod3, w1, w2)


def _fch_kernel(x_ref, gn_ref, sh_ref, sc_ref, cs_ref, o_ref):
    h = _rms(x_ref[...], gn_ref[...]) * (1.0 + sc_ref[...]) + sh_ref[...]
    hb = h.astype(BF16)
    for g in range(C_GROUPS):
        cols = slice(g * C_GROUP_DIM, (g + 1) * C_GROUP_DIM)
        y = _dot(hb[:, cols], cs_ref[...])
        o_ref[0, :, cols] = y[:, :C_GROUP_DIM].astype(BF16)
        o_ref[1, :, cols] = y[:, C_GROUP_DIM:].astype(BF16)


def _fch(x2, gn, mod3, mod_row_fn, cs_ch, bsz, n_lat, tm):
    tpb = n_lat // tm
    mod = lambda col: pl.BlockSpec((None, 1, D_MODEL), lambda i: (mod_row_fn(i), 0, col))
    return pl.pallas_call(
        _fch_kernel,
        out_shape=jax.ShapeDtypeStruct((bsz, 2, n_lat, D_MODEL), BF16),
        grid=(bsz * tpb,),
        in_specs=[pl.BlockSpec((tm, D_MODEL), lambda i: (i, 0)),
                  pl.BlockSpec((1, D_MODEL), lambda i: (0, 0)),
                  mod(0), mod(1),
                  pl.BlockSpec((C_GROUP_DIM, 2 * C_GROUP_DIM), lambda i: (0, 0))],
        out_specs=pl.BlockSpec((None, 2, tm, D_MODEL), lambda i: (i // tpb, 0, i % tpb, 0)),
        compiler_params=_cparams(("parallel",)),
        name="fch",
    )(x2, gn, mod3, mod3, cs_ch)


def _fpos_kernel(cs_ref, y_ref, o_ref, *, scale):
    o_ref[...] = (_dot(cs_ref[...], y_ref[...]) * scale).astype(BF16)


def _fpos(cs_pos, ycs, bsz, n_lat, tk, td):
    nk = n_lat // tk
    nd = D_MODEL // td
    scale = float((n_lat * C_GROUP_DIM) ** -0.5)
    return pl.pallas_call(
        functools.partial(_fpos_kernel, scale=scale),
        out_shape=jax.ShapeDtypeStruct((bsz * n_lat, D_MODEL), BF16),
        grid=(bsz, nd, nk),
        in_specs=[pl.BlockSpec((tk, 2 * n_lat), lambda b, d, k: (k, 0)),
                  pl.BlockSpec((None, 2 * n_lat, td), lambda b, d, k: (b, 0, d))],
        out_specs=pl.BlockSpec((tk, td), lambda b, d, k: (b * nk + k, d)),
        compiler_params=_cparams(("parallel", "parallel", "parallel")),
        name="fpos",
    )(cs_pos, ycs)


def _linres_kernel(z_ref, w_ref, b_ref, x_ref, g1_ref, o_ref):
    y = _dot(z_ref[...], w_ref[...]) + b_ref[...]
    o_ref[...] = x_ref[...] + g1_ref[...] * y


def _linres(z, w, bias, x2, mod3, mod_row_fn, tm):
    t_tok = x2.shape[0]
    tok = lambda width: pl.BlockSpec((tm, width), lambda i: (i, 0))
    return pl.pallas_call(
        _linres_kernel,
        out_shape=jax.ShapeDtypeStruct((t_tok, D_MODEL), F32),
        grid=(t_tok // tm,),
        in_specs=[tok(D_MODEL), pl.BlockSpec((D_MODEL, D_MODEL), lambda i: (0, 0)),
                  pl.BlockSpec((1, D_MODEL), lambda i: (0, 0)), tok(D_MODEL),
                  pl.BlockSpec((None, 1, D_MODEL), lambda i: (mod_row_fn(i), 0, 2))],
        out_specs=tok(D_MODEL),
        compiler_params=_cparams(("parallel",)),
        name="linres",
    )(z, w, bias, x2, mod3)


def _rope_tables(n_lat):
    t = jnp.arange(n_lat)
    row = (t // GRID_W).astype(F32)
    col = (t % GRID_W).astype(F32)
    n_freq = HEAD_DIM // 4
    inv_freq = ROPE_BASE ** (-jnp.arange(n_freq, dtype=F32) / n_freq)
    ang_r, ang_c = row[:, None] * inv_freq, col[:, None] * inv_freq
    cr, sr, cc, sc = jnp.cos(ang_r), jnp.sin(ang_r), jnp.cos(ang_c), jnp.sin(ang_c)
    zero = jnp.zeros_like(sr)
    cos = jnp.concatenate([cr, cr, cc, cc], axis=-1)
    sin_a = jnp.concatenate([-sr, zero, -sc, zero], axis=-1)
    sin_b = jnp.concatenate([zero, sr, zero, sc], axis=-1)
    return cos, sin_a, sin_b


def _dft_tables(n_lat):
    def cs(n):
        idx = np.arange(n, dtype=np.int64)
        ang = 2.0 * np.pi * ((idx[:, None] * idx[None, :]) % n).astype(np.float64) / n
        return np.cos(ang), np.sin(ang)

    cm, sm = cs(C_GROUP_DIM)
    cn, sn = cs(n_lat)
    cs_ch = jnp.asarray(np.concatenate([cm, sm], axis=1), dtype=F32)
    cs_pos = jnp.asarray(np.concatenate([cn, -sn], axis=1), dtype=F32)
    return cs_ch.astype(BF16), cs_pos.astype(BF16)


def kernel(x, c, ctx, c_ctx, ada_w, ada_b, norm_mix, norm_mlp, mlp_w1, mlp_w2, ab_w_in, ab_q_norm, ab_k_norm,
           ab_sink, ab_gk_f, ab_gk_f_bias, ab_gk_b, ab_gk_b_bias, ab_gla_norm, ab_w_out, c_w_out, c_b_out):
    bsz, n_lat, _ = x.shape
    n_ctx = ctx.shape[1]
    depth = ada_w.shape[0]
    assert depth == 2 and bsz < MOD_ROWS
    t_tok = bsz * n_lat

    cvec = jnp.concatenate([c, c_ctx[None, :], jnp.zeros((MOD_ROWS - bsz - 1, D_MODEL), F32)], axis=0)
    mod = _ada(cvec, ada_w, ada_b)
    mod3 = mod.reshape(depth * MOD_ROWS, 1, 6 * D_MODEL)

    w1 = mlp_w1.astype(BF16)
    w2 = mlp_w2.astype(BF16)
    x2 = x.reshape(t_tok, D_MODEL)

    tm = min(256, n_lat)
    tpb = n_lat // tm
    lat_row0 = lambda i: i // tpb
    w_in = jnp.pad(ab_w_in[0], ((0, 0), (0, IN_COLS_PAD - IN_COLS))).astype(BF16)
    gn_mix0 = norm_mix[0].reshape(1, D_MODEL)
    qn = ab_q_norm[0].reshape(1, HEAD_DIM)
    kn = ab_k_norm[0].reshape(1, HEAD_DIM)
    gkf = jnp.pad(ab_gk_f[0], ((0, LANE - B_GATE_RANK), (0, 0))).astype(BF16)
    gkb = jnp.pad(ab_gk_b[0], ((B_GATE_RANK, LANE - 2 * B_GATE_RANK), (0, 0))).astype(BF16)
    gkfb = ab_gk_f_bias[0].reshape(1, B_QK_W)
    gkbb = ab_gk_b_bias[0].reshape(1, B_QK_W)
    q, k, v, bq, bk, bv, gate, gf, gb = _proj(
        x2, gn_mix0, mod3, lat_row0, tpb, tm, w_in, qn, kn, _rope_tables(n_lat), gkf, gkfb, gkb, gkbb, True)
    tmc = min(256, n_ctx)
    kc, vc, bkc, bvc, gfc, gbc = _proj(
        ctx.reshape(bsz * n_ctx, D_MODEL), gn_mix0, mod3, lambda i: bsz, n_ctx // tmc, tmc, w_in, None, kn, None,
        gkf, gkfb, gkb, gkbb, False)

    oa = _attn(ab_sink[0], q, k, v, kc, vc, bsz, n_lat, n_ctx)
    ob = _gla(bq, bk, bv, gf, gb, bkc, bvc, gfc, gbc, bsz, n_lat, n_ctx)

    w_out = ab_w_out[0].astype(BF16)
    tmo = min(512, n_lat)
    tpbo = n_lat // tmo
    x2 = _mixout(oa, ob, gate, ab_gla_norm[0].reshape(1, B_DV), w_out[:A_Q_W], w_out[A_Q_W:], x2, mod3,
                 lambda i: i // tpbo, tmo)
    x2 = _mlp(x2, norm_mlp[0].reshape(1, D_MODEL), mod3, lambda i: i // tpbo, w1, w2, 0, tmo, 512)

    cs_ch, cs_pos = _dft_tables(n_lat)
    row1 = lambda i: MOD_ROWS + i // tpbo
    ycs = _fch(x2, norm_mix[1].reshape(1, D_MODEL), mod3, row1, cs_ch, bsz, n_lat, tmo)
    z = _fpos(cs_pos, ycs.reshape(bsz, 2 * n_lat, D_MODEL), bsz, n_lat, min(512, n_lat), 1024)
    x2 = _linres(z, c_w_out[0].astype(BF16), c_b_out[0].reshape(1, D_MODEL), x2, mod3, row1, tmo)
    x2 = _mlp(x2, norm_mlp[1].reshape(1, D_MODEL), mod3, row1, w1, w2, 1, tmo, 512)
    return x2.reshape(bsz, n_lat, D_MODEL)
```

```python
import functools

import numpy as np
import jax
import jax.numpy as jnp
from jax import lax
from jax.experimental import pallas as pl
from jax.experimental.pallas import tpu as pltpu

D_MODEL = 2048
GRID_W = 64
EPS = 1e-6
NEG_INF = -1e30
HEAD_DIM = 128
A_Q_HEADS = 8
A_KV_HEADS = 2
A_GROUP = A_Q_HEADS // A_KV_HEADS
WINDOW = 128
BLOCK_Q = 128
ROPE_BASE = 10000.0
B_HEADS = 4
B_DV = 256
B_DK = 128
B_GATE_RANK = 16
B_GATE_NORM = 16.0
B_CHUNK = 64
GLA_GROUP = 4
C_GROUPS = 8
C_GROUP_DIM = D_MODEL // C_GROUPS
D_FF = 4 * D_MODEL

A_Q_W = A_Q_HEADS * HEAD_DIM
A_KV_W = A_KV_HEADS * HEAD_DIM
B_QK_W = B_HEADS * B_DK
B_V_W = B_HEADS * B_DV
OFF_AQ = 0
OFF_AK = OFF_AQ + A_Q_W
OFF_AV = OFF_AK + A_KV_W
OFF_BQ = OFF_AV + A_KV_W
OFF_BK = OFF_BQ + B_QK_W
OFF_BV = OFF_BK + B_QK_W
OFF_GATE = OFF_BV + B_V_W
OFF_LR = OFF_GATE + B_V_W
IN_COLS = OFF_LR + 2 * B_GATE_RANK
LANE = 128
IN_COLS_PAD = OFF_LR + LANE
MOD_ROWS = 16

VMEM_LIMIT = 56 * 1024 * 1024

BF16 = jnp.bfloat16
F32 = jnp.float32


def _cparams(sem):
    return pltpu.CompilerParams(dimension_semantics=sem, vmem_limit_bytes=VMEM_LIMIT)


def _dot(a, b):
    return jnp.dot(a, b, preferred_element_type=F32)


def _dot_nt(a, b):
    return lax.dot_general(a, b, (((1,), (1,)), ((), ())), preferred_element_type=F32)


def _dot_tn(a, b):
    return lax.dot_general(a, b, (((0,), (0,)), ((), ())), preferred_element_type=F32)


def _tri_sum(tri, g):
    g_hi = g.astype(BF16)
    g_lo = (g - g_hi.astype(F32)).astype(BF16)
    return _dot(tri, g_hi) + _dot(tri, g_lo)


def _rms(xf, gain):
    return xf * lax.rsqrt(jnp.mean(xf * xf, axis=-1, keepdims=True) + EPS) * gain


def _ada_kernel(c_ref, w_ref, b_ref, o_ref):
    cv = c_ref[...]
    s = (cv * jax.nn.sigmoid(cv)).astype(BF16)
    o_ref[...] = _dot(s, w_ref[...].astype(BF16)) + b_ref[...]


def _ada(cvec, ada_w, ada_b):
    depth = ada_w.shape[0]
    n = ada_w.shape[2]
    tn = 1024
    return pl.pallas_call(
        _ada_kernel,
        out_shape=jax.ShapeDtypeStruct((depth, MOD_ROWS, n), F32),
        grid=(depth, n // tn),
        in_specs=[
            pl.BlockSpec((MOD_ROWS, D_MODEL), lambda l, j: (0, 0)),
            pl.BlockSpec((None, D_MODEL, tn), lambda l, j: (l, 0, j)),
            pl.BlockSpec((None, 1, tn), lambda l, j: (l, 0, j)),
        ],
        out_specs=pl.BlockSpec((None, MOD_ROWS, tn), lambda l, j: (l, 0, j)),
        compiler_params=_cparams(("parallel", "parallel")),
        name="ada",
    )(cvec, ada_w, ada_b.reshape(depth, 1, n))


def _proj_kernel(*refs, latent):
    if latent:
        (x_ref, gn_ref, sh_ref, sc_ref, w_ref, qn_ref, kn_ref, cos_ref, sa_ref, sb_ref,
         gkf_ref, gkfb_ref, gkb_ref, gkbb_ref,
         q_ref, k_ref, v_ref, bq_ref, bk_ref, bv_ref, gate_ref, gf_ref, gb_ref) = refs
    else:
        (x_ref, gn_ref, sh_ref, sc_ref, w_ref, kn_ref,
         gkf_ref, gkfb_ref, gkb_ref, gkbb_ref,
         k_ref, v_ref, bk_ref, bv_ref, gf_ref, gb_ref) = refs

    xt = x_ref[...]
    h = _rms(xt, gn_ref[...]) * (1.0 + sc_ref[...]) + sh_ref[...]
    hb = h.astype(BF16)

    def proj(off, width):
        return _dot(hb, w_ref[:, off:off + width])

    def rope(t):
        return (t * cos_ref[...] + pltpu.roll(t, HEAD_DIM - HEAD_DIM // 4, 1) * sa_ref[...]
                + pltpu.roll(t, HEAD_DIM // 4, 1) * sb_ref[...])

    if latent:
        for half in range(2):
            seg = proj(OFF_AQ + half * (A_Q_W // 2), A_Q_W // 2)
            for j in range(A_Q_HEADS // 2):
                t = _rms(seg[:, j * HEAD_DIM:(j + 1) * HEAD_DIM], qn_ref[...])
                t = rope(t) * HEAD_DIM ** -0.5
                hd = half * (A_Q_HEADS // 2) + j
                q_ref[:, hd * HEAD_DIM:(hd + 1) * HEAD_DIM] = t.astype(BF16)

    seg = proj(OFF_AK, 2 * A_KV_W)
    for j in range(A_KV_HEADS):
        t = _rms(seg[:, j * HEAD_DIM:(j + 1) * HEAD_DIM], kn_ref[...])
        if latent:
            t = rope(t)
        k_ref[:, j * HEAD_DIM:(j + 1) * HEAD_DIM] = t.astype(BF16)
    v_ref[...] = seg[:, A_KV_W:].astype(BF16)

    if latent:
        bq_ref[...] = proj(OFF_BQ, B_QK_W) * B_DK ** -0.5
    bk_ref[...] = proj(OFF_BK, B_QK_W)
    for half in range(2):
        bv_ref[:, half * (B_V_W // 2):(half + 1) * (B_V_W // 2)] = proj(OFF_BV + half * (B_V_W // 2), B_V_W // 2)
    if latent:
        for half in range(2):
            gate_ref[:, half * (B_V_W // 2):(half + 1) * (B_V_W // 2)] = proj(
                OFF_GATE + half * (B_V_W // 2), B_V_W // 2)

    lr = proj(OFF_LR, LANE).astype(BF16)

    def log_decay(gk_ref, gkb_ref, out_ref):
        z = _dot(lr, gk_ref[...]) + gkb_ref[...]
        out_ref[...] = (jnp.minimum(z, 0.0) - jnp.log1p(jnp.exp(-jnp.abs(z)))) / B_GATE_NORM

    log_decay(gkf_ref, gkfb_ref, gf_ref)
    log_decay(gkb_ref, gkbb_ref, gb_ref)


def _proj(x2, gn, mod3, mod_row_fn, tiles_per_batch, tm, w_in, qn, kn, rope_tabs, gkf, gkfb, gkb, gkbb, latent):
    t_tok = x2.shape[0]
    row = lambda width: pl.BlockSpec((1, width), lambda i: (0, 0))
    tok = lambda width: pl.BlockSpec((tm, width), lambda i: (i, 0))
    mod = lambda col: pl.BlockSpec((None, 1, D_MODEL), lambda i: (mod_row_fn(i), 0, col))
    in_specs = [tok(D_MODEL), row(D_MODEL), mod(0), mod(1),
                pl.BlockSpec((D_MODEL, IN_COLS_PAD), lambda i: (0, 0), pipeline_mode=pl.Buffered(1))]
    args = [x2, gn, mod3, mod3, w_in]
    if latent:
        in_specs += [row(HEAD_DIM), row(HEAD_DIM)]
        args += [qn, kn]
        in_specs += [pl.BlockSpec((tm, HEAD_DIM), lambda i: (i % tiles_per_batch, 0))] * 3
        args += list(rope_tabs)
    else:
        in_specs += [row(HEAD_DIM)]
        args += [kn]
    in_specs += [pl.BlockSpec((LANE, B_QK_W), lambda i: (0, 0)), row(B_QK_W)] * 2
    args += [gkf, gkfb, gkb, gkbb]

    def out(width, dtype):
        return jax.ShapeDtypeStruct((t_tok, width), dtype), tok(width)

    if latent:
        outs = [out(A_Q_W, BF16), out(A_KV_W, BF16), out(A_KV_W, BF16), out(B_QK_W, F32), out(B_QK_W, F32),
                out(B_V_W, F32), out(B_V_W, F32), out(B_QK_W, F32), out(B_QK_W, F32)]
    else:
        outs = [out(A_KV_W, BF16), out(A_KV_W, BF16), out(B_QK_W, F32), out(B_V_W, F32),
                out(B_QK_W, F32), out(B_QK_W, F32)]
    return pl.pallas_call(
        functools.partial(_proj_kernel, latent=latent),
        out_shape=[o[0] for o in outs],
        grid=(t_tok // tm,),
        in_specs=in_specs,
        out_specs=[o[1] for o in outs],
        compiler_params=_cparams(("parallel",)),
        name="proj_lat" if latent else "proj_ctx",
    )(*args)


def _attn_kernel(sink_ref, q_ref, k_ref, v_ref, kc_ref, vc_ref, o_ref, *, n_lat):
    kvh = pl.program_id(1)
    n = pl.program_id(2)
    n_win = 3 * BLOCK_Q
    start = jnp.clip(n * BLOCK_Q - BLOCK_Q, 0, n_lat - n_win)
    start = pl.multiple_of(start, BLOCK_Q)
    kw = k_ref[pl.ds(start, n_win), :]
    vw = v_ref[pl.ds(start, n_win), :]
    kc = kc_ref[...]
    vc = vc_ref[...]
    qpos = n * BLOCK_Q + lax.broadcasted_iota(jnp.int32, (BLOCK_Q, n_win), 0)
    kpos = start + lax.broadcasted_iota(jnp.int32, (BLOCK_Q, n_win), 1)
    valid = jnp.abs(qpos - kpos) <= WINDOW
    heads = range(A_GROUP)
    q = jnp.concatenate([q_ref[:, g * HEAD_DIM:(g + 1) * HEAD_DIM] for g in heads], axis=0)
    s_lat = _dot_nt(q, kw)
    s_lat = jnp.concatenate(
        [jnp.where(valid, s_lat[g * BLOCK_Q:(g + 1) * BLOCK_Q], NEG_INF) for g in heads], axis=0)
    s_ctx = _dot_nt(q, kc)
    sk = jnp.concatenate(
        [jnp.full((BLOCK_Q, 1), sink_ref[kvh * A_GROUP + g], F32) for g in heads], axis=0)
    m = jnp.maximum(jnp.maximum(jnp.max(s_lat, axis=-1, keepdims=True),
                                jnp.max(s_ctx, axis=-1, keepdims=True)), sk)
    p_lat = jnp.exp(s_lat - m)
    p_ctx = jnp.exp(s_ctx - m)
    den = (jnp.sum(p_lat, axis=-1, keepdims=True) + jnp.sum(p_ctx, axis=-1, keepdims=True)
           + jnp.exp(sk - m))
    o = (_dot(p_lat.astype(BF16), vw) + _dot(p_ctx.astype(BF16), vc)) * (1.0 / den)
    for g in heads:
        o_ref[:, g * HEAD_DIM:(g + 1) * HEAD_DIM] = o[g * BLOCK_Q:(g + 1) * BLOCK_Q].astype(BF16)


def _attn(sink, q, k, v, kc, vc, bsz, n_lat, n_ctx):
    nb = n_lat // BLOCK_Q
    gw = A_GROUP * HEAD_DIM
    return pl.pallas_call(
        functools.partial(_attn_kernel, n_lat=n_lat),
        out_shape=jax.ShapeDtypeStruct((bsz * n_lat, A_Q_W), BF16),
        grid=(bsz, A_KV_HEADS, nb),
        in_specs=[
            pl.BlockSpec(memory_space=pltpu.SMEM),
            pl.BlockSpec((BLOCK_Q, gw), lambda b, h, n: (b * nb + n, h)),
            pl.BlockSpec((n_lat, HEAD_DIM), lambda b, h, n: (b, h)),
            pl.BlockSpec((n_lat, HEAD_DIM), lambda b, h, n: (b, h)),
            pl.BlockSpec((n_ctx, HEAD_DIM), lambda b, h, n: (b, h)),
            pl.BlockSpec((n_ctx, HEAD_DIM), lambda b, h, n: (b, h)),
        ],
        out_specs=pl.BlockSpec((BLOCK_Q, gw), lambda b, h, n: (b * nb + n, h)),
        compiler_params=_cparams(("parallel", "parallel", "parallel")),
        name="attn",
    )(sink, q, k, v, kc, vc)


def _gla_kernel(q_ref, k_ref, v_ref, gf_ref, gb_ref, kc_ref, vc_ref, gfc_ref, gbc_ref, o_ref,
                sf_ref, sb_ref, *, n_lat, n_ctx):
    c = B_CHUNK

    def tri(nn, fn):
        r = lax.broadcasted_iota(jnp.int32, (nn, nn), 0)
        s = lax.broadcasted_iota(jnp.int32, (nn, nn), 1)
        return fn(r, s)

    kc = kc_ref[...]
    vcb = vc_ref[...].astype(BF16)
    up_strict = tri(n_ctx, lambda r, s: s > r).astype(BF16)
    lo_strict = tri(n_ctx, lambda r, s: s < r).astype(BF16)
    kd_f = (kc * jnp.exp(_tri_sum(up_strict, gfc_ref[...]))).astype(BF16)
    kd_b = (kc * jnp.exp(_tri_sum(lo_strict, gbc_ref[...]))).astype(BF16)
    sf_ref[...] = _dot_tn(vcb, kd_f)
    sb_ref[...] = _dot_tn(vcb, kd_b)

    o_ref[...] = jnp.zeros_like(o_ref)
    r = c * GLA_GROUP
    ng = n_lat // r
    same = tri(r, lambda i, j: (i // c) == (j // c))
    lo_mask = jnp.logical_and(same, tri(r, lambda i, j: j <= i))
    up_mask = jnp.logical_and(same, tri(r, lambda i, j: j >= i))
    lo_incl = lo_mask.astype(BF16)
    up_incl = up_mask.astype(BF16)
    blk_ones = same.astype(BF16)

    def group(rows, g_ref, tri_incl, mask, s_ref, order):
        g = g_ref[rows, :]
        b = _tri_sum(tri_incl, g)
        bl = _tri_sum(blk_ones, g)
        q = q_ref[rows, :]
        k = k_ref[rows, :]
        vb = v_ref[rows, :].astype(BF16)
        qe = (q * jnp.exp(b)).astype(BF16)
        ke = (k * jnp.exp(-b)).astype(BF16)
        kd = (k * jnp.exp(bl - b)).astype(BF16)
        dec = jnp.exp(bl)
        a = jnp.where(mask, _dot_nt(qe, ke), 0.0).astype(BF16)
        o_intra = _dot(a, vb)
        st = s_ref[...]
        o_inter = [None] * GLA_GROUP
        for n in order:
            sl = slice(n * c, (n + 1) * c)
            o_inter[n] = _dot_nt(qe[sl], st.astype(BF16))
            st = st * dec[n * c:n * c + 1, :] + _dot_tn(vb[sl], kd[sl])
        s_ref[...] = st
        o_ref[rows, :] += o_intra + jnp.concatenate(o_inter, axis=0)

    def body(i, carry):
        rf = pl.ds(pl.multiple_of(i * r, r), r)
        rb = pl.ds(pl.multiple_of((ng - 1 - i) * r, r), r)
        group(rf, gf_ref, lo_incl, lo_mask, sf_ref, range(GLA_GROUP))
        group(rb, gb_ref, up_incl, up_mask, sb_ref, range(GLA_GROUP - 1, -1, -1))
        return carry

    lax.fori_loop(0, ng, body, 0)


def _gla(bq, bk, bv, gf, gb, bkc, bvc, gfc, gbc, bsz, n_lat, n_ctx):
    lat = lambda width: pl.BlockSpec((n_lat, width), lambda b, h: (b, h))
    cx = lambda width: pl.BlockSpec((n_ctx, width), lambda b, h: (b, h))
    return pl.pallas_call(
        functools.partial(_gla_kernel, n_lat=n_lat, n_ctx=n_ctx),
        out_shape=jax.ShapeDtypeStruct((bsz * n_lat, B_V_W), F32),
        grid=(bsz, B_HEADS),
        in_specs=[lat(B_DK), lat(B_DK), lat(B_DV), lat(B_DK), lat(B_DK),
                  cx(B_DK), cx(B_DV), cx(B_DK), cx(B_DK)],
        out_specs=lat(B_DV),
        scratch_shapes=[pltpu.VMEM((B_DV, B_DK), F32), pltpu.VMEM((B_DV, B_DK), F32)],
        compiler_params=_cparams(("parallel", "parallel")),
        name="gla",
    )(bq, bk, bv, gf, gb, bkc, bvc, gfc, gbc)


def _mixout_kernel(oa_ref, ob_ref, gate_ref, gn_ref, wa_ref, wb_ref, x_ref, g1_ref, o_ref):
    y = _dot(oa_ref[...], wa_ref[...])
    for hd in range(B_HEADS):
        cols = slice(hd * B_DV, (hd + 1) * B_DV)
        gt = gate_ref[:, cols]
        t = _rms(ob_ref[:, cols], gn_ref[...]) * (gt * jax.nn.sigmoid(gt))
        y = y + _dot(t.astype(BF16), wb_ref[cols, :])
    o_ref[...] = x_ref[...] + g1_ref[...] * y


def _mixout(oa, ob, gate, gla_norm, w_a, w_b, x2, mod3, mod_row_fn, tm):
    t_tok = x2.shape[0]
    tok = lambda width: pl.BlockSpec((tm, width), lambda i: (i, 0))
    const = lambda shape: pl.BlockSpec(shape, lambda i: (0, 0))
    return pl.pallas_call(
        _mixout_kernel,
        out_shape=jax.ShapeDtypeStruct((t_tok, D_MODEL), F32),
        grid=(t_tok // tm,),
        in_specs=[tok(A_Q_W), tok(B_V_W), tok(B_V_W), const((1, B_DV)),
                  const((A_Q_W, D_MODEL)), const((B_V_W, D_MODEL)), tok(D_MODEL),
                  pl.BlockSpec((None, 1, D_MODEL), lambda i: (mod_row_fn(i), 0, 2))],
        out_specs=tok(D_MODEL),
        compiler_params=_cparams(("parallel",)),
        name="mixout",
    )(oa, ob, gate, gla_norm, w_a, w_b, x2, mod3)


def _mlp_kernel(x_ref, gn_ref, sh_ref, sc_ref, g2_ref, w1_ref, w2_ref, o_ref, h_ref):
    j = pl.program_id(1)

    @pl.when(j == 0)
    def _():
        xt = x_ref[...]
        h = _rms(xt, gn_ref[...] * (1.0 + sc_ref[...])) + sh_ref[...]
        h_ref[...] = h.astype(BF16)
        o_ref[...] = xt

    u = jnp.maximum(_dot(h_ref[...], w1_ref[...]), 0.0)
    o_ref[...] += g2_ref[...] * _dot((u * u).astype(BF16), w2_ref[...])


def _mlp(x2, gn, mod3, mod_row_fn, w1, w2, layer, tm, tf):
    t_tok = x2.shape[0]
    mod = lambda col: pl.BlockSpec((None, 1, D_MODEL), lambda i, j: (mod_row_fn(i), 0, col))
    return pl.pallas_call(
        _mlp_kernel,
        out_shape=jax.ShapeDtypeStruct((t_tok, D_MODEL), F32),
        grid=(t_tok // tm, D_FF // tf),
        in_specs=[pl.BlockSpec((tm, D_MODEL), lambda i, j: (i, 0)),
                  pl.BlockSpec((1, D_MODEL), lambda i, j: (0, 0)),
                  mod(3), mod(4), mod(5),
                  pl.BlockSpec((None, D_MODEL, tf), lambda i, j: (layer, 0, j)),
                  pl.BlockSpec((None, tf, D_MODEL), lambda i, j: (layer, j, 0))],
        out_specs=pl.BlockSpec((tm, D_MODEL), lambda i, j: (i, 0)),
        scratch_shapes=[pltpu.VMEM((tm, D_MODEL), BF16)],
        compiler_params=_cparams(("parallel", "arbitrary")),
        name="mlp",
    )(x2, gn, mod3, mod3, mod3, w1, w2)


def _fch_kernel(x_ref, gn_ref, sh_ref, sc_ref, cs_ref, o_ref):
    h = _rms(x_ref[...], gn_ref[...]) * (1.0 + sc_ref[...]) + sh_ref[...]
    hb = h.astype(BF16)
    for g in range(C_GROUPS):
        cols = slice(g * C_GROUP_DIM, (g + 1) * C_GROUP_DIM)
        y = _dot(hb[:, cols], cs_ref[...])
        o_ref[0, :, cols] = y[:, :C_GROUP_DIM].astype(BF16)
        o_ref[1, :, cols] = y[:, C_GROUP_DIM:].astype(BF16)


def _fch(x2, gn, mod3, mod_row_fn, cs_ch, bsz, n_lat, tm):
    tpb = n_lat // tm
    mod = lambda col: pl.BlockSpec((None, 1, D_MODEL), lambda i: (mod_row_fn(i), 0, col))
    return pl.pallas_call(
        _fch_kernel,
        out_shape=jax.ShapeDtypeStruct((bsz, 2, n_lat, D_MODEL), BF16),
        grid=(bsz * tpb,),
        in_specs=[pl.BlockSpec((tm, D_MODEL), lambda i: (i, 0)),
                  pl.BlockSpec((1, D_MODEL), lambda i: (0, 0)),
                  mod(0), mod(1),
                  pl.BlockSpec((C_GROUP_DIM, 2 * C_GROUP_DIM), lambda i: (0, 0))],
        out_specs=pl.BlockSpec((None, 2, tm, D_MODEL), lambda i: (i // tpb, 0, i % tpb, 0)),
        compiler_params=_cparams(("parallel",)),
        name="fch",
    )(x2, gn, mod3, mod3, cs_ch)


def _fpos_kernel(cs_ref, y_ref, o_ref, *, scale):
    o_ref[...] = (_dot(cs_ref[...], y_ref[...]) * scale).astype(BF16)


def _fpos(cs_pos, ycs, bsz, n_lat, tk, td):
    nk = n_lat // tk
    nd = D_MODEL // td
    scale = float((n_lat * C_GROUP_DIM) ** -0.5)
    return pl.pallas_call(
        functools.partial(_fpos_kernel, scale=scale),
        out_shape=jax.ShapeDtypeStruct((bsz * n_lat, D_MODEL), BF16),
        grid=(bsz, nd, nk),
        in_specs=[pl.BlockSpec((tk, 2 * n_lat), lambda b, d, k: (k, 0)),
                  pl.BlockSpec((None, 2 * n_lat, td), lambda b, d, k: (b, 0, d))],
        out_specs=pl.BlockSpec((tk, td), lambda b, d, k: (b * nk + k, d)),
        compiler_params=_cparams(("parallel", "parallel", "parallel")),
        name="fpos",
    )(cs_pos, ycs)


def _linres_kernel(z_ref, w_ref, b_ref, x_ref, g1_ref, o_ref):
    y = _dot(z_ref[...], w_ref[...]) + b_ref[...]
    o_ref[...] = x_ref[...] + g1_ref[...] * y


def _linres(z, w, bias, x2, mod3, mod_row_fn, tm):
    t_tok = x2.shape[0]
    tok = lambda width: pl.BlockSpec((tm, width), lambda i: (i, 0))
    return pl.pallas_call(
        _linres_kernel,
        out_shape=jax.ShapeDtypeStruct((t_tok, D_MODEL), F32),
        grid=(t_tok // tm,),
        in_specs=[tok(D_MODEL), pl.BlockSpec((D_MODEL, D_MODEL), lambda i: (0, 0)),
                  pl.BlockSpec((1, D_MODEL), lambda i: (0, 0)), tok(D_MODEL),
                  pl.BlockSpec((None, 1, D_MODEL), lambda i: (mod_row_fn(i), 0, 2))],
        out_specs=tok(D_MODEL),
        compiler_params=_cparams(("parallel",)),
        name="linres",
    )(z, w, bias, x2, mod3)


def _rope_tables(n_lat):
    t = jnp.arange(n_lat)
    row = (t // GRID_W).astype(F32)
    col = (t % GRID_W).astype(F32)
    n_freq = HEAD_DIM // 4
    inv_freq = ROPE_BASE ** (-jnp.arange(n_freq, dtype=F32) / n_freq)
    ang_r, ang_c = row[:, None] * inv_freq, col[:, None] * inv_freq
    cr, sr, cc, sc = jnp.cos(ang_r), jnp.sin(ang_r), jnp.cos(ang_c), jnp.sin(ang_c)
    zero = jnp.zeros_like(sr)
    cos = jnp.concatenate([cr, cr, cc, cc], axis=-1)
    sin_a = jnp.concatenate([-sr, zero, -sc, zero], axis=-1)
    sin_b = jnp.concatenate([zero, sr, zero, sc], axis=-1)
    return cos, sin_a, sin_b


def _dft_tables(n_lat):
    def cs(n):
        idx = np.arange(n, dtype=np.int64)
        ang = 2.0 * np.pi * ((idx[:, None] * idx[None, :]) % n).astype(np.float64) / n
        return np.cos(ang), np.sin(ang)

    cm, sm = cs(C_GROUP_DIM)
    cn, sn = cs(n_lat)
    cs_ch = jnp.asarray(np.concatenate([cm, sm], axis=1), dtype=F32)
    cs_pos = jnp.asarray(np.concatenate([cn, -sn], axis=1), dtype=F32)
    return cs_ch.astype(BF16), cs_pos.astype(BF16)


def kernel(x, c, ctx, c_ctx, ada_w, ada_b, norm_mix, norm_mlp, mlp_w1, mlp_w2, ab_w_in, ab_q_norm, ab_k_norm,
           ab_sink, ab_gk_f, ab_gk_f_bias, ab_gk_b, ab_gk_b_bias, ab_gla_norm, ab_w_out, c_w_out, c_b_out):
    bsz, n_lat, _ = x.shape
    n_ctx = ctx.shape[1]
    depth = ada_w.shape[0]
    assert depth == 2 and bsz < MOD_ROWS
    t_tok = bsz * n_lat

    cvec = jnp.concatenate([c, c_ctx[None, :], jnp.zeros((MOD_ROWS - bsz - 1, D_MODEL), F32)], axis=0)
    mod = _ada(cvec, ada_w, ada_b)
    mod3 = mod.reshape(depth * MOD_ROWS, 1, 6 * D_MODEL)

    w1 = mlp_w1.astype(BF16)
    w2 = mlp_w2.astype(BF16)
    x2 = x.reshape(t_tok, D_MODEL)

    tm = min(256, n_lat)
    tpb = n_lat // tm
    lat_row0 = lambda i: i // tpb
    w_in = jnp.pad(ab_w_in[0], ((0, 0), (0, IN_COLS_PAD - IN_COLS))).astype(BF16)
    gn_mix0 = norm_mix[0].reshape(1, D_MODEL)
    qn = ab_q_norm[0].reshape(1, HEAD_DIM)
    kn = ab_k_norm[0].reshape(1, HEAD_DIM)
    gkf = jnp.pad(ab_gk_f[0], ((0, LANE - B_GATE_RANK), (0, 0))).astype(BF16)
    gkb = jnp.pad(ab_gk_b[0], ((B_GATE_RANK, LANE - 2 * B_GATE_RANK), (0, 0))).astype(BF16)
    gkfb = ab_gk_f_bias[0].reshape(1, B_QK_W)
    gkbb = ab_gk_b_bias[0].reshape(1, B_QK_W)
    q, k, v, bq, bk, bv, gate, gf, gb = _proj(
        x2, gn_mix0, mod3, lat_row0, tpb, tm, w_in, qn, kn, _rope_tables(n_lat), gkf, gkfb, gkb, gkbb, True)
    tmc = min(256, n_ctx)
    kc, vc, bkc, bvc, gfc, gbc = _proj(
        ctx.reshape(bsz * n_ctx, D_MODEL), gn_mix0, mod3, lambda i: bsz, n_ctx // tmc, tmc, w_in, None, kn, None,
        gkf, gkfb, gkb, gkbb, False)

    oa = _attn(ab_sink[0], q, k, v, kc, vc, bsz, n_lat, n_ctx)
    ob = _gla(bq, bk, bv, gf, gb, bkc, bvc, gfc, gbc, bsz, n_lat, n_ctx)

    w_out = ab_w_out[0].astype(BF16)
    tmo = min(512, n_lat)
    tpbo = n_lat // tmo
    x2 = _mixout(oa, ob, gate, ab_gla_norm[0].reshape(1, B_DV), w_out[:A_Q_W], w_out[A_Q_W:], x2, mod3,
                 lambda i: i // tpbo, tmo)
    x2 = _mlp(x2, norm_mlp[0].reshape(1, D_MODEL), mod3, lambda i: i // tpbo, w1, w2, 0, tmo, 1024)

    cs_ch, cs_pos = _dft_tables(n_lat)
    row1 = lambda i: MOD_ROWS + i // tpbo
    ycs = _fch(x2, norm_mix[1].reshape(1, D_MODEL), mod3, row1, cs_ch, bsz, n_lat, tmo)
    z = _fpos(cs_pos, ycs.reshape(bsz, 2 * n_lat, D_MODEL), bsz, n_lat, min(512, n_lat), 1024)
    x2 = _linres(z, c_w_out[0].astype(BF16), c_b_out[0].reshape(1, D_MODEL), x2, mod3, row1, tmo)
    x2 = _mlp(x2, norm_mlp[1].reshape(1, D_MODEL), mod3, row1, w1, w2, 1, tmo, 1024)
    return x2.reshape(bsz, n_lat, D_MODEL)
```

```python
import functools

import numpy as np
import jax
import jax.numpy as jnp
from jax import lax
from jax.experimental import pallas as pl
from jax.experimental.pallas import tpu as pltpu

D_MODEL = 2048
GRID_W = 64
EPS = 1e-6
NEG_INF = -1e30
HEAD_DIM = 128
A_Q_HEADS = 8
A_KV_HEADS = 2
A_GROUP = A_Q_HEADS // A_KV_HEADS
WINDOW = 128
BLOCK_Q = 128
ROPE_BASE = 10000.0
B_HEADS = 4
B_DV = 256
B_DK = 128
B_GATE_RANK = 16
B_GATE_NORM = 16.0
B_CHUNK = 64
GLA_GROUP = 4
C_GROUPS = 8
C_GROUP_DIM = D_MODEL // C_GROUPS
D_FF = 4 * D_MODEL

A_Q_W = A_Q_HEADS * HEAD_DIM
A_KV_W = A_KV_HEADS * HEAD_DIM
B_QK_W = B_HEADS * B_DK
B_V_W = B_HEADS * B_DV
OFF_AQ = 0
OFF_AK = OFF_AQ + A_Q_W
OFF_AV = OFF_AK + A_KV_W
OFF_BQ = OFF_AV + A_KV_W
OFF_BK = OFF_BQ + B_QK_W
OFF_BV = OFF_BK + B_QK_W
OFF_GATE = OFF_BV + B_V_W
OFF_LR = OFF_GATE + B_V_W
IN_COLS = OFF_LR + 2 * B_GATE_RANK
LANE = 128
IN_COLS_PAD = OFF_LR + LANE
MOD_ROWS = 16

VMEM_LIMIT = 56 * 1024 * 1024

BF16 = jnp.bfloat16
F32 = jnp.float32


def _cparams(sem):
    return pltpu.CompilerParams(dimension_semantics=sem, vmem_limit_bytes=VMEM_LIMIT)


def _dot(a, b):
    return jnp.dot(a, b, preferred_element_type=F32)


def _dot_nt(a, b):
    return lax.dot_general(a, b, (((1,), (1,)), ((), ())), preferred_element_type=F32)


def _dot_tn(a, b):
    return lax.dot_general(a, b, (((0,), (0,)), ((), ())), preferred_element_type=F32)


def _tri_sum(tri, g):
    g_hi = g.astype(BF16)
    g_lo = (g - g_hi.astype(F32)).astype(BF16)
    return _dot(tri, g_hi) + _dot(tri, g_lo)


def _rms(xf, gain):
    return xf * lax.rsqrt(jnp.mean(xf * xf, axis=-1, keepdims=True) + EPS) * gain


def _ada_kernel(c_ref, w_ref, b_ref, o_ref):
    cv = c_ref[...]
    s = (cv * jax.nn.sigmoid(cv)).astype(BF16)
    o_ref[...] = _dot(s, w_ref[...].astype(BF16)) + b_ref[...]


def _ada(cvec, ada_w, ada_b):
    depth = ada_w.shape[0]
    n = ada_w.shape[2]
    tn = 1024
    return pl.pallas_call(
        _ada_kernel,
        out_shape=jax.ShapeDtypeStruct((depth, MOD_ROWS, n), F32),
        grid=(depth, n // tn),
        in_specs=[
            pl.BlockSpec((MOD_ROWS, D_MODEL), lambda l, j: (0, 0)),
            pl.BlockSpec((None, D_MODEL, tn), lambda l, j: (l, 0, j)),
            pl.BlockSpec((None, 1, tn), lambda l, j: (l, 0, j)),
        ],
        out_specs=pl.BlockSpec((None, MOD_ROWS, tn), lambda l, j: (l, 0, j)),
        compiler_params=_cparams(("parallel", "parallel")),
        name="ada",
    )(cvec, ada_w, ada_b.reshape(depth, 1, n))


def _proj_kernel(*refs, latent):
    if latent:
        (x_ref, gn_ref, sh_ref, sc_ref, w_ref, qn_ref, kn_ref, cos_ref, sa_ref, sb_ref,
         gkf_ref, gkfb_ref, gkb_ref, gkbb_ref,
         q_ref, k_ref, v_ref, bq_ref, bk_ref, bv_ref, gate_ref, gf_ref, gb_ref) = refs
    else:
        (x_ref, gn_ref, sh_ref, sc_ref, w_ref, kn_ref,
         gkf_ref, gkfb_ref, gkb_ref, gkbb_ref,
         k_ref, v_ref, bk_ref, bv_ref, gf_ref, gb_ref) = refs

    xt = x_ref[...]
    h = _rms(xt, gn_ref[...]) * (1.0 + sc_ref[...]) + sh_ref[...]
    hb = h.astype(BF16)

    def proj(off, width):
        return _dot(hb, w_ref[:, off:off + width])

    def rope(t):
        return (t * cos_ref[...] + pltpu.roll(t, HEAD_DIM - HEAD_DIM // 4, 1) * sa_ref[...]
                + pltpu.roll(t, HEAD_DIM // 4, 1) * sb_ref[...])

    if latent:
        for half in range(2):
            seg = proj(OFF_AQ + half * (A_Q_W // 2), A_Q_W // 2)
            for j in range(A_Q_HEADS // 2):
                t = _rms(seg[:, j * HEAD_DIM:(j + 1) * HEAD_DIM], qn_ref[...])
                t = rope(t) * HEAD_DIM ** -0.5
                hd = half * (A_Q_HEADS // 2) + j
                q_ref[:, hd * HEAD_DIM:(hd + 1) * HEAD_DIM] = t.astype(BF16)

    seg = proj(OFF_AK, 2 * A_KV_W)
    for j in range(A_KV_HEADS):
        t = _rms(seg[:, j * HEAD_DIM:(j + 1) * HEAD_DIM], kn_ref[...])
        if latent:
            t = rope(t)
        k_ref[:, j * HEAD_DIM:(j + 1) * HEAD_DIM] = t.astype(BF16)
    v_ref[...] = seg[:, A_KV_W:].astype(BF16)

    if latent:
        bq_ref[...] = proj(OFF_BQ, B_QK_W) * B_DK ** -0.5
    bk_ref[...] = proj(OFF_BK, B_QK_W)
    for half in range(2):
        bv_ref[:, half * (B_V_W // 2):(half + 1) * (B_V_W // 2)] = proj(OFF_BV + half * (B_V_W // 2), B_V_W // 2)
    if latent:
        for half in range(2):
            gate_ref[:, half * (B_V_W // 2):(half + 1) * (B_V_W // 2)] = proj(
                OFF_GATE + half * (B_V_W // 2), B_V_W // 2)

    lr = proj(OFF_LR, LANE).astype(BF16)

    def log_decay(gk_ref, gkb_ref, out_ref):
        z = _dot(lr, gk_ref[...]) + gkb_ref[...]
        out_ref[...] = (jnp.minimum(z, 0.0) - jnp.log1p(jnp.exp(-jnp.abs(z)))) / B_GATE_NORM

    log_decay(gkf_ref, gkfb_ref, gf_ref)
    log_decay(gkb_ref, gkbb_ref, gb_ref)


def _proj(x2, gn, mod3, mod_row_fn, tiles_per_batch, tm, w_in, qn, kn, rope_tabs, gkf, gkfb, gkb, gkbb, latent):
    t_tok = x2.shape[0]
    row = lambda width: pl.BlockSpec((1, width), lambda i: (0, 0))
    tok = lambda width: pl.BlockSpec((tm, width), lambda i: (i, 0))
    mod = lambda col: pl.BlockSpec((None, 1, D_MODEL), lambda i: (mod_row_fn(i), 0, col))
    in_specs = [tok(D_MODEL), row(D_MODEL), mod(0), mod(1),
                pl.BlockSpec((D_MODEL, IN_COLS_PAD), lambda i: (0, 0), pipeline_mode=pl.Buffered(1))]
    args = [x2, gn, mod3, mod3, w_in]
    if latent:
        in_specs += [row(HEAD_DIM), row(HEAD_DIM)]
        args += [qn, kn]
        in_specs += [pl.BlockSpec((tm, HEAD_DIM), lambda i: (i % tiles_per_batch, 0))] * 3
        args += list(rope_tabs)
    else:
        in_specs += [row(HEAD_DIM)]
        args += [kn]
    in_specs += [pl.BlockSpec((LANE, B_QK_W), lambda i: (0, 0)), row(B_QK_W)] * 2
    args += [gkf, gkfb, gkb, gkbb]

    def out(width, dtype):
        return jax.ShapeDtypeStruct((t_tok, width), dtype), tok(width)

    if latent:
        outs = [out(A_Q_W, BF16), out(A_KV_W, BF16), out(A_KV_W, BF16), out(B_QK_W, F32), out(B_QK_W, F32),
                out(B_V_W, F32), out(B_V_W, F32), out(B_QK_W, F32), out(B_QK_W, F32)]
    else:
        outs = [out(A_KV_W, BF16), out(A_KV_W, BF16), out(B_QK_W, F32), out(B_V_W, F32),
                out(B_QK_W, F32), out(B_QK_W, F32)]
    return pl.pallas_call(
        functools.partial(_proj_kernel, latent=latent),
        out_shape=[o[0] for o in outs],
        grid=(t_tok // tm,),
        in_specs=in_specs,
        out_specs=[o[1] for o in outs],
        compiler_params=_cparams(("parallel",)),
        name="proj_lat" if latent else "proj_ctx",
    )(*args)


def _attn_kernel(sink_ref, q_ref, k_ref, v_ref, kc_ref, vc_ref, o_ref, *, n_lat):
    kvh = pl.program_id(1)
    n = pl.program_id(2)
    n_win = 3 * BLOCK_Q
    start = jnp.clip(n * BLOCK_Q - BLOCK_Q, 0, n_lat - n_win)
    start = pl.multiple_of(start, BLOCK_Q)
    kw = k_ref[pl.ds(start, n_win), :]
    vw = v_ref[pl.ds(start, n_win), :]
    kc = kc_ref[...]
    vc = vc_ref[...]
    kpos = start + lax.broadcasted_iota(jnp.int32, (n_win, BLOCK_Q), 0)
    qpos = n * BLOCK_Q + lax.broadcasted_iota(jnp.int32, (n_win, BLOCK_Q), 1)
    valid = jnp.abs(qpos - kpos) <= WINDOW
    heads = range(A_GROUP)
    q = jnp.concatenate([q_ref[:, g * HEAD_DIM:(g + 1) * HEAD_DIM] for g in heads], axis=0)
    s_lat = _dot_nt(kw, q)
    s_lat = jnp.concatenate(
        [jnp.where(valid, s_lat[:, g * BLOCK_Q:(g + 1) * BLOCK_Q], NEG_INF) for g in heads], axis=1)
    s_ctx = _dot_nt(kc, q)
    sk = jnp.concatenate(
        [jnp.full((1, BLOCK_Q), sink_ref[kvh * A_GROUP + g], F32) for g in heads], axis=1)
    m = jnp.maximum(jnp.maximum(jnp.max(s_lat, axis=0, keepdims=True),
                                jnp.max(s_ctx, axis=0, keepdims=True)), sk)
    p_lat = jnp.exp(s_lat - m)
    p_ctx = jnp.exp(s_ctx - m)
    den = (jnp.sum(p_lat, axis=0, keepdims=True) + jnp.sum(p_ctx, axis=0, keepdims=True)
           + jnp.exp(sk - m))
    o_t = (_dot_tn(vw, p_lat.astype(BF16)) + _dot_tn(vc, p_ctx.astype(BF16))) * (1.0 / den)
    for g in heads:
        o_ref[:, g * HEAD_DIM:(g + 1) * HEAD_DIM] = o_t[:, g * BLOCK_Q:(g + 1) * BLOCK_Q].T.astype(BF16)


def _attn(sink, q, k, v, kc, vc, bsz, n_lat, n_ctx):
    nb = n_lat // BLOCK_Q
    gw = A_GROUP * HEAD_DIM
    return pl.pallas_call(
        functools.partial(_attn_kernel, n_lat=n_lat),
        out_shape=jax.ShapeDtypeStruct((bsz * n_lat, A_Q_W), BF16),
        grid=(bsz, A_KV_HEADS, nb),
        in_specs=[
            pl.BlockSpec(memory_space=pltpu.SMEM),
            pl.BlockSpec((BLOCK_Q, gw), lambda b, h, n: (b * nb + n, h)),
            pl.BlockSpec((n_lat, HEAD_DIM), lambda b, h, n: (b, h)),
            pl.BlockSpec((n_lat, HEAD_DIM), lambda b, h, n: (b, h)),
            pl.BlockSpec((n_ctx, HEAD_DIM), lambda b, h, n: (b, h)),
            pl.BlockSpec((n_ctx, HEAD_DIM), lambda b, h, n: (b, h)),
        ],
        out_specs=pl.BlockSpec((BLOCK_Q, gw), lambda b, h, n: (b * nb + n, h)),
        compiler_params=_cparams(("parallel", "parallel", "parallel")),
        name="attn",
    )(sink, q, k, v, kc, vc)


def _gla_kernel(q_ref, k_ref, v_ref, gf_ref, gb_ref, kc_ref, vc_ref, gfc_ref, gbc_ref, o_ref,
                sf_ref, sb_ref, *, n_lat, n_ctx):
    c = B_CHUNK

    def tri(nn, fn):
        r = lax.broadcasted_iota(jnp.int32, (nn, nn), 0)
        s = lax.broadcasted_iota(jnp.int32, (nn, nn), 1)
        return fn(r, s)

    kc = kc_ref[...]
    vcb = vc_ref[...].astype(BF16)
    up_strict = tri(n_ctx, lambda r, s: s > r).astype(BF16)
    lo_strict = tri(n_ctx, lambda r, s: s < r).astype(BF16)
    kd_f = (kc * jnp.exp(_tri_sum(up_strict, gfc_ref[...]))).astype(BF16)
    kd_b = (kc * jnp.exp(_tri_sum(lo_strict, gbc_ref[...]))).astype(BF16)
    sf_ref[...] = _dot_tn(vcb, kd_f)
    sb_ref[...] = _dot_tn(vcb, kd_b)

    o_ref[...] = jnp.zeros_like(o_ref)
    r = c * GLA_GROUP
    ng = n_lat // r
    same = tri(r, lambda i, j: (i // c) == (j // c))
    lo_mask = jnp.logical_and(same, tri(r, lambda i, j: j <= i))
    up_mask = jnp.logical_and(same, tri(r, lambda i, j: j >= i))
    lo_incl = lo_mask.astype(BF16)
    up_incl = up_mask.astype(BF16)

    def group(rows, g_ref, tri_incl, mask, s_ref, order, last):
        g = g_ref[rows, :]
        b = _tri_sum(tri_incl, g)
        bl = jnp.concatenate(
            [jnp.broadcast_to(b[n * c + last:n * c + last + 1, :], (c, B_DK)) for n in range(GLA_GROUP)], axis=0)
        q = q_ref[rows, :]
        k = k_ref[rows, :]
        vb = v_ref[rows, :].astype(BF16)
        qe = (q * jnp.exp(b)).astype(BF16)
        ke = (k * jnp.exp(-b)).astype(BF16)
        kd = (k * jnp.exp(bl - b)).astype(BF16)
        dec = jnp.exp(bl)
        a = jnp.where(mask, _dot_nt(qe, ke), 0.0).astype(BF16)
        o_intra = _dot(a, vb)
        st = s_ref[...]
        o_inter = [None] * GLA_GROUP
        for n in order:
            sl = slice(n * c, (n + 1) * c)
            o_inter[n] = _dot_nt(qe[sl], st.astype(BF16))
            st = st * dec[n * c:n * c + 1, :] + _dot_tn(vb[sl], kd[sl])
        s_ref[...] = st
        o_ref[rows, :] += o_intra + jnp.concatenate(o_inter, axis=0)

    def body(i, carry):
        rf = pl.ds(pl.multiple_of(i * r, r), r)
        rb = pl.ds(pl.multiple_of((ng - 1 - i) * r, r), r)
        group(rf, gf_ref, lo_incl, lo_mask, sf_ref, range(GLA_GROUP), c - 1)
        group(rb, gb_ref, up_incl, up_mask, sb_ref, range(GLA_GROUP - 1, -1, -1), 0)
        return carry

    lax.fori_loop(0, ng, body, 0, unroll=2 if ng % 2 == 0 else 1)


def _gla(bq, bk, bv, gf, gb, bkc, bvc, gfc, gbc, bsz, n_lat, n_ctx):
    lat = lambda width: pl.BlockSpec((n_lat, width), lambda b, h: (b, h))
    cx = lambda width: pl.BlockSpec((n_ctx, width), lambda b, h: (b, h))
    return pl.pallas_call(
        functools.partial(_gla_kernel, n_lat=n_lat, n_ctx=n_ctx),
        out_shape=jax.ShapeDtypeStruct((bsz * n_lat, B_V_W), F32),
        grid=(bsz, B_HEADS),
        in_specs=[lat(B_DK), lat(B_DK), lat(B_DV), lat(B_DK), lat(B_DK),
                  cx(B_DK), cx(B_DV), cx(B_DK), cx(B_DK)],
        out_specs=lat(B_DV),
        scratch_shapes=[pltpu.VMEM((B_DV, B_DK), F32), pltpu.VMEM((B_DV, B_DK), F32)],
        compiler_params=_cparams(("parallel", "parallel")),
        name="gla",
    )(bq, bk, bv, gf, gb, bkc, bvc, gfc, gbc)


def _mixout_kernel(oa_ref, ob_ref, gate_ref, gn_ref, wa_ref, wb_ref, x_ref, g1_ref, o_ref):
    y = _dot(oa_ref[...], wa_ref[...])
    for hd in range(B_HEADS):
        cols = slice(hd * B_DV, (hd + 1) * B_DV)
        gt = gate_ref[:, cols]
        t = _rms(ob_ref[:, cols], gn_ref[...]) * (gt * jax.nn.sigmoid(gt))
        y = y + _dot(t.astype(BF16), wb_ref[cols, :])
    o_ref[...] = x_ref[...] + g1_ref[...] * y


def _mixout(oa, ob, gate, gla_norm, w_a, w_b, x2, mod3, mod_row_fn, tm):
    t_tok = x2.shape[0]
    tok = lambda width: pl.BlockSpec((tm, width), lambda i: (i, 0))
    const = lambda shape: pl.BlockSpec(shape, lambda i: (0, 0))
    return pl.pallas_call(
        _mixout_kernel,
        out_shape=jax.ShapeDtypeStruct((t_tok, D_MODEL), F32),
        grid=(t_tok // tm,),
        in_specs=[tok(A_Q_W), tok(B_V_W), tok(B_V_W), const((1, B_DV)),
                  const((A_Q_W, D_MODEL)), const((B_V_W, D_MODEL)), tok(D_MODEL),
                  pl.BlockSpec((None, 1, D_MODEL), lambda i: (mod_row_fn(i), 0, 2))],
        out_specs=tok(D_MODEL),
        compiler_params=_cparams(("parallel",)),
        name="mixout",
    )(oa, ob, gate, gla_norm, w_a, w_b, x2, mod3)


def _mlp_kernel(x_ref, gn_ref, sh_ref, sc_ref, g2_ref, w1_ref, w2_ref, o_ref, h_ref):
    j = pl.program_id(1)

    @pl.when(j == 0)
    def _():
        xt = x_ref[...]
        h = _rms(xt, gn_ref[...] * (1.0 + sc_ref[...])) + sh_ref[...]
        h_ref[...] = h.astype(BF16)
        o_ref[...] = xt

    u = jnp.maximum(_dot(h_ref[...], w1_ref[...]), 0.0)
    o_ref[...] += g2_ref[...] * _dot((u * u).astype(BF16), w2_ref[...])


def _mlp(x2, gn, mod3, mod_row_fn, w1, w2, layer, tm, tf):
    t_tok = x2.shape[0]
    mod = lambda col: pl.BlockSpec((None, 1, D_MODEL), lambda i, j: (mod_row_fn(i), 0, col))
    return pl.pallas_call(
        _mlp_kernel,
        out_shape=jax.ShapeDtypeStruct((t_tok, D_MODEL), F32),
        grid=(t_tok // tm, D_FF // tf),
        in_specs=[pl.BlockSpec((tm, D_MODEL), lambda i, j: (i, 0)),
                  pl.BlockSpec((1, D_MODEL), lambda i, j: (0, 0)),
                  mod(3), mod(4), mod(5),
                  pl.BlockSpec((None, D_MODEL, tf), lambda i, j: (layer, 0, j)),
                  pl.BlockSpec((None, tf, D_MODEL), lambda i, j: (layer, j, 0))],
        out_specs=pl.BlockSpec((tm, D_MODEL), lambda i, j: (i, 0)),
        scratch_shapes=[pltpu.VMEM((tm, D_MODEL), BF16)],
        compiler_params=_cparams(("parallel", "arbitrary")),
        name="mlp",
    )(x2, gn, mod3, mod3, mod3, w1, w2)


def _fch_kernel(x_ref, gn_ref, sh_ref, sc_ref, cs_ref, o_ref):
    h = _rms(x_ref[...], gn_ref[...]) * (1.0 + sc_ref[...]) + sh_ref[...]
    hb = h.astype(BF16)
    for g in range(C_GROUPS):
        cols = slice(g * C_GROUP_DIM, (g + 1) * C_GROUP_DIM)
        y = _dot(hb[:, cols], cs_ref[...])
        o_ref[0, :, cols] = y[:, :C_GROUP_DIM].astype(BF16)
        o_ref[1, :, cols] = y[:, C_GROUP_DIM:].astype(BF16)


def _fch(x2, gn, mod3, mod_row_fn, cs_ch, bsz, n_lat, tm):
    tpb = n_lat // tm
    mod = lambda col: pl.BlockSpec((None, 1, D_MODEL), lambda i: (mod_row_fn(i), 0, col))
    return pl.pallas_call(
        _fch_kernel,
        out_shape=jax.ShapeDtypeStruct((bsz, 2, n_lat, D_MODEL), BF16),
        grid=(bsz * tpb,),
        in_specs=[pl.BlockSpec((tm, D_MODEL), lambda i: (i, 0)),
                  pl.BlockSpec((1, D_MODEL), lambda i: (0, 0)),
                  mod(0), mod(1),
                  pl.BlockSpec((C_GROUP_DIM, 2 * C_GROUP_DIM), lambda i: (0, 0))],
        out_specs=pl.BlockSpec((None, 2, tm, D_MODEL), lambda i: (i // tpb, 0, i % tpb, 0)),
        compiler_params=_cparams(("parallel",)),
        name="fch",
    )(x2, gn, mod3, mod3, cs_ch)


def _fpos_kernel(cs_ref, y_ref, o_ref, *, scale):
    o_ref[...] = (_dot(cs_ref[...], y_ref[...]) * scale).astype(BF16)


def _fpos(cs_pos, ycs, bsz, n_lat, tk, td):
    nk = n_lat // tk
    nd = D_MODEL // td
    scale = float((n_lat * C_GROUP_DIM) ** -0.5)
    return pl.pallas_call(
        functools.partial(_fpos_kernel, scale=scale),
        out_shape=jax.ShapeDtypeStruct((bsz * n_lat, D_MODEL), BF16),
        grid=(bsz, nd, nk),
        in_specs=[pl.BlockSpec((tk, 2 * n_lat), lambda b, d, k: (k, 0)),
                  pl.BlockSpec((None, 2 * n_lat, td), lambda b, d, k: (b, 0, d))],
        out_specs=pl.BlockSpec((tk, td), lambda b, d, k: (b * nk + k, d)),
        compiler_params=_cparams(("parallel", "parallel", "parallel")),
        name="fpos",
    )(cs_pos, ycs)


def _linres_kernel(z_ref, w_ref, b_ref, x_ref, g1_ref, o_ref):
    y = _dot(z_ref[...], w_ref[...]) + b_ref[...]
    o_ref[...] = x_ref[...] + g1_ref[...] * y


def _linres(z, w, bias, x2, mod3, mod_row_fn, tm):
    t_tok = x2.shape[0]
    tok = lambda width: pl.BlockSpec((tm, width), lambda i: (i, 0))
    return pl.pallas_call(
        _linres_kernel,
        out_shape=jax.ShapeDtypeStruct((t_tok, D_MODEL), F32),
        grid=(t_tok // tm,),
        in_specs=[tok(D_MODEL), pl.BlockSpec((D_MODEL, D_MODEL), lambda i: (0, 0)),
                  pl.BlockSpec((1, D_MODEL), lambda i: (0, 0)), tok(D_MODEL),
                  pl.BlockSpec((None, 1, D_MODEL), lambda i: (mod_row_fn(i), 0, 2))],
        out_specs=tok(D_MODEL),
        compiler_params=_cparams(("parallel",)),
        name="linres",
    )(z, w, bias, x2, mod3)


def _rope_tables(n_lat):
    t = jnp.arange(n_lat)
    row = (t // GRID_W).astype(F32)
    col = (t % GRID_W).astype(F32)
    n_freq = HEAD_DIM // 4
    inv_freq = ROPE_BASE ** (-jnp.arange(n_freq, dtype=F32) / n_freq)
    ang_r, ang_c = row[:, None] * inv_freq, col[:, None] * inv_freq
    cr, sr, cc, sc = jnp.cos(ang_r), jnp.sin(ang_r), jnp.cos(ang_c), jnp.sin(ang_c)
    zero = jnp.zeros_like(sr)
    cos = jnp.concatenate([cr, cr, cc, cc], axis=-1)
    sin_a = jnp.concatenate([-sr, zero, -sc, zero], axis=-1)
    sin_b = jnp.concatenate([zero, sr, zero, sc], axis=-1)
    return cos, sin_a, sin_b


def _dft_tables(n_lat):
    def cs(n):
        idx = np.arange(n, dtype=np.int64)
        ang = 2.0 * np.pi * ((idx[:, None] * idx[None, :]) % n).astype(np.float64) / n
        return np.cos(ang), np.sin(ang)

    cm, sm = cs(C_GROUP_DIM)
    cn, sn = cs(n_lat)
    cs_ch = jnp.asarray(np.concatenate([cm, sm], axis=1), dtype=F32)
    cs_pos = jnp.asarray(np.concatenate([cn, -sn], axis=1), dtype=F32)
    return cs_ch.astype(BF16), cs_pos.astype(BF16)


def kernel(x, c, ctx, c_ctx, ada_w, ada_b, norm_mix, norm_mlp, mlp_w1, mlp_w2, ab_w_in, ab_q_norm, ab_k_norm,
           ab_sink, ab_gk_f, ab_gk_f_bias, ab_gk_b, ab_gk_b_bias, ab_gla_norm, ab_w_out, c_w_out, c_b_out):
    bsz, n_lat, _ = x.shape
    n_ctx = ctx.shape[1]
    depth = ada_w.shape[0]
    assert depth == 2 and bsz < MOD_ROWS
    t_tok = bsz * n_lat

    cvec = jnp.concatenate([c, c_ctx[None, :], jnp.zeros((MOD_ROWS - bsz - 1, D_MODEL), F32)], axis=0)
    mod = _ada(cvec, ada_w, ada_b)
    mod3 = mod.reshape(depth * MOD_ROWS, 1, 6 * D_MODEL)

    w1 = mlp_w1.astype(BF16)
    w2 = mlp_w2.astype(BF16)
    x2 = x.reshape(t_tok, D_MODEL)

    tm = min(256, n_lat)
    tpb = n_lat // tm
    lat_row0 = lambda i: i // tpb
    w_in = jnp.pad(ab_w_in[0], ((0, 0), (0, IN_COLS_PAD - IN_COLS))).astype(BF16)
    gn_mix0 = norm_mix[0].reshape(1, D_MODEL)
    qn = ab_q_norm[0].reshape(1, HEAD_DIM)
    kn = ab_k_norm[0].reshape(1, HEAD_DIM)
    gkf = jnp.pad(ab_gk_f[0], ((0, LANE - B_GATE_RANK), (0, 0))).astype(BF16)
    gkb = jnp.pad(ab_gk_b[0], ((B_GATE_RANK, LANE - 2 * B_GATE_RANK), (0, 0))).astype(BF16)
    gkfb = ab_gk_f_bias[0].reshape(1, B_QK_W)
    gkbb = ab_gk_b_bias[0].reshape(1, B_QK_W)
    q, k, v, bq, bk, bv, gate, gf, gb = _proj(
        x2, gn_mix0, mod3, lat_row0, tpb, tm, w_in, qn, kn, _rope_tables(n_lat), gkf, gkfb, gkb, gkbb, True)
    tmc = min(256, n_ctx)
    kc, vc, bkc, bvc, gfc, gbc = _proj(
        ctx.reshape(bsz * n_ctx, D_MODEL), gn_mix0, mod3, lambda i: bsz, n_ctx // tmc, tmc, w_in, None, kn, None,
        gkf, gkfb, gkb, gkbb, False)

    oa = _attn(ab_sink[0], q, k, v, kc, vc, bsz, n_lat, n_ctx)
    ob = _gla(bq, bk, bv, gf, gb, bkc, bvc, gfc, gbc, bsz, n_lat, n_ctx)

    w_out = ab_w_out[0].astype(BF16)
    tmo = min(512, n_lat)
    tpbo = n_lat // tmo
    x2 = _mixout(oa, ob, gate, ab_gla_norm[0].reshape(1, B_DV), w_out[:A_Q_W], w_out[A_Q_W:], x2, mod3,
                 lambda i: i // tpbo, tmo)
    x2 = _mlp(x2, norm_mlp[0].reshape(1, D_MODEL), mod3, lambda i: i // tpbo, w1, w2, 0, tmo, 1024)

    cs_ch, cs_pos = _dft_tables(n_lat)
    row1 = lambda i: MOD_ROWS + i // tpbo
    ycs = _fch(x2, norm_mix[1].reshape(1, D_MODEL), mod3, row1, cs_ch, bsz, n_lat, tmo)
    z = _fpos(cs_pos, ycs.reshape(bsz, 2 * n_lat, D_MODEL), bsz, n_lat, min(512, n_lat), 1024)
    x2 = _linres(z, c_w_out[0].astype(BF16), c_b_out[0].reshape(1, D_MODEL), x2, mod3, row1, tmo)
    x2 = _mlp(x2, norm_mlp[1].reshape(1, D_MODEL), mod3, row1, w1, w2, 1, tmo, 1024)
    return x2.reshape(bsz, n_lat, D_MODEL)
```

```python
import functools

import numpy as np
import jax
import jax.numpy as jnp
from jax import lax
from jax.experimental import pallas as pl
from jax.experimental.pallas import tpu as pltpu

D_MODEL = 2048
GRID_W = 64
EPS = 1e-6
NEG_INF = -1e30
HEAD_DIM = 128
A_Q_HEADS = 8
A_KV_HEADS = 2
A_GROUP = A_Q_HEADS // A_KV_HEADS
WINDOW = 128
BLOCK_Q = 128
ROPE_BASE = 10000.0
B_HEADS = 4
B_DV = 256
B_DK = 128
B_GATE_RANK = 16
B_GATE_NORM = 16.0
B_CHUNK = 64
GLA_GROUP = 4
C_GROUPS = 8
C_GROUP_DIM = D_MODEL // C_GROUPS
D_FF = 4 * D_MODEL

A_Q_W = A_Q_HEADS * HEAD_DIM
A_KV_W = A_KV_HEADS * HEAD_DIM
B_QK_W = B_HEADS * B_DK
B_V_W = B_HEADS * B_DV
OFF_AQ = 0
OFF_AK = OFF_AQ + A_Q_W
OFF_AV = OFF_AK + A_KV_W
OFF_BQ = OFF_AV + A_KV_W
OFF_BK = OFF_BQ + B_QK_W
OFF_BV = OFF_BK + B_QK_W
OFF_GATE = OFF_BV + B_V_W
OFF_LR = OFF_GATE + B_V_W
IN_COLS = OFF_LR + 2 * B_GATE_RANK
LANE = 128
IN_COLS_PAD = OFF_LR + LANE
MOD_ROWS = 16

VMEM_LIMIT = 56 * 1024 * 1024

BF16 = jnp.bfloat16
F32 = jnp.float32


def _cparams(sem):
    return pltpu.CompilerParams(dimension_semantics=sem, vmem_limit_bytes=VMEM_LIMIT)


def _dot(a, b):
    return jnp.dot(a, b, preferred_element_type=F32)


def _dot_nt(a, b):
    return lax.dot_general(a, b, (((1,), (1,)), ((), ())), preferred_element_type=F32)


def _dot_tn(a, b):
    return lax.dot_general(a, b, (((0,), (0,)), ((), ())), preferred_element_type=F32)


def _tri_sum(tri, g):
    g_hi = g.astype(BF16)
    g_lo = (g - g_hi.astype(F32)).astype(BF16)
    return _dot(tri, g_hi) + _dot(tri, g_lo)


def _rms(xf, gain):
    return xf * lax.rsqrt(jnp.mean(xf * xf, axis=-1, keepdims=True) + EPS) * gain


def _ada_kernel(c_ref, w_ref, b_ref, o_ref):
    cv = c_ref[...]
    s = (cv * jax.nn.sigmoid(cv)).astype(BF16)
    o_ref[...] = _dot(s, w_ref[...].astype(BF16)) + b_ref[...]


def _ada(cvec, ada_w, ada_b):
    depth = ada_w.shape[0]
    n = ada_w.shape[2]
    tn = 1024
    return pl.pallas_call(
        _ada_kernel,
        out_shape=jax.ShapeDtypeStruct((depth, MOD_ROWS, n), F32),
        grid=(depth, n // tn),
        in_specs=[
            pl.BlockSpec((MOD_ROWS, D_MODEL), lambda l, j: (0, 0)),
            pl.BlockSpec((None, D_MODEL, tn), lambda l, j: (l, 0, j)),
            pl.BlockSpec((None, 1, tn), lambda l, j: (l, 0, j)),
        ],
        out_specs=pl.BlockSpec((None, MOD_ROWS, tn), lambda l, j: (l, 0, j)),
        compiler_params=_cparams(("parallel", "parallel")),
        name="ada",
    )(cvec, ada_w, ada_b.reshape(depth, 1, n))


def _proj_kernel(*refs, latent):
    if latent:
        (x_ref, gn_ref, sh_ref, sc_ref, w_ref, qn_ref, kn_ref, cos_ref, sa_ref, sb_ref,
         gkf_ref, gkfb_ref, gkb_ref, gkbb_ref,
         q_ref, k_ref, v_ref, bq_ref, bk_ref, bv_ref, gate_ref, gf_ref, gb_ref) = refs
    else:
        (x_ref, gn_ref, sh_ref, sc_ref, w_ref, kn_ref,
         gkf_ref, gkfb_ref, gkb_ref, gkbb_ref,
         k_ref, v_ref, bk_ref, bv_ref, gf_ref, gb_ref) = refs

    xt = x_ref[...]
    h = _rms(xt, gn_ref[...]) * (1.0 + sc_ref[...]) + sh_ref[...]
    hb = h.astype(BF16)

    def proj(off, width):
        return _dot(hb, w_ref[:, off:off + width])

    def rope(t):
        return (t * cos_ref[...] + pltpu.roll(t, HEAD_DIM - HEAD_DIM // 4, 1) * sa_ref[...]
                + pltpu.roll(t, HEAD_DIM // 4, 1) * sb_ref[...])

    if latent:
        for half in range(2):
            seg = proj(OFF_AQ + half * (A_Q_W // 2), A_Q_W // 2)
            for j in range(A_Q_HEADS // 2):
                t = _rms(seg[:, j * HEAD_DIM:(j + 1) * HEAD_DIM], qn_ref[...])
                t = rope(t) * HEAD_DIM ** -0.5
                hd = half * (A_Q_HEADS // 2) + j
                q_ref[:, hd * HEAD_DIM:(hd + 1) * HEAD_DIM] = t.astype(BF16)

    seg = proj(OFF_AK, 2 * A_KV_W)
    for j in range(A_KV_HEADS):
        t = _rms(seg[:, j * HEAD_DIM:(j + 1) * HEAD_DIM], kn_ref[...])
        if latent:
            t = rope(t)
        k_ref[:, j * HEAD_DIM:(j + 1) * HEAD_DIM] = t.astype(BF16)
    v_ref[...] = seg[:, A_KV_W:].astype(BF16)

    if latent:
        bq_ref[...] = proj(OFF_BQ, B_QK_W) * B_DK ** -0.5
    bk_ref[...] = proj(OFF_BK, B_QK_W)
    for half in range(2):
        bv_ref[:, half * (B_V_W // 2):(half + 1) * (B_V_W // 2)] = proj(OFF_BV + half * (B_V_W // 2), B_V_W // 2)
    if latent:
        for half in range(2):
            gate_ref[:, half * (B_V_W // 2):(half + 1) * (B_V_W // 2)] = proj(
                OFF_GATE + half * (B_V_W // 2), B_V_W // 2)

    lr = proj(OFF_LR, LANE).astype(BF16)

    def log_decay(gk_ref, gkb_ref, out_ref):
        z = _dot(lr, gk_ref[...]) + gkb_ref[...]
        out_ref[...] = (jnp.minimum(z, 0.0) - jnp.log1p(jnp.exp(-jnp.abs(z)))) / B_GATE_NORM

    log_decay(gkf_ref, gkfb_ref, gf_ref)
    log_decay(gkb_ref, gkbb_ref, gb_ref)


def _proj(x2, gn, mod3, mod_row_fn, tiles_per_batch, tm, w_in, qn, kn, rope_tabs, gkf, gkfb, gkb, gkbb, latent):
    t_tok = x2.shape[0]
    row = lambda width: pl.BlockSpec((1, width), lambda i: (0, 0))
    tok = lambda width: pl.BlockSpec((tm, width), lambda i: (i, 0))
    mod = lambda col: pl.BlockSpec((None, 1, D_MODEL), lambda i: (mod_row_fn(i), 0, col))
    in_specs = [tok(D_MODEL), row(D_MODEL), mod(0), mod(1),
                pl.BlockSpec((D_MODEL, IN_COLS_PAD), lambda i: (0, 0), pipeline_mode=pl.Buffered(1))]
    args = [x2, gn, mod3, mod3, w_in]
    if latent:
        in_specs += [row(HEAD_DIM), row(HEAD_DIM)]
        args += [qn, kn]
        in_specs += [pl.BlockSpec((tm, HEAD_DIM), lambda i: (i % tiles_per_batch, 0))] * 3
        args += list(rope_tabs)
    else:
        in_specs += [row(HEAD_DIM)]
        args += [kn]
    in_specs += [pl.BlockSpec((LANE, B_QK_W), lambda i: (0, 0)), row(B_QK_W)] * 2
    args += [gkf, gkfb, gkb, gkbb]

    def out(width, dtype):
        return jax.ShapeDtypeStruct((t_tok, width), dtype), tok(width)

    if latent:
        outs = [out(A_Q_W, BF16), out(A_KV_W, BF16), out(A_KV_W, BF16), out(B_QK_W, F32), out(B_QK_W, F32),
                out(B_V_W, F32), out(B_V_W, F32), out(B_QK_W, F32), out(B_QK_W, F32)]
    else:
        outs = [out(A_KV_W, BF16), out(A_KV_W, BF16), out(B_QK_W, F32), out(B_V_W, F32),
                out(B_QK_W, F32), out(B_QK_W, F32)]
    return pl.pallas_call(
        functools.partial(_proj_kernel, latent=latent),
        out_shape=[o[0] for o in outs],
        grid=(t_tok // tm,),
        in_specs=in_specs,
        out_specs=[o[1] for o in outs],
        compiler_params=_cparams(("parallel",)),
        name="proj_lat" if latent else "proj_ctx",
    )(*args)


def _attn_kernel(sink_ref, q_ref, k_ref, v_ref, kc_ref, vc_ref, o_ref, *, n_lat):
    kvh = pl.program_id(1)
    n_win = 3 * BLOCK_Q
    heads = range(A_GROUP)
    kc = kc_ref[...]
    vc = vc_ref[...]
    sk = jnp.concatenate(
        [jnp.full((1, BLOCK_Q), sink_ref[kvh * A_GROUP + g], F32) for g in heads], axis=1)
    key_row = lax.broadcasted_iota(jnp.int32, (n_win, BLOCK_Q), 0)
    query_col = lax.broadcasted_iota(jnp.int32, (n_win, BLOCK_Q), 1)

    def block(n, carry):
        start = jnp.clip(n * BLOCK_Q - BLOCK_Q, 0, n_lat - n_win)
        start = pl.multiple_of(start, BLOCK_Q)
        rows = pl.ds(pl.multiple_of(n * BLOCK_Q, BLOCK_Q), BLOCK_Q)
        kw = k_ref[pl.ds(start, n_win), :]
        vw = v_ref[pl.ds(start, n_win), :]
        valid = jnp.abs((n * BLOCK_Q + query_col) - (start + key_row)) <= WINDOW
        q = jnp.concatenate([q_ref[rows, g * HEAD_DIM:(g + 1) * HEAD_DIM] for g in heads], axis=0)
        s_lat = _dot_nt(kw, q)
        s_lat = jnp.concatenate(
            [jnp.where(valid, s_lat[:, g * BLOCK_Q:(g + 1) * BLOCK_Q], NEG_INF) for g in heads], axis=1)
        s_ctx = _dot_nt(kc, q)
        m = jnp.maximum(jnp.maximum(jnp.max(s_lat, axis=0, keepdims=True),
                                    jnp.max(s_ctx, axis=0, keepdims=True)), sk)
        p_lat = jnp.exp(s_lat - m)
        p_ctx = jnp.exp(s_ctx - m)
        den = (jnp.sum(p_lat, axis=0, keepdims=True) + jnp.sum(p_ctx, axis=0, keepdims=True)
               + jnp.exp(sk - m))
        o_t = (_dot_tn(vw, p_lat.astype(BF16)) + _dot_tn(vc, p_ctx.astype(BF16))) * (1.0 / den)
        for g in heads:
            o_ref[rows, g * HEAD_DIM:(g + 1) * HEAD_DIM] = o_t[:, g * BLOCK_Q:(g + 1) * BLOCK_Q].T.astype(BF16)
        return carry

    lax.fori_loop(0, n_lat // BLOCK_Q, block, 0)


def _attn(sink, q, k, v, kc, vc, bsz, n_lat, n_ctx):
    gw = A_GROUP * HEAD_DIM
    blk = lambda rows, width: pl.BlockSpec((rows, width), lambda b, h: (b, h))
    return pl.pallas_call(
        functools.partial(_attn_kernel, n_lat=n_lat),
        out_shape=jax.ShapeDtypeStruct((bsz * n_lat, A_Q_W), BF16),
        grid=(bsz, A_KV_HEADS),
        in_specs=[pl.BlockSpec(memory_space=pltpu.SMEM), blk(n_lat, gw), blk(n_lat, HEAD_DIM),
                  blk(n_lat, HEAD_DIM), blk(n_ctx, HEAD_DIM), blk(n_ctx, HEAD_DIM)],
        out_specs=blk(n_lat, gw),
        compiler_params=_cparams(("parallel", "parallel")),
        name="attn",
    )(sink, q, k, v, kc, vc)


def _gla_kernel(q_ref, k_ref, v_ref, gf_ref, gb_ref, kc_ref, vc_ref, gfc_ref, gbc_ref, o_ref,
                sf_ref, sb_ref, *, n_lat, n_ctx):
    c = B_CHUNK

    def tri(nn, fn):
        r = lax.broadcasted_iota(jnp.int32, (nn, nn), 0)
        s = lax.broadcasted_iota(jnp.int32, (nn, nn), 1)
        return fn(r, s)

    kc = kc_ref[...]
    vcb = vc_ref[...].astype(BF16)
    up_strict = tri(n_ctx, lambda r, s: s > r).astype(BF16)
    lo_strict = tri(n_ctx, lambda r, s: s < r).astype(BF16)
    kd_f = (kc * jnp.exp(_tri_sum(up_strict, gfc_ref[...]))).astype(BF16)
    kd_b = (kc * jnp.exp(_tri_sum(lo_strict, gbc_ref[...]))).astype(BF16)
    sf_ref[...] = _dot_tn(vcb, kd_f)
    sb_ref[...] = _dot_tn(vcb, kd_b)

    o_ref[...] = jnp.zeros_like(o_ref)
    r = c * GLA_GROUP
    ng = n_lat // r
    same = tri(r, lambda i, j: (i // c) == (j // c))
    lo_mask = jnp.logical_and(same, tri(r, lambda i, j: j <= i))
    up_mask = jnp.logical_and(same, tri(r, lambda i, j: j >= i))
    lo_incl = lo_mask.astype(BF16)
    up_incl = up_mask.astype(BF16)
    row_chunk = lax.broadcasted_iota(jnp.int32, (r, B_DK), 0) // c

    def group(rows, g_ref, tri_incl, mask, s_ref, order, last):
        g = g_ref[rows, :]
        b = _tri_sum(tri_incl, g)
        bl = jnp.concatenate(
            [jnp.broadcast_to(b[n * c + last:n * c + last + 1, :], (c, B_DK)) for n in range(GLA_GROUP)], axis=0)
        q = q_ref[rows, :]
        k = k_ref[rows, :]
        vb = v_ref[rows, :].astype(BF16)
        qe = (q * jnp.exp(b)).astype(BF16)
        ke = (k * jnp.exp(-b)).astype(BF16)
        kd = k * jnp.exp(bl - b)
        dec = jnp.exp(bl)
        a = jnp.where(mask, _dot_nt(qe, ke), 0.0).astype(BF16)
        o_intra = _dot(a, vb)
        kd_blocks = jnp.concatenate(
            [jnp.where(row_chunk == n, kd, 0.0).astype(BF16) for n in range(GLA_GROUP)], axis=1)
        kv = _dot_tn(vb, kd_blocks)
        st = s_ref[...]
        o_inter = [None] * GLA_GROUP
        for n in order:
            o_inter[n] = _dot_nt(qe[n * c:(n + 1) * c], st.astype(BF16))
            st = st * dec[n * c:n * c + 1, :] + kv[:, n * B_DK:(n + 1) * B_DK]
        s_ref[...] = st
        o_ref[rows, :] += o_intra + jnp.concatenate(o_inter, axis=0)

    def body(i, carry):
        rf = pl.ds(pl.multiple_of(i * r, r), r)
        rb = pl.ds(pl.multiple_of((ng - 1 - i) * r, r), r)
        group(rf, gf_ref, lo_incl, lo_mask, sf_ref, range(GLA_GROUP), c - 1)
        group(rb, gb_ref, up_incl, up_mask, sb_ref, range(GLA_GROUP - 1, -1, -1), 0)
        return carry

    lax.fori_loop(0, ng, body, 0, unroll=2 if ng % 2 == 0 else 1)


def _gla(bq, bk, bv, gf, gb, bkc, bvc, gfc, gbc, bsz, n_lat, n_ctx):
    lat = lambda width: pl.BlockSpec((n_lat, width), lambda b, h: (b, h))
    cx = lambda width: pl.BlockSpec((n_ctx, width), lambda b, h: (b, h))
    return pl.pallas_call(
        functools.partial(_gla_kernel, n_lat=n_lat, n_ctx=n_ctx),
        out_shape=jax.ShapeDtypeStruct((bsz * n_lat, B_V_W), F32),
        grid=(bsz, B_HEADS),
        in_specs=[lat(B_DK), lat(B_DK), lat(B_DV), lat(B_DK), lat(B_DK),
                  cx(B_DK), cx(B_DV), cx(B_DK), cx(B_DK)],
        out_specs=lat(B_DV),
        scratch_shapes=[pltpu.VMEM((B_DV, B_DK), F32), pltpu.VMEM((B_DV, B_DK), F32)],
        compiler_params=_cparams(("parallel", "parallel")),
        name="gla",
    )(bq, bk, bv, gf, gb, bkc, bvc, gfc, gbc)


def _mixout_kernel(oa_ref, ob_ref, gate_ref, gn_ref, wa_ref, wb_ref, x_ref, g1_ref, o_ref):
    y = _dot(oa_ref[...], wa_ref[...])
    for hd in range(B_HEADS):
        cols = slice(hd * B_DV, (hd + 1) * B_DV)
        gt = gate_ref[:, cols]
        t = _rms(ob_ref[:, cols], gn_ref[...]) * (gt * jax.nn.sigmoid(gt))
        y = y + _dot(t.astype(BF16), wb_ref[cols, :])
    o_ref[...] = x_ref[...] + g1_ref[...] * y


def _mixout(oa, ob, gate, gla_norm, w_a, w_b, x2, mod3, mod_row_fn, tm):
    t_tok = x2.shape[0]
    tok = lambda width: pl.BlockSpec((tm, width), lambda i: (i, 0))
    const = lambda shape: pl.BlockSpec(shape, lambda i: (0, 0))
    return pl.pallas_call(
        _mixout_kernel,
        out_shape=jax.ShapeDtypeStruct((t_tok, D_MODEL), F32),
        grid=(t_tok // tm,),
        in_specs=[tok(A_Q_W), tok(B_V_W), tok(B_V_W), const((1, B_DV)),
                  const((A_Q_W, D_MODEL)), const((B_V_W, D_MODEL)), tok(D_MODEL),
                  pl.BlockSpec((None, 1, D_MODEL), lambda i: (mod_row_fn(i), 0, 2))],
        out_specs=tok(D_MODEL),
        compiler_params=_cparams(("parallel",)),
        name="mixout",
    )(oa, ob, gate, gla_norm, w_a, w_b, x2, mod3)


def _mlp_kernel(x_ref, gn_ref, sh_ref, sc_ref, g2_ref, w1_ref, w2_ref, o_ref, h_ref):
    j = pl.program_id(1)

    @pl.when(j == 0)
    def _():
        xt = x_ref[...]
        h = _rms(xt, gn_ref[...] * (1.0 + sc_ref[...])) + sh_ref[...]
        h_ref[...] = h.astype(BF16)
        o_ref[...] = xt

    u = jnp.maximum(_dot(h_ref[...], w1_ref[...]), 0.0)
    o_ref[...] += g2_ref[...] * _dot((u * u).astype(BF16), w2_ref[...])


def _mlp(x2, gn, mod3, mod_row_fn, w1, w2, layer, tm, tf):
    t_tok = x2.shape[0]
    mod = lambda col: pl.BlockSpec((None, 1, D_MODEL), lambda i, j: (mod_row_fn(i), 0, col))
    return pl.pallas_call(
        _mlp_kernel,
        out_shape=jax.ShapeDtypeStruct((t_tok, D_MODEL), F32),
        grid=(t_tok // tm, D_FF // tf),
        in_specs=[pl.BlockSpec((tm, D_MODEL), lambda i, j: (i, 0)),
                  pl.BlockSpec((1, D_MODEL), lambda i, j: (0, 0)),
                  mod(3), mod(4), mod(5),
                  pl.BlockSpec((None, D_MODEL, tf), lambda i, j: (layer, 0, j)),
                  pl.BlockSpec((None, tf, D_MODEL), lambda i, j: (layer, j, 0))],
        out_specs=pl.BlockSpec((tm, D_MODEL), lambda i, j: (i, 0)),
        scratch_shapes=[pltpu.VMEM((tm, D_MODEL), BF16)],
        compiler_params=_cparams(("parallel", "arbitrary")),
        name="mlp",
    )(x2, gn, mod3, mod3, mod3, w1, w2)


F_HALF = C_GROUPS * (C_GROUP_DIM // 2)
F_MID = LANE
F_COLS = 2 * F_HALF + F_MID


def _fch_kernel(x_ref, gn_ref, sh_ref, sc_ref, t_ref, mid_ref, yc_ref, ys_ref):
    h = _rms(x_ref[...], gn_ref[...] * (1.0 + sc_ref[...])) + sh_ref[...]
    hb = h.astype(BF16)
    half = C_GROUP_DIM // 2
    for g in range(C_GROUPS):
        y = _dot(hb[:, g * C_GROUP_DIM:(g + 1) * C_GROUP_DIM], t_ref[...])
        yc_ref[:, g * half:(g + 1) * half] = y[:, :half].astype(BF16)
        ys_ref[:, g * half:(g + 1) * half] = y[:, half:].astype(BF16)
    yc_ref[:, F_HALF:] = _dot(hb, mid_ref[...]).astype(BF16)


def _fch(x2, gn, mod3, mod_row_fn, t_ch, t_mid, tm):
    t_tok = x2.shape[0]
    mod = lambda col: pl.BlockSpec((None, 1, D_MODEL), lambda i: (mod_row_fn(i), 0, col))
    tok = lambda width: pl.BlockSpec((tm, width), lambda i: (i, 0))
    return pl.pallas_call(
        _fch_kernel,
        out_shape=[jax.ShapeDtypeStruct((t_tok, F_HALF + F_MID), BF16),
                   jax.ShapeDtypeStruct((t_tok, F_HALF), BF16)],
        grid=(t_tok // tm,),
        in_specs=[tok(D_MODEL), pl.BlockSpec((1, D_MODEL), lambda i: (0, 0)), mod(0), mod(1),
                  pl.BlockSpec((C_GROUP_DIM, C_GROUP_DIM), lambda i: (0, 0)),
                  pl.BlockSpec((D_MODEL, F_MID), lambda i: (0, 0))],
        out_specs=[tok(F_HALF + F_MID), tok(F_HALF)],
        compiler_params=_cparams(("parallel",)),
        name="fch",
    )(x2, gn, mod3, mod3, t_ch, t_mid)


def _fpos_kernel(cn_ref, sn_ref, yc_ref, ys_ref, o_ref, *, scale):
    p = _dot(cn_ref[...], yc_ref[...])
    q = _dot(sn_ref[...], ys_ref[...])
    o_ref[:, :F_HALF] = ((p[:, :F_HALF] - q) * scale).astype(BF16)
    o_ref[:, F_HALF:2 * F_HALF] = ((p[:, :F_HALF] + q) * scale).astype(BF16)
    o_ref[:, 2 * F_HALF:] = (p[:, F_HALF:] * scale).astype(BF16)


def _fpos(cn, sn, yc, ys, bsz, n_lat, tk):
    nk = n_lat // tk
    scale = float((n_lat * C_GROUP_DIM) ** -0.5)
    mat = pl.BlockSpec((tk, n_lat), lambda b, k: (k, 0))
    return pl.pallas_call(
        functools.partial(_fpos_kernel, scale=scale),
        out_shape=jax.ShapeDtypeStruct((bsz * n_lat, F_COLS), BF16),
        grid=(bsz, nk),
        in_specs=[mat, mat,
                  pl.BlockSpec((n_lat, F_HALF + F_MID), lambda b, k: (b, 0)),
                  pl.BlockSpec((n_lat, F_HALF), lambda b, k: (b, 0))],
        out_specs=pl.BlockSpec((tk, F_COLS), lambda b, k: (b * nk + k, 0)),
        compiler_params=_cparams(("parallel", "parallel")),
        name="fpos",
    )(cn, sn, yc, ys)


def _linres_kernel(z_ref, w_ref, b_ref, x_ref, g1_ref, o_ref):
    y = _dot(z_ref[...], w_ref[...]) + b_ref[...]
    o_ref[...] = x_ref[...] + g1_ref[...] * y


def _linres(z, w, bias, x2, mod3, mod_row_fn, tm):
    t_tok = x2.shape[0]
    tok = lambda width: pl.BlockSpec((tm, width), lambda i: (i, 0))
    return pl.pallas_call(
        _linres_kernel,
        out_shape=jax.ShapeDtypeStruct((t_tok, D_MODEL), F32),
        grid=(t_tok // tm,),
        in_specs=[tok(z.shape[1]), pl.BlockSpec((z.shape[1], D_MODEL), lambda i: (0, 0)),
                  pl.BlockSpec((1, D_MODEL), lambda i: (0, 0)), tok(D_MODEL),
                  pl.BlockSpec((None, 1, D_MODEL), lambda i: (mod_row_fn(i), 0, 2))],
        out_specs=tok(D_MODEL),
        compiler_params=_cparams(("parallel",)),
        name="linres",
    )(z, w, bias, x2, mod3)


def _rope_tables(n_lat):
    t = jnp.arange(n_lat)
    row = (t // GRID_W).astype(F32)
    col = (t % GRID_W).astype(F32)
    n_freq = HEAD_DIM // 4
    inv_freq = ROPE_BASE ** (-jnp.arange(n_freq, dtype=F32) / n_freq)
    ang_r, ang_c = row[:, None] * inv_freq, col[:, None] * inv_freq
    cr, sr, cc, sc = jnp.cos(ang_r), jnp.sin(ang_r), jnp.cos(ang_c), jnp.sin(ang_c)
    zero = jnp.zeros_like(sr)
    cos = jnp.concatenate([cr, cr, cc, cc], axis=-1)
    sin_a = jnp.concatenate([-sr, zero, -sc, zero], axis=-1)
    sin_b = jnp.concatenate([zero, sr, zero, sc], axis=-1)
    return cos, sin_a, sin_b


def _dft_tables(n_lat):
    def cs(n):
        idx = np.arange(n, dtype=np.int64)
        ang = 2.0 * np.pi * ((idx[:, None] * idx[None, :]) % n).astype(np.float64) / n
        return np.cos(ang), np.sin(ang)

    half = C_GROUP_DIM // 2
    cm, sm = cs(C_GROUP_DIM)
    cn, sn = cs(n_lat)
    t_ch = np.concatenate([cm[:, :half], sm[:, :half]], axis=1)
    t_mid = np.zeros((D_MODEL, F_MID))
    for g in range(C_GROUPS):
        t_mid[g * C_GROUP_DIM:(g + 1) * C_GROUP_DIM, g] = cm[:, half]
    m = np.arange(half)
    base = np.arange(C_GROUPS)[:, None] * C_GROUP_DIM
    lo_rows = (base + m[None, :]).reshape(-1)
    hi_rows = (base + (C_GROUP_DIM - m[None, :]) % C_GROUP_DIM).reshape(-1)
    mid_rows = np.concatenate([np.arange(C_GROUPS) * C_GROUP_DIM + half, np.zeros(F_MID - C_GROUPS, np.int64)])
    rows = np.concatenate([lo_rows, hi_rows, mid_rows]).astype(np.int32)
    keep = np.concatenate([np.ones(F_HALF), np.tile(m > 0, C_GROUPS), np.arange(F_MID) < C_GROUPS]).astype(np.float32)
    as_bf16 = lambda t: jnp.asarray(t, dtype=F32).astype(BF16)
    return as_bf16(t_ch), as_bf16(t_mid), as_bf16(cn), as_bf16(sn), rows, keep


def kernel(x, c, ctx, c_ctx, ada_w, ada_b, norm_mix, norm_mlp, mlp_w1, mlp_w2, ab_w_in, ab_q_norm, ab_k_norm,
           ab_sink, ab_gk_f, ab_gk_f_bias, ab_gk_b, ab_gk_b_bias, ab_gla_norm, ab_w_out, c_w_out, c_b_out):
    bsz, n_lat, _ = x.shape
    n_ctx = ctx.shape[1]
    depth = ada_w.shape[0]
    assert depth == 2 and bsz < MOD_ROWS
    t_tok = bsz * n_lat

    cvec = jnp.concatenate([c, c_ctx[None, :], jnp.zeros((MOD_ROWS - bsz - 1, D_MODEL), F32)], axis=0)
    mod = _ada(cvec, ada_w, ada_b)
    mod3 = mod.reshape(depth * MOD_ROWS, 1, 6 * D_MODEL)

    w1 = mlp_w1.astype(BF16)
    w2 = mlp_w2.astype(BF16)
    x2 = x.reshape(t_tok, D_MODEL)

    tm = min(256, n_lat)
    tpb = n_lat // tm
    lat_row0 = lambda i: i // tpb
    w_in = jnp.pad(ab_w_in[0], ((0, 0), (0, IN_COLS_PAD - IN_COLS))).astype(BF16)
    gn_mix0 = norm_mix[0].reshape(1, D_MODEL)
    qn = ab_q_norm[0].reshape(1, HEAD_DIM)
    kn = ab_k_norm[0].reshape(1, HEAD_DIM)
    gkf = jnp.pad(ab_gk_f[0], ((0, LANE - B_GATE_RANK), (0, 0))).astype(BF16)
    gkb = jnp.pad(ab_gk_b[0], ((B_GATE_RANK, LANE - 2 * B_GATE_RANK), (0, 0))).astype(BF16)
    gkfb = ab_gk_f_bias[0].reshape(1, B_QK_W)
    gkbb = ab_gk_b_bias[0].reshape(1, B_QK_W)
    q, k, v, bq, bk, bv, gate, gf, gb = _proj(
        x2, gn_mix0, mod3, lat_row0, tpb, tm, w_in, qn, kn, _rope_tables(n_lat), gkf, gkfb, gkb, gkbb, True)
    tmc = min(256, n_ctx)
    kc, vc, bkc, bvc, gfc, gbc = _proj(
        ctx.reshape(bsz * n_ctx, D_MODEL), gn_mix0, mod3, lambda i: bsz, n_ctx // tmc, tmc, w_in, None, kn, None,
        gkf, gkfb, gkb, gkbb, False)

    oa = _attn(ab_sink[0], q, k, v, kc, vc, bsz, n_lat, n_ctx)
    ob = _gla(bq, bk, bv, gf, gb, bkc, bvc, gfc, gbc, bsz, n_lat, n_ctx)

    w_out = ab_w_out[0].astype(BF16)
    tmo = min(512, n_lat)
    tpbo = n_lat // tmo
    x2 = _mixout(oa, ob, gate, ab_gla_norm[0].reshape(1, B_DV), w_out[:A_Q_W], w_out[A_Q_W:], x2, mod3,
                 lambda i: i // tpbo, tmo)
    x2 = _mlp(x2, norm_mlp[0].reshape(1, D_MODEL), mod3, lambda i: i // tpbo, w1, w2, 0, tmo, 1024)

    t_ch, t_mid, cn, sn, w_rows, w_keep = _dft_tables(n_lat)
    row1 = lambda i: MOD_ROWS + i // tpbo
    yc, ys = _fch(x2, norm_mix[1].reshape(1, D_MODEL), mod3, row1, t_ch, t_mid, tmo)
    z = _fpos(cn, sn, yc, ys, bsz, n_lat, min(512, n_lat))
    w_mix = jnp.where(w_keep[:, None] > 0, c_w_out[0][w_rows], 0.0).astype(BF16)
    x2 = _linres(z, w_mix, c_b_out[0].reshape(1, D_MODEL), x2, mod3, row1, tmo)
    x2 = _mlp(x2, norm_mlp[1].reshape(1, D_MODEL), mod3, row1, w1, w2, 1, tmo, 1024)
    return x2.reshape(bsz, n_lat, D_MODEL)
```

```python
import functools

import numpy as np
import jax
import jax.numpy as jnp
from jax import lax
from jax.experimental import pallas as pl
from jax.experimental.pallas import tpu as pltpu

D_MODEL = 2048
GRID_W = 64
EPS = 1e-6
NEG_INF = -1e30
HEAD_DIM = 128
A_Q_HEADS = 8
A_KV_HEADS = 2
A_GROUP = A_Q_HEADS // A_KV_HEADS
WINDOW = 128
BLOCK_Q = 128
ROPE_BASE = 10000.0
B_HEADS = 4
B_DV = 256
B_DK = 128
B_GATE_RANK = 16
B_GATE_NORM = 16.0
B_CHUNK = 64
GLA_GROUP = 4
C_GROUPS = 8
C_GROUP_DIM = D_MODEL // C_GROUPS
D_FF = 4 * D_MODEL

A_Q_W = A_Q_HEADS * HEAD_DIM
A_KV_W = A_KV_HEADS * HEAD_DIM
B_QK_W = B_HEADS * B_DK
B_V_W = B_HEADS * B_DV
OFF_AQ = 0
OFF_AK = OFF_AQ + A_Q_W
OFF_AV = OFF_AK + A_KV_W
OFF_BQ = OFF_AV + A_KV_W
OFF_BK = OFF_BQ + B_QK_W
OFF_BV = OFF_BK + B_QK_W
OFF_GATE = OFF_BV + B_V_W
OFF_LR = OFF_GATE + B_V_W
IN_COLS = OFF_LR + 2 * B_GATE_RANK
LANE = 128
IN_COLS_PAD = OFF_LR + LANE
MOD_ROWS = 16

VMEM_LIMIT = 56 * 1024 * 1024

BF16 = jnp.bfloat16
F32 = jnp.float32


def _cparams(sem):
    return pltpu.CompilerParams(dimension_semantics=sem, vmem_limit_bytes=VMEM_LIMIT)


def _dot(a, b):
    return jnp.dot(a, b, preferred_element_type=F32)


def _dot_nt(a, b):
    return lax.dot_general(a, b, (((1,), (1,)), ((), ())), preferred_element_type=F32)


def _dot_tn(a, b):
    return lax.dot_general(a, b, (((0,), (0,)), ((), ())), preferred_element_type=F32)


def _tri_sum(tri, g):
    g_hi = g.astype(BF16)
    g_lo = (g - g_hi.astype(F32)).astype(BF16)
    d = g.shape[1]
    both = _dot(tri, jnp.concatenate([g_hi, g_lo], axis=1))
    return both[:, :d] + both[:, d:]


def _rms(xf, gain):
    return xf * lax.rsqrt(jnp.mean(xf * xf, axis=-1, keepdims=True) + EPS) * gain


def _ada_kernel(c_ref, w_ref, b_ref, o_ref):
    cv = c_ref[...]
    s = (cv * jax.nn.sigmoid(cv)).astype(BF16)
    o_ref[...] = _dot(s, w_ref[...].astype(BF16)) + b_ref[...]


def _ada(cvec, ada_w, ada_b):
    depth = ada_w.shape[0]
    n = ada_w.shape[2]
    tn = 1024
    return pl.pallas_call(
        _ada_kernel,
        out_shape=jax.ShapeDtypeStruct((depth, MOD_ROWS, n), F32),
        grid=(depth, n // tn),
        in_specs=[
            pl.BlockSpec((MOD_ROWS, D_MODEL), lambda l, j: (0, 0)),
            pl.BlockSpec((None, D_MODEL, tn), lambda l, j: (l, 0, j)),
            pl.BlockSpec((None, 1, tn), lambda l, j: (l, 0, j)),
        ],
        out_specs=pl.BlockSpec((None, MOD_ROWS, tn), lambda l, j: (l, 0, j)),
        compiler_params=_cparams(("parallel", "parallel")),
        name="ada",
    )(cvec, ada_w, ada_b.reshape(depth, 1, n))


def _proj_kernel(*refs, latent):
    if latent:
        (x_ref, gn_ref, sh_ref, sc_ref, w_ref, qn_ref, kn_ref, cos_ref, sa_ref, sb_ref,
         gkf_ref, gkfb_ref, gkb_ref, gkbb_ref,
         q_ref, k_ref, v_ref, bq_ref, bk_ref, bv_ref, gate_ref, gf_ref, gb_ref) = refs
    else:
        (x_ref, gn_ref, sh_ref, sc_ref, w_ref, kn_ref,
         gkf_ref, gkfb_ref, gkb_ref, gkbb_ref,
         k_ref, v_ref, bk_ref, bv_ref, gf_ref, gb_ref) = refs

    xt = x_ref[...]
    h = _rms(xt, gn_ref[...]) * (1.0 + sc_ref[...]) + sh_ref[...]
    hb = h.astype(BF16)

    def proj(off, width):
        return _dot(hb, w_ref[:, off:off + width])

    def rope(t):
        return (t * cos_ref[...] + pltpu.roll(t, HEAD_DIM - HEAD_DIM // 4, 1) * sa_ref[...]
                + pltpu.roll(t, HEAD_DIM // 4, 1) * sb_ref[...])

    if latent:
        for half in range(2):
            seg = proj(OFF_AQ + half * (A_Q_W // 2), A_Q_W // 2)
            for j in range(A_Q_HEADS // 2):
                t = _rms(seg[:, j * HEAD_DIM:(j + 1) * HEAD_DIM], qn_ref[...])
                t = rope(t) * HEAD_DIM ** -0.5
                hd = half * (A_Q_HEADS // 2) + j
                q_ref[:, hd * HEAD_DIM:(hd + 1) * HEAD_DIM] = t.astype(BF16)

    seg = proj(OFF_AK, 2 * A_KV_W)
    for j in range(A_KV_HEADS):
        t = _rms(seg[:, j * HEAD_DIM:(j + 1) * HEAD_DIM], kn_ref[...])
        if latent:
            t = rope(t)
        k_ref[:, j * HEAD_DIM:(j + 1) * HEAD_DIM] = t.astype(BF16)
    v_ref[...] = seg[:, A_KV_W:].astype(BF16)

    if latent:
        bq_ref[...] = proj(OFF_BQ, B_QK_W) * B_DK ** -0.5
    bk_ref[...] = proj(OFF_BK, B_QK_W)
    for half in range(2):
        bv_ref[:, half * (B_V_W // 2):(half + 1) * (B_V_W // 2)] = proj(OFF_BV + half * (B_V_W // 2), B_V_W // 2)
    if latent:
        for half in range(2):
            gate_ref[:, half * (B_V_W // 2):(half + 1) * (B_V_W // 2)] = proj(
                OFF_GATE + half * (B_V_W // 2), B_V_W // 2)

    lr = proj(OFF_LR, LANE).astype(BF16)

    def log_decay(gk_ref, gkb_ref, out_ref):
        z = _dot(lr, gk_ref[...]) + gkb_ref[...]
        out_ref[...] = (jnp.minimum(z, 0.0) - jnp.log1p(jnp.exp(-jnp.abs(z)))) / B_GATE_NORM

    log_decay(gkf_ref, gkfb_ref, gf_ref)
    log_decay(gkb_ref, gkbb_ref, gb_ref)


def _proj(x2, gn, mod3, mod_row_fn, tiles_per_batch, tm, w_in, qn, kn, rope_tabs, gkf, gkfb, gkb, gkbb, latent):
    t_tok = x2.shape[0]
    row = lambda width: pl.BlockSpec((1, width), lambda i: (0, 0))
    tok = lambda width: pl.BlockSpec((tm, width), lambda i: (i, 0))
    mod = lambda col: pl.BlockSpec((None, 1, D_MODEL), lambda i: (mod_row_fn(i), 0, col))
    in_specs = [tok(D_MODEL), row(D_MODEL), mod(0), mod(1),
                pl.BlockSpec((D_MODEL, IN_COLS_PAD), lambda i: (0, 0), pipeline_mode=pl.Buffered(1))]
    args = [x2, gn, mod3, mod3, w_in]
    if latent:
        in_specs += [row(HEAD_DIM), row(HEAD_DIM)]
        args += [qn, kn]
        in_specs += [pl.BlockSpec((tm, HEAD_DIM), lambda i: (i % tiles_per_batch, 0))] * 3
        args += list(rope_tabs)
    else:
        in_specs += [row(HEAD_DIM)]
        args += [kn]
    in_specs += [pl.BlockSpec((LANE, B_QK_W), lambda i: (0, 0)), row(B_QK_W)] * 2
    args += [gkf, gkfb, gkb, gkbb]

    def out(width, dtype):
        return jax.ShapeDtypeStruct((t_tok, width), dtype), tok(width)

    if latent:
        outs = [out(A_Q_W, BF16), out(A_KV_W, BF16), out(A_KV_W, BF16), out(B_QK_W, F32), out(B_QK_W, F32),
                out(B_V_W, F32), out(B_V_W, F32), out(B_QK_W, F32), out(B_QK_W, F32)]
    else:
        outs = [out(A_KV_W, BF16), out(A_KV_W, BF16), out(B_QK_W, F32), out(B_V_W, F32),
                out(B_QK_W, F32), out(B_QK_W, F32)]
    return pl.pallas_call(
        functools.partial(_proj_kernel, latent=latent),
        out_shape=[o[0] for o in outs],
        grid=(t_tok // tm,),
        in_specs=in_specs,
        out_specs=[o[1] for o in outs],
        compiler_params=_cparams(("parallel",)),
        name="proj_lat" if latent else "proj_ctx",
    )(*args)


def _attn_kernel(sink_ref, q_ref, k_ref, v_ref, kc_ref, vc_ref, o_ref, *, n_lat):
    kvh = pl.program_id(1)
    n_win = 3 * BLOCK_Q
    heads = range(A_GROUP)
    kc = kc_ref[...]
    vc = vc_ref[...]
    sk = jnp.concatenate(
        [jnp.full((1, BLOCK_Q), sink_ref[kvh * A_GROUP + g], F32) for g in heads], axis=1)
    key_row = lax.broadcasted_iota(jnp.int32, (n_win, BLOCK_Q), 0)
    query_col = lax.broadcasted_iota(jnp.int32, (n_win, BLOCK_Q), 1)

    def scores(n):
        start = jnp.clip(n * BLOCK_Q - BLOCK_Q, 0, n_lat - n_win)
        start = pl.multiple_of(start, BLOCK_Q)
        rows = pl.ds(pl.multiple_of(n * BLOCK_Q, BLOCK_Q), BLOCK_Q)
        kw = k_ref[pl.ds(start, n_win), :]
        valid = jnp.abs((n * BLOCK_Q + query_col) - (start + key_row)) <= WINDOW
        q = jnp.concatenate([q_ref[rows, g * HEAD_DIM:(g + 1) * HEAD_DIM] for g in heads], axis=0)
        s_lat = _dot_nt(kw, q)
        s_lat = jnp.concatenate(
            [jnp.where(valid, s_lat[:, g * BLOCK_Q:(g + 1) * BLOCK_Q], NEG_INF) for g in heads], axis=1)
        s_ctx = _dot_nt(kc, q)
        return start, rows, s_lat, s_ctx

    def softmax(s_lat, s_ctx):
        m = jnp.maximum(jnp.maximum(jnp.max(s_lat, axis=0, keepdims=True),
                                    jnp.max(s_ctx, axis=0, keepdims=True)), sk)
        p_lat = jnp.exp(s_lat - m)
        p_ctx = jnp.exp(s_ctx - m)
        den = (jnp.sum(p_lat, axis=0, keepdims=True) + jnp.sum(p_ctx, axis=0, keepdims=True)
               + jnp.exp(sk - m))
        return p_lat.astype(BF16), p_ctx.astype(BF16), 1.0 / den

    def weighted_values(start, rows, p_lat, p_ctx, inv_den):
        vw = v_ref[pl.ds(start, n_win), :]
        o_t = (_dot_tn(vw, p_lat) + _dot_tn(vc, p_ctx)) * inv_den
        for g in heads:
            o_ref[rows, g * HEAD_DIM:(g + 1) * HEAD_DIM] = o_t[:, g * BLOCK_Q:(g + 1) * BLOCK_Q].T.astype(BF16)

    nb = n_lat // BLOCK_Q
    per_step = 4 if nb % 4 == 0 else 1

    def step(i, carry):
        sc = [scores(i * per_step + u) for u in range(per_step)]
        pr = [softmax(s_lat, s_ctx) for _, _, s_lat, s_ctx in sc]
        for (start, rows, _, _), (p_lat, p_ctx, inv_den) in zip(sc, pr):
            weighted_values(start, rows, p_lat, p_ctx, inv_den)
        return carry

    lax.fori_loop(0, nb // per_step, step, 0)


def _attn(sink, q, k, v, kc, vc, bsz, n_lat, n_ctx):
    gw = A_GROUP * HEAD_DIM
    blk = lambda rows, width: pl.BlockSpec((rows, width), lambda b, h: (b, h))
    return pl.pallas_call(
        functools.partial(_attn_kernel, n_lat=n_lat),
        out_shape=jax.ShapeDtypeStruct((bsz * n_lat, A_Q_W), BF16),
        grid=(bsz, A_KV_HEADS),
        in_specs=[pl.BlockSpec(memory_space=pltpu.SMEM), blk(n_lat, gw), blk(n_lat, HEAD_DIM),
                  blk(n_lat, HEAD_DIM), blk(n_ctx, HEAD_DIM), blk(n_ctx, HEAD_DIM)],
        out_specs=blk(n_lat, gw),
        compiler_params=_cparams(("parallel", "parallel")),
        name="attn",
    )(sink, q, k, v, kc, vc)


def _gla_kernel(q_ref, k_ref, v_ref, gf_ref, gb_ref, kc_ref, vc_ref, gfc_ref, gbc_ref, o_ref,
                sf_ref, sb_ref, ob_ref, *, n_lat, n_ctx):
    c = B_CHUNK

    def tri(nn, fn):
        r = lax.broadcasted_iota(jnp.int32, (nn, nn), 0)
        s = lax.broadcasted_iota(jnp.int32, (nn, nn), 1)
        return fn(r, s)

    kc = kc_ref[...]
    vcb = vc_ref[...].astype(BF16)
    up_strict = tri(n_ctx, lambda r, s: s > r).astype(BF16)
    lo_strict = tri(n_ctx, lambda r, s: s < r).astype(BF16)
    kd_f = (kc * jnp.exp(_tri_sum(up_strict, gfc_ref[...]))).astype(BF16)
    kd_b = (kc * jnp.exp(_tri_sum(lo_strict, gbc_ref[...]))).astype(BF16)
    sf_ref[...] = _dot_tn(vcb, kd_f)
    sb_ref[...] = _dot_tn(vcb, kd_b)

    r = c * GLA_GROUP
    ng = n_lat // r
    same = tri(r, lambda i, j: (i // c) == (j // c))
    lo_mask = jnp.logical_and(same, tri(r, lambda i, j: j <= i))
    up_mask = jnp.logical_and(same, tri(r, lambda i, j: j >= i))
    lo_incl = lo_mask.astype(BF16)
    up_incl = up_mask.astype(BF16)
    row_chunk = lax.broadcasted_iota(jnp.int32, (r, B_DK), 0) // c

    per_dir = 2 if ng % 2 == 0 else 1

    def decay_sums(rows, g_ref, tri_incl, last):
        b = _tri_sum(tri_incl, g_ref[rows, :])
        bl = jnp.concatenate(
            [jnp.broadcast_to(b[n * c + last:n * c + last + 1, :], (c, B_DK)) for n in range(GLA_GROUP)], axis=0)
        return b, bl

    def scaled_operands(rows, b, bl):
        q = q_ref[rows, :]
        k = k_ref[rows, :]
        qe = (q * jnp.exp(b)).astype(BF16)
        ke = (k * jnp.exp(-b)).astype(BF16)
        kd = k * jnp.exp(bl - b)
        kd_blocks = jnp.concatenate(
            [jnp.where(row_chunk == n, kd, 0.0).astype(BF16) for n in range(GLA_GROUP)], axis=1)
        return qe, ke, kd_blocks, jnp.exp(bl), v_ref[rows, :].astype(BF16)

    def body(i, carry):
        fwd = [pl.ds(pl.multiple_of((i * per_dir + u) * r, r), r) for u in range(per_dir)]
        bwd = [pl.ds(pl.multiple_of((ng - 1 - i * per_dir - u) * r, r), r) for u in range(per_dir)]
        streams = ([(rows, gf_ref, lo_incl, lo_mask, c - 1) for rows in fwd]
                   + [(rows, gb_ref, up_incl, up_mask, 0) for rows in bwd])
        sums = [decay_sums(rows, g_ref, tri_incl, last) for rows, g_ref, tri_incl, _, last in streams]
        ops = [scaled_operands(strm[0], b, bl) for strm, (b, bl) in zip(streams, sums)]
        a = [jnp.where(strm[3], _dot_nt(qe, ke), 0.0).astype(BF16)
             for strm, (qe, ke, _, _, _) in zip(streams, ops)]
        o_intra = [_dot(a_s, vb) for a_s, (_, _, _, _, vb) in zip(a, ops)]
        kv = [_dot_tn(vb, kd_blocks) for _, _, kd_blocks, _, vb in ops]
        o_inter = [[None] * GLA_GROUP for _ in streams]
        st_f, st_b = sf_ref[...], sb_ref[...]
        for u in range(per_dir):
            for step in range(GLA_GROUP):
                for s, n in ((u, step), (per_dir + u, GLA_GROUP - 1 - step)):
                    qe, _, _, dec, _ = ops[s]
                    st = st_f if s < per_dir else st_b
                    o_inter[s][n] = _dot_nt(qe[n * c:(n + 1) * c], st.astype(BF16))
                    st = st * dec[n * c:n * c + 1, :] + kv[s][:, n * B_DK:(n + 1) * B_DK]
                    if s < per_dir:
                        st_f = st
                    else:
                        st_b = st
        sf_ref[...] = st_f
        sb_ref[...] = st_b
        for s, (rows, *_) in enumerate(streams):
            out_ref = o_ref if s < per_dir else ob_ref
            out_ref[rows, :] = o_intra[s] + jnp.concatenate(o_inter[s], axis=0)
        return carry

    lax.fori_loop(0, ng // per_dir, body, 0)
    o_ref[...] += ob_ref[...]


def _gla(bq, bk, bv, gf, gb, bkc, bvc, gfc, gbc, bsz, n_lat, n_ctx):
    lat = lambda width: pl.BlockSpec((n_lat, width), lambda b, h: (b, h))
    cx = lambda width: pl.BlockSpec((n_ctx, width), lambda b, h: (b, h))
    return pl.pallas_call(
        functools.partial(_gla_kernel, n_lat=n_lat, n_ctx=n_ctx),
        out_shape=jax.ShapeDtypeStruct((bsz * n_lat, B_V_W), F32),
        grid=(bsz, B_HEADS),
        in_specs=[lat(B_DK), lat(B_DK), lat(B_DV), lat(B_DK), lat(B_DK),
                  cx(B_DK), cx(B_DV), cx(B_DK), cx(B_DK)],
        out_specs=lat(B_DV),
        scratch_shapes=[pltpu.VMEM((B_DV, B_DK), F32), pltpu.VMEM((B_DV, B_DK), F32),
                        pltpu.VMEM((n_lat, B_DV), F32)],
        compiler_params=_cparams(("parallel", "parallel")),
        name="gla",
    )(bq, bk, bv, gf, gb, bkc, bvc, gfc, gbc)


def _mixout_kernel(oa_ref, ob_ref, gate_ref, gn_ref, wa_ref, wb_ref, x_ref, g1_ref, o_ref):
    y = _dot(oa_ref[...], wa_ref[...])
    for hd in range(B_HEADS):
        cols = slice(hd * B_DV, (hd + 1) * B_DV)
        gt = gate_ref[:, cols]
        t = _rms(ob_ref[:, cols], gn_ref[...]) * (gt * jax.nn.sigmoid(gt))
        y = y + _dot(t.astype(BF16), wb_ref[cols, :])
    o_ref[...] = x_ref[...] + g1_ref[...] * y


def _mixout(oa, ob, gate, gla_norm, w_a, w_b, x2, mod3, mod_row_fn, tm):
    t_tok = x2.shape[0]
    tok = lambda width: pl.BlockSpec((tm, width), lambda i: (i, 0))
    const = lambda shape: pl.BlockSpec(shape, lambda i: (0, 0))
    return pl.pallas_call(
        _mixout_kernel,
        out_shape=jax.ShapeDtypeStruct((t_tok, D_MODEL), F32),
        grid=(t_tok // tm,),
        in_specs=[tok(A_Q_W), tok(B_V_W), tok(B_V_W), const((1, B_DV)),
                  const((A_Q_W, D_MODEL)), const((B_V_W, D_MODEL)), tok(D_MODEL),
                  pl.BlockSpec((None, 1, D_MODEL), lambda i: (mod_row_fn(i), 0, 2))],
        out_specs=tok(D_MODEL),
        compiler_params=_cparams(("parallel",)),
        name="mixout",
    )(oa, ob, gate, gla_norm, w_a, w_b, x2, mod3)


def _mlp_kernel(x_ref, gn_ref, sh_ref, sc_ref, g2_ref, w1_ref, w2_ref, o_ref, h_ref):
    j = pl.program_id(1)

    @pl.when(j == 0)
    def _():
        xt = x_ref[...]
        h = _rms(xt, gn_ref[...] * (1.0 + sc_ref[...])) + sh_ref[...]
        h_ref[...] = h.astype(BF16)
        o_ref[...] = xt

    u = jnp.maximum(_dot(h_ref[...], w1_ref[...]), 0.0)
    o_ref[...] += g2_ref[...] * _dot((u * u).astype(BF16), w2_ref[...])


def _mlp(x2, gn, mod3, mod_row_fn, w1, w2, layer, tm, tf):
    t_tok = x2.shape[0]
    mod = lambda col: pl.BlockSpec((None, 1, D_MODEL), lambda i, j: (mod_row_fn(i), 0, col))
    return pl.pallas_call(
        _mlp_kernel,
        out_shape=jax.ShapeDtypeStruct((t_tok, D_MODEL), F32),
        grid=(t_tok // tm, D_FF // tf),
        in_specs=[pl.BlockSpec((tm, D_MODEL), lambda i, j: (i, 0)),
                  pl.BlockSpec((1, D_MODEL), lambda i, j: (0, 0)),
                  mod(3), mod(4), mod(5),
                  pl.BlockSpec((None, D_MODEL, tf), lambda i, j: (layer, 0, j)),
                  pl.BlockSpec((None, tf, D_MODEL), lambda i, j: (layer, j, 0))],
        out_specs=pl.BlockSpec((tm, D_MODEL), lambda i, j: (i, 0)),
        scratch_shapes=[pltpu.VMEM((tm, D_MODEL), BF16)],
        compiler_params=_cparams(("parallel", "arbitrary")),
        name="mlp",
    )(x2, gn, mod3, mod3, mod3, w1, w2)


F_HALF = C_GROUPS * (C_GROUP_DIM // 2)
F_MID = LANE
F_COLS = 2 * F_HALF + F_MID


def _fch_kernel(x_ref, gn_ref, sh_ref, sc_ref, t_ref, mid_ref, yc_ref, ys_ref):
    h = _rms(x_ref[...], gn_ref[...] * (1.0 + sc_ref[...])) + sh_ref[...]
    hb = h.astype(BF16)
    half = C_GROUP_DIM // 2
    for g in range(C_GROUPS):
        y = _dot(hb[:, g * C_GROUP_DIM:(g + 1) * C_GROUP_DIM], t_ref[...])
        yc_ref[:, g * half:(g + 1) * half] = y[:, :half].astype(BF16)
        ys_ref[:, g * half:(g + 1) * half] = y[:, half:].astype(BF16)
    yc_ref[:, F_HALF:] = _dot(hb, mid_ref[...]).astype(BF16)


def _fch(x2, gn, mod3, mod_row_fn, t_ch, t_mid, tm):
    t_tok = x2.shape[0]
    mod = lambda col: pl.BlockSpec((None, 1, D_MODEL), lambda i: (mod_row_fn(i), 0, col))
    tok = lambda width: pl.BlockSpec((tm, width), lambda i: (i, 0))
    return pl.pallas_call(
        _fch_kernel,
        out_shape=[jax.ShapeDtypeStruct((t_tok, F_HALF + F_MID), BF16),
                   jax.ShapeDtypeStruct((t_tok, F_HALF), BF16)],
        grid=(t_tok // tm,),
        in_specs=[tok(D_MODEL), pl.BlockSpec((1, D_MODEL), lambda i: (0, 0)), mod(0), mod(1),
                  pl.BlockSpec((C_GROUP_DIM, C_GROUP_DIM), lambda i: (0, 0)),
                  pl.BlockSpec((D_MODEL, F_MID), lambda i: (0, 0))],
        out_specs=[tok(F_HALF + F_MID), tok(F_HALF)],
        compiler_params=_cparams(("parallel",)),
        name="fch",
    )(x2, gn, mod3, mod3, t_ch, t_mid)


def _fpos_kernel(cn_ref, sn_ref, yc_ref, ys_ref, o_ref, *, scale):
    p = _dot(cn_ref[...], yc_ref[...])
    q = _dot(sn_ref[...], ys_ref[...])
    o_ref[:, :F_HALF] = ((p[:, :F_HALF] - q) * scale).astype(BF16)
    o_ref[:, F_HALF:2 * F_HALF] = ((p[:, :F_HALF] + q) * scale).astype(BF16)
    o_ref[:, 2 * F_HALF:] = (p[:, F_HALF:] * scale).astype(BF16)


def _fpos(cn, sn, yc, ys, bsz, n_lat, tk):
    nk = n_lat // tk
    scale = float((n_lat * C_GROUP_DIM) ** -0.5)
    mat = pl.BlockSpec((tk, n_lat), lambda b, k: (k, 0))
    return pl.pallas_call(
        functools.partial(_fpos_kernel, scale=scale),
        out_shape=jax.ShapeDtypeStruct((bsz * n_lat, F_COLS), BF16),
        grid=(bsz, nk),
        in_specs=[mat, mat,
                  pl.BlockSpec((n_lat, F_HALF + F_MID), lambda b, k: (b, 0)),
                  pl.BlockSpec((n_lat, F_HALF), lambda b, k: (b, 0))],
        out_specs=pl.BlockSpec((tk, F_COLS), lambda b, k: (b * nk + k, 0)),
        compiler_params=_cparams(("parallel", "parallel")),
        name="fpos",
    )(cn, sn, yc, ys)


def _linres_kernel(z_ref, w_ref, b_ref, x_ref, g1_ref, o_ref):
    y = _dot(z_ref[...], w_ref[...]) + b_ref[...]
    o_ref[...] = x_ref[...] + g1_ref[...] * y


def _linres(z, w, bias, x2, mod3, mod_row_fn, tm):
    t_tok = x2.shape[0]
    tok = lambda width: pl.BlockSpec((tm, width), lambda i: (i, 0))
    return pl.pallas_call(
        _linres_kernel,
        out_shape=jax.ShapeDtypeStruct((t_tok, D_MODEL), F32),
        grid=(t_tok // tm,),
        in_specs=[tok(z.shape[1]), pl.BlockSpec((z.shape[1], D_MODEL), lambda i: (0, 0)),
                  pl.BlockSpec((1, D_MODEL), lambda i: (0, 0)), tok(D_MODEL),
                  pl.BlockSpec((None, 1, D_MODEL), lambda i: (mod_row_fn(i), 0, 2))],
        out_specs=tok(D_MODEL),
        compiler_params=_cparams(("parallel",)),
        name="linres",
    )(z, w, bias, x2, mod3)


def _rope_tables(n_lat):
    t = jnp.arange(n_lat)
    row = (t // GRID_W).astype(F32)
    col = (t % GRID_W).astype(F32)
    n_freq = HEAD_DIM // 4
    inv_freq = ROPE_BASE ** (-jnp.arange(n_freq, dtype=F32) / n_freq)
    ang_r, ang_c = row[:, None] * inv_freq, col[:, None] * inv_freq
    cr, sr, cc, sc = jnp.cos(ang_r), jnp.sin(ang_r), jnp.cos(ang_c), jnp.sin(ang_c)
    zero = jnp.zeros_like(sr)
    cos = jnp.concatenate([cr, cr, cc, cc], axis=-1)
    sin_a = jnp.concatenate([-sr, zero, -sc, zero], axis=-1)
    sin_b = jnp.concatenate([zero, sr, zero, sc], axis=-1)
    return cos, sin_a, sin_b


def _dft_tables(n_lat):
    def cs(n):
        idx = np.arange(n, dtype=np.int64)
        ang = 2.0 * np.pi * ((idx[:, None] * idx[None, :]) % n).astype(np.float64) / n
        return np.cos(ang), np.sin(ang)

    half = C_GROUP_DIM // 2
    cm, sm = cs(C_GROUP_DIM)
    cn, sn = cs(n_lat)
    t_ch = np.concatenate([cm[:, :half], sm[:, :half]], axis=1)
    t_mid = np.zeros((D_MODEL, F_MID))
    for g in range(C_GROUPS):
        t_mid[g * C_GROUP_DIM:(g + 1) * C_GROUP_DIM, g] = cm[:, half]
    m = np.arange(half)
    base = np.arange(C_GROUPS)[:, None] * C_GROUP_DIM
    lo_rows = (base + m[None, :]).reshape(-1)
    hi_rows = (base + (C_GROUP_DIM - m[None, :]) % C_GROUP_DIM).reshape(-1)
    mid_rows = np.concatenate([np.arange(C_GROUPS) * C_GROUP_DIM + half, np.zeros(F_MID - C_GROUPS, np.int64)])
    rows = np.concatenate([lo_rows, hi_rows, mid_rows]).astype(np.int32)
    keep = np.concatenate([np.ones(F_HALF), np.tile(m > 0, C_GROUPS), np.arange(F_MID) < C_GROUPS]).astype(np.float32)
    as_bf16 = lambda t: jnp.asarray(t, dtype=F32).astype(BF16)
    return as_bf16(t_ch), as_bf16(t_mid), as_bf16(cn), as_bf16(sn), rows, keep


def kernel(x, c, ctx, c_ctx, ada_w, ada_b, norm_mix, norm_mlp, mlp_w1, mlp_w2, ab_w_in, ab_q_norm, ab_k_norm,
           ab_sink, ab_gk_f, ab_gk_f_bias, ab_gk_b, ab_gk_b_bias, ab_gla_norm, ab_w_out, c_w_out, c_b_out):
    bsz, n_lat, _ = x.shape
    n_ctx = ctx.shape[1]
    depth = ada_w.shape[0]
    assert depth == 2 and bsz < MOD_ROWS
    t_tok = bsz * n_lat

    cvec = jnp.concatenate([c, c_ctx[None, :], jnp.zeros((MOD_ROWS - bsz - 1, D_MODEL), F32)], axis=0)
    mod = _ada(cvec, ada_w, ada_b)
    mod3 = mod.reshape(depth * MOD_ROWS, 1, 6 * D_MODEL)

    w1 = mlp_w1.astype(BF16)
    w2 = mlp_w2.astype(BF16)
    x2 = x.reshape(t_tok, D_MODEL)

    tm = min(256, n_lat)
    tpb = n_lat // tm
    lat_row0 = lambda i: i // tpb
    w_in = jnp.pad(ab_w_in[0], ((0, 0), (0, IN_COLS_PAD - IN_COLS))).astype(BF16)
    gn_mix0 = norm_mix[0].reshape(1, D_MODEL)
    qn = ab_q_norm[0].reshape(1, HEAD_DIM)
    kn = ab_k_norm[0].reshape(1, HEAD_DIM)
    gkf = jnp.pad(ab_gk_f[0], ((0, LANE - B_GATE_RANK), (0, 0))).astype(BF16)
    gkb = jnp.pad(ab_gk_b[0], ((B_GATE_RANK, LANE - 2 * B_GATE_RANK), (0, 0))).astype(BF16)
    gkfb = ab_gk_f_bias[0].reshape(1, B_QK_W)
    gkbb = ab_gk_b_bias[0].reshape(1, B_QK_W)
    q, k, v, bq, bk, bv, gate, gf, gb = _proj(
        x2, gn_mix0, mod3, lat_row0, tpb, tm, w_in, qn, kn, _rope_tables(n_lat), gkf, gkfb, gkb, gkbb, True)
    tmc = min(256, n_ctx)
    kc, vc, bkc, bvc, gfc, gbc = _proj(
        ctx.reshape(bsz * n_ctx, D_MODEL), gn_mix0, mod3, lambda i: bsz, n_ctx // tmc, tmc, w_in, None, kn, None,
        gkf, gkfb, gkb, gkbb, False)

    oa = _attn(ab_sink[0], q, k, v, kc, vc, bsz, n_lat, n_ctx)
    ob = _gla(bq, bk, bv, gf, gb, bkc, bvc, gfc, gbc, bsz, n_lat, n_ctx)

    w_out = ab_w_out[0].astype(BF16)
    tmo = min(512, n_lat)
    tpbo = n_lat // tmo
    x2 = _mixout(oa, ob, gate, ab_gla_norm[0].reshape(1, B_DV), w_out[:A_Q_W], w_out[A_Q_W:], x2, mod3,
                 lambda i: i // tpbo, tmo)
    x2 = _mlp(x2, norm_mlp[0].reshape(1, D_MODEL), mod3, lambda i: i // tpbo, w1, w2, 0, tmo, 1024)

    t_ch, t_mid, cn, sn, w_rows, w_keep = _dft_tables(n_lat)
    row1 = lambda i: MOD_ROWS + i // tpbo
    yc, ys = _fch(x2, norm_mix[1].reshape(1, D_MODEL), mod3, row1, t_ch, t_mid, tmo)
    z = _fpos(cn, sn, yc, ys, bsz, n_lat, min(512, n_lat))
    w_mix = jnp.where(w_keep[:, None] > 0, c_w_out[0][w_rows], 0.0).astype(BF16)
    x2 = _linres(z, w_mix, c_b_out[0].reshape(1, D_MODEL), x2, mod3, row1, tmo)
    x2 = _mlp(x2, norm_mlp[1].reshape(1, D_MODEL), mod3, row1, w1, w2, 1, tmo, 1024)
    return x2.reshape(bsz, n_lat, D_MODEL)
```

```python
import functools

import numpy as np
import jax
import jax.numpy as jnp
from jax import lax
from jax.experimental import pallas as pl
from jax.experimental.pallas import tpu as pltpu

D_MODEL = 2048
GRID_W = 64
EPS = 1e-6
NEG_INF = -1e30
HEAD_DIM = 128
A_Q_HEADS = 8
A_KV_HEADS = 2
A_GROUP = A_Q_HEADS // A_KV_HEADS
WINDOW = 128
BLOCK_Q = 128
ROPE_BASE = 10000.0
B_HEADS = 4
B_DV = 256
B_DK = 128
B_GATE_RANK = 16
B_GATE_NORM = 16.0
B_CHUNK = 64
GLA_GROUP = 4
C_GROUPS = 8
C_GROUP_DIM = D_MODEL // C_GROUPS
D_FF = 4 * D_MODEL

A_Q_W = A_Q_HEADS * HEAD_DIM
A_KV_W = A_KV_HEADS * HEAD_DIM
B_QK_W = B_HEADS * B_DK
B_V_W = B_HEADS * B_DV
OFF_AQ = 0
OFF_AK = OFF_AQ + A_Q_W
OFF_AV = OFF_AK + A_KV_W
OFF_BQ = OFF_AV + A_KV_W
OFF_BK = OFF_BQ + B_QK_W
OFF_BV = OFF_BK + B_QK_W
OFF_GATE = OFF_BV + B_V_W
OFF_LR = OFF_GATE + B_V_W
IN_COLS = OFF_LR + 2 * B_GATE_RANK
LANE = 128
IN_COLS_PAD = OFF_LR + LANE
BF16_SUBLANES = 16
MOD_ROWS = BF16_SUBLANES
VMEM_LIMIT = 56 * 1024 * 1024
CAST_BLOCK_BYTES = 2 * 1024 * 1024

BF16 = jnp.bfloat16
F32 = jnp.float32


def _cparams(sem):
    return pltpu.CompilerParams(dimension_semantics=sem, vmem_limit_bytes=VMEM_LIMIT)


def _dot(a, b):
    return jnp.dot(a, b, preferred_element_type=F32)


def _dot_nt(a, b):
    return lax.dot_general(a, b, (((1,), (1,)), ((), ())), preferred_element_type=F32)


def _dot_tn(a, b):
    return lax.dot_general(a, b, (((0,), (0,)), ((), ())), preferred_element_type=F32)


def _tri_sum(tri, g):
    g_hi = g.astype(BF16)
    g_lo = (g - g_hi.astype(F32)).astype(BF16)
    d = g.shape[1]
    both = _dot(tri, jnp.concatenate([g_hi, g_lo], axis=1))
    return both[:, :d] + both[:, d:]


def _rms(xf, gain):
    return xf * lax.rsqrt(jnp.mean(xf * xf, axis=-1, keepdims=True) + EPS) * gain


def _ada_kernel(c_ref, w_ref, b_ref, o_ref):
    cv = c_ref[...]
    s = (cv * jax.nn.sigmoid(cv)).astype(BF16)
    o_ref[...] = _dot(s, w_ref[...].astype(BF16)) + b_ref[...]


def _ada(cvec, ada_w, ada_b):
    depth = ada_w.shape[0]
    n = ada_w.shape[2]
    tn = 1024
    return pl.pallas_call(
        _ada_kernel,
        out_shape=jax.ShapeDtypeStruct((depth, MOD_ROWS, n), F32),
        grid=(depth, n // tn),
        in_specs=[
            pl.BlockSpec((MOD_ROWS, D_MODEL), lambda l, j: (0, 0)),
            pl.BlockSpec((None, D_MODEL, tn), lambda l, j: (l, 0, j)),
            pl.BlockSpec((None, 1, tn), lambda l, j: (l, 0, j)),
        ],
        out_specs=pl.BlockSpec((None, MOD_ROWS, tn), lambda l, j: (l, 0, j)),
        compiler_params=_cparams(("parallel", "parallel")),
        name="ada",
    )(cvec, ada_w, ada_b.reshape(depth, 1, n))


def _proj_kernel(*refs, latent, n_cast):
    for src_ref, dst_ref in zip(refs[:n_cast], refs[len(refs) - n_cast:]):
        dst_ref[...] = src_ref[...].astype(BF16)
    refs = refs[n_cast:len(refs) - n_cast]
    if latent:
        (x_ref, gn_ref, sh_ref, sc_ref, w_ref, qn_ref, kn_ref, cos_ref, sa_ref, sb_ref,
         gkf_ref, gkfb_ref, gkb_ref, gkbb_ref,
         q_ref, k_ref, v_ref, bq_ref, bk_ref, bv_ref, gate_ref, gf_ref, gb_ref) = refs
    else:
        (x_ref, gn_ref, sh_ref, sc_ref, w_ref, kn_ref,
         gkf_ref, gkfb_ref, gkb_ref, gkbb_ref,
         k_ref, v_ref, bk_ref, bv_ref, gf_ref, gb_ref) = refs

    xt = x_ref[...]
    h = _rms(xt, gn_ref[...]) * (1.0 + sc_ref[...]) + sh_ref[...]
    hb = h.astype(BF16)

    def proj(off, width):
        return _dot(hb, w_ref[:, off:off + width])

    def rope(t):
        return (t * cos_ref[...] + pltpu.roll(t, HEAD_DIM - HEAD_DIM // 4, 1) * sa_ref[...]
                + pltpu.roll(t, HEAD_DIM // 4, 1) * sb_ref[...])

    if latent:
        for half in range(2):
            seg = proj(OFF_AQ + half * (A_Q_W // 2), A_Q_W // 2)
            for j in range(A_Q_HEADS // 2):
                t = _rms(seg[:, j * HEAD_DIM:(j + 1) * HEAD_DIM], qn_ref[...])
                t = rope(t) * HEAD_DIM ** -0.5
                hd = half * (A_Q_HEADS // 2) + j
                q_ref[:, hd * HEAD_DIM:(hd + 1) * HEAD_DIM] = t.astype(BF16)

    seg = proj(OFF_AK, 2 * A_KV_W)
    for j in range(A_KV_HEADS):
        t = _rms(seg[:, j * HEAD_DIM:(j + 1) * HEAD_DIM], kn_ref[...])
        if latent:
            t = rope(t)
        k_ref[:, j * HEAD_DIM:(j + 1) * HEAD_DIM] = t.astype(BF16)
    v_ref[...] = seg[:, A_KV_W:].astype(BF16)

    if latent:
        bq_ref[...] = proj(OFF_BQ, B_QK_W) * B_DK ** -0.5
    bk_ref[...] = proj(OFF_BK, B_QK_W)
    for half in range(2):
        bv_ref[:, half * (B_V_W // 2):(half + 1) * (B_V_W // 2)] = proj(OFF_BV + half * (B_V_W // 2), B_V_W // 2)
    if latent:
        for half in range(2):
            gate_ref[:, half * (B_V_W // 2):(half + 1) * (B_V_W // 2)] = proj(
                OFF_GATE + half * (B_V_W // 2), B_V_W // 2)

    lr = proj(OFF_LR, LANE).astype(BF16)

    def log_decay(gk_ref, gkb_ref, out_ref):
        z = _dot(lr, gk_ref[...]) + gkb_ref[...]
        out_ref[...] = (jnp.minimum(z, 0.0) - jnp.log1p(jnp.exp(-jnp.abs(z)))) / B_GATE_NORM

    log_decay(gkf_ref, gkfb_ref, gf_ref)
    log_decay(gkb_ref, gkbb_ref, gb_ref)


def _proj(x2, gn, mod3, mod_row_fn, tiles_per_batch, tm, w_in, qn, kn, rope_tabs, gkf, gkfb, gkb, gkbb, latent,
          cast=()):
    t_tok = x2.shape[0]
    steps = t_tok // tm
    row = lambda width: pl.BlockSpec((1, width), lambda i: (0, 0))
    tok = lambda width: pl.BlockSpec((tm, width), lambda i: (i, 0))
    mod = lambda col: pl.BlockSpec((None, 1, D_MODEL), lambda i: (mod_row_fn(i), 0, col))
    in_specs = [tok(D_MODEL), row(D_MODEL), mod(0), mod(1),
                pl.BlockSpec((D_MODEL, IN_COLS_PAD), lambda i: (0, 0), pipeline_mode=pl.Buffered(1))]
    args = [x2, gn, mod3, mod3, w_in]
    if latent:
        in_specs += [row(HEAD_DIM), row(HEAD_DIM)]
        args += [qn, kn]
        in_specs += [pl.BlockSpec((tm, HEAD_DIM), lambda i: (i % tiles_per_batch, 0))] * 3
        args += list(rope_tabs)
    else:
        in_specs += [row(HEAD_DIM)]
        args += [kn]
    in_specs += [pl.BlockSpec((LANE, B_QK_W), lambda i: (0, 0)), row(B_QK_W)] * 2
    args += [gkf, gkfb, gkb, gkbb]

    def out(width, dtype):
        return jax.ShapeDtypeStruct((t_tok, width), dtype), tok(width)

    if latent:
        outs = [out(A_Q_W, BF16), out(A_KV_W, BF16), out(A_KV_W, BF16), out(B_QK_W, F32), out(B_QK_W, F32),
                out(B_V_W, F32), out(B_V_W, F32), out(B_QK_W, F32), out(B_QK_W, F32)]
    else:
        outs = [out(A_KV_W, BF16), out(A_KV_W, BF16), out(B_QK_W, F32), out(B_V_W, F32),
                out(B_QK_W, F32), out(B_QK_W, F32)]
    cast_specs = [pl.BlockSpec((a.shape[0] // steps, a.shape[1]), lambda i: (i, 0)) for a in cast]
    outs += [(jax.ShapeDtypeStruct(a.shape, BF16), spec) for a, spec in zip(cast, cast_specs)]
    return pl.pallas_call(
        functools.partial(_proj_kernel, latent=latent, n_cast=len(cast)),
        out_shape=[o[0] for o in outs],
        grid=(steps,),
        in_specs=cast_specs + in_specs,
        out_specs=[o[1] for o in outs],
        compiler_params=_cparams(("parallel",)),
        name="proj_lat" if latent else "proj_ctx",
    )(*cast, *args)


def _attn_kernel(sink_ref, q_ref, k_ref, v_ref, kc_ref, vc_ref, o_ref, *, n_lat):
    kvh = pl.program_id(1)
    n_win = 3 * BLOCK_Q
    heads = range(A_GROUP)
    kc = kc_ref[...]
    vc = vc_ref[...]
    sk = jnp.concatenate(
        [jnp.full((1, BLOCK_Q), sink_ref[kvh * A_GROUP + g], F32) for g in heads], axis=1)
    key_row = lax.broadcasted_iota(jnp.int32, (n_win, BLOCK_Q), 0)
    query_col = lax.broadcasted_iota(jnp.int32, (n_win, BLOCK_Q), 1)

    def scores(n):
        start = jnp.clip(n * BLOCK_Q - BLOCK_Q, 0, n_lat - n_win)
        start = pl.multiple_of(start, BLOCK_Q)
        rows = pl.ds(pl.multiple_of(n * BLOCK_Q, BLOCK_Q), BLOCK_Q)
        kw = k_ref[pl.ds(start, n_win), :]
        valid = jnp.abs((n * BLOCK_Q + query_col) - (start + key_row)) <= WINDOW
        q = jnp.concatenate([q_ref[rows, g * HEAD_DIM:(g + 1) * HEAD_DIM] for g in heads], axis=0)
        s_lat = _dot_nt(kw, q)
        s_lat = jnp.concatenate(
            [jnp.where(valid, s_lat[:, g * BLOCK_Q:(g + 1) * BLOCK_Q], NEG_INF) for g in heads], axis=1)
        s_ctx = _dot_nt(kc, q)
        return start, rows, s_lat, s_ctx

    def softmax(s_lat, s_ctx):
        m = jnp.maximum(jnp.maximum(jnp.max(s_lat, axis=0, keepdims=True),
                                    jnp.max(s_ctx, axis=0, keepdims=True)), sk)
        p_lat = jnp.exp(s_lat - m)
        p_ctx = jnp.exp(s_ctx - m)
        den = (jnp.sum(p_lat, axis=0, keepdims=True) + jnp.sum(p_ctx, axis=0, keepdims=True)
               + jnp.exp(sk - m))
        return p_lat.astype(BF16), p_ctx.astype(BF16), 1.0 / den

    def weighted_values(start, rows, p_lat, p_ctx, inv_den):
        vw = v_ref[pl.ds(start, n_win), :]
        o_t = (_dot_tn(vw, p_lat) + _dot_tn(vc, p_ctx)) * inv_den
        for g in heads:
            o_ref[rows, g * HEAD_DIM:(g + 1) * HEAD_DIM] = o_t[:, g * BLOCK_Q:(g + 1) * BLOCK_Q].T.astype(BF16)

    nb = n_lat // BLOCK_Q
    per_step = 4 if nb % 4 == 0 else 1

    def step(i, carry):
        sc = [scores(i * per_step + u) for u in range(per_step)]
        pr = [softmax(s_lat, s_ctx) for _, _, s_lat, s_ctx in sc]
        for (start, rows, _, _), (p_lat, p_ctx, inv_den) in zip(sc, pr):
            weighted_values(start, rows, p_lat, p_ctx, inv_den)
        return carry

    lax.fori_loop(0, nb // per_step, step, 0)


def _attn(sink, q, k, v, kc, vc, bsz, n_lat, n_ctx):
    gw = A_GROUP * HEAD_DIM
    blk = lambda rows, width: pl.BlockSpec((rows, width), lambda b, h: (b, h))
    return pl.pallas_call(
        functools.partial(_attn_kernel, n_lat=n_lat),
        out_shape=jax.ShapeDtypeStruct((bsz * n_lat, A_Q_W), BF16),
        grid=(bsz, A_KV_HEADS),
        in_specs=[pl.BlockSpec(memory_space=pltpu.SMEM), blk(n_lat, gw), blk(n_lat, HEAD_DIM),
                  blk(n_lat, HEAD_DIM), blk(n_ctx, HEAD_DIM), blk(n_ctx, HEAD_DIM)],
        out_specs=blk(n_lat, gw),
        compiler_params=_cparams(("parallel", "parallel")),
        name="attn",
    )(sink, q, k, v, kc, vc)


def _gla_kernel(q_ref, k_ref, v_ref, gf_ref, gb_ref, kc_ref, vc_ref, gfc_ref, gbc_ref, o_ref,
                sf_ref, sb_ref, ob_ref, *, n_lat, n_ctx):
    c = B_CHUNK

    def tri(nn, fn):
        r = lax.broadcasted_iota(jnp.int32, (nn, nn), 0)
        s = lax.broadcasted_iota(jnp.int32, (nn, nn), 1)
        return fn(r, s)

    kc = kc_ref[...]
    vcb = vc_ref[...].astype(BF16)
    up_strict = tri(n_ctx, lambda r, s: s > r).astype(BF16)
    lo_strict = tri(n_ctx, lambda r, s: s < r).astype(BF16)
    kd_f = (kc * jnp.exp(_tri_sum(up_strict, gfc_ref[...]))).astype(BF16)
    kd_b = (kc * jnp.exp(_tri_sum(lo_strict, gbc_ref[...]))).astype(BF16)
    sf_ref[...] = _dot_tn(vcb, kd_f)
    sb_ref[...] = _dot_tn(vcb, kd_b)

    r = c * GLA_GROUP
    ng = n_lat // r
    same = tri(r, lambda i, j: (i // c) == (j // c))
    lo_mask = jnp.logical_and(same, tri(r, lambda i, j: j <= i))
    up_mask = jnp.logical_and(same, tri(r, lambda i, j: j >= i))
    lo_incl = lo_mask.astype(BF16)
    up_incl = up_mask.astype(BF16)
    row_chunk = lax.broadcasted_iota(jnp.int32, (r, B_DK), 0) // c

    per_dir = 4 if ng % 4 == 0 else (2 if ng % 2 == 0 else 1)

    def decay_sums(rows, g_ref, tri_incl, last):
        b = _tri_sum(tri_incl, g_ref[rows, :])
        bl = jnp.concatenate(
            [jnp.broadcast_to(b[n * c + last:n * c + last + 1, :], (c, B_DK)) for n in range(GLA_GROUP)], axis=0)
        return b, bl

    def scaled_operands(rows, b, bl):
        q = q_ref[rows, :]
        k = k_ref[rows, :]
        qe = (q * jnp.exp(b)).astype(BF16)
        ke = (k * jnp.exp(-b)).astype(BF16)
        kd = k * jnp.exp(bl - b)
        kd_blocks = jnp.concatenate(
            [jnp.where(row_chunk == n, kd, 0.0).astype(BF16) for n in range(GLA_GROUP)], axis=1)
        return qe, ke, kd_blocks, jnp.exp(bl), v_ref[rows, :].astype(BF16)

    def body(i, carry):
        fwd = [pl.ds(pl.multiple_of((i * per_dir + u) * r, r), r) for u in range(per_dir)]
        bwd = [pl.ds(pl.multiple_of((ng - 1 - i * per_dir - u) * r, r), r) for u in range(per_dir)]
        streams = ([(rows, gf_ref, lo_incl, lo_mask, c - 1) for rows in fwd]
                   + [(rows, gb_ref, up_incl, up_mask, 0) for rows in bwd])
        sums = [decay_sums(rows, g_ref, tri_incl, last) for rows, g_ref, tri_incl, _, last in streams]
        ops = [scaled_operands(strm[0], b, bl) for strm, (b, bl) in zip(streams, sums)]
        a = [jnp.where(strm[3], _dot_nt(qe, ke), 0.0).astype(BF16)
             for strm, (qe, ke, _, _, _) in zip(streams, ops)]
        o_intra = [_dot(a_s, vb) for a_s, (_, _, _, _, vb) in zip(a, ops)]
        kv = [_dot_tn(vb, kd_blocks) for _, _, kd_blocks, _, vb in ops]
        o_inter = [[None] * GLA_GROUP for _ in streams]
        st_f, st_b = sf_ref[...], sb_ref[...]
        for u in range(per_dir):
            for step in range(GLA_GROUP):
                for s, n in ((u, step), (per_dir + u, GLA_GROUP - 1 - step)):
                    qe, _, _, dec, _ = ops[s]
                    st = st_f if s < per_dir else st_b
                    o_inter[s][n] = _dot_nt(qe[n * c:(n + 1) * c], st.astype(BF16))
                    st = st * dec[n * c:n * c + 1, :] + kv[s][:, n * B_DK:(n + 1) * B_DK]
                    if s < per_dir:
                        st_f = st
                    else:
                        st_b = st
        sf_ref[...] = st_f
        sb_ref[...] = st_b
        for s, (rows, *_) in enumerate(streams):
            out_ref = o_ref if s < per_dir else ob_ref
            out_ref[rows, :] = o_intra[s] + jnp.concatenate(o_inter[s], axis=0)
        return carry

    lax.fori_loop(0, ng // per_dir, body, 0)
    o_ref[...] += ob_ref[...]


def _gla(bq, bk, bv, gf, gb, bkc, bvc, gfc, gbc, bsz, n_lat, n_ctx):
    lat = lambda width: pl.BlockSpec((n_lat, width), lambda b, h: (b, h))
    cx = lambda width: pl.BlockSpec((n_ctx, width), lambda b, h: (b, h))
    return pl.pallas_call(
        functools.partial(_gla_kernel, n_lat=n_lat, n_ctx=n_ctx),
        out_shape=jax.ShapeDtypeStruct((bsz * n_lat, B_V_W), F32),
        grid=(bsz, B_HEADS),
        in_specs=[lat(B_DK), lat(B_DK), lat(B_DV), lat(B_DK), lat(B_DK),
                  cx(B_DK), cx(B_DV), cx(B_DK), cx(B_DK)],
        out_specs=lat(B_DV),
        scratch_shapes=[pltpu.VMEM((B_DV, B_DK), F32), pltpu.VMEM((B_DV, B_DK), F32),
                        pltpu.VMEM((n_lat, B_DV), F32)],
        compiler_params=_cparams(("parallel", "parallel")),
        name="gla",
    )(bq, bk, bv, gf, gb, bkc, bvc, gfc, gbc)


def _mixout_kernel(oa_ref, ob_ref, gate_ref, gn_ref, wa_ref, wb_ref, x_ref, g1_ref, o_ref):
    y = _dot(oa_ref[...], wa_ref[...])
    for hd in range(B_HEADS):
        cols = slice(hd * B_DV, (hd + 1) * B_DV)
        gt = gate_ref[:, cols]
        t = _rms(ob_ref[:, cols], gn_ref[...]) * (gt * jax.nn.sigmoid(gt))
        y = y + _dot(t.astype(BF16), wb_ref[cols, :])
    o_ref[...] = x_ref[...] + g1_ref[...] * y


def _mixout(oa, ob, gate, gla_norm, w_a, w_b, x2, mod3, mod_row_fn, tm):
    t_tok = x2.shape[0]
    tok = lambda width: pl.BlockSpec((tm, width), lambda i: (i, 0))
    const = lambda shape: pl.BlockSpec(shape, lambda i: (0, 0))
    return pl.pallas_call(
        _mixout_kernel,
        out_shape=jax.ShapeDtypeStruct((t_tok, D_MODEL), F32),
        grid=(t_tok // tm,),
        in_specs=[tok(A_Q_W), tok(B_V_W), tok(B_V_W), const((1, B_DV)),
                  const((A_Q_W, D_MODEL)), const((B_V_W, D_MODEL)), tok(D_MODEL),
                  pl.BlockSpec((None, 1, D_MODEL), lambda i: (mod_row_fn(i), 0, 2))],
        out_specs=tok(D_MODEL),
        compiler_params=_cparams(("parallel",)),
        name="mixout",
    )(oa, ob, gate, gla_norm, w_a, w_b, x2, mod3)


def _mlp_kernel(x_ref, gn_ref, sh_ref, sc_ref, g2_ref, w1_ref, w2_ref, o_ref, h_ref):
    j = pl.program_id(1)

    @pl.when(j == 0)
    def _():
        xt = x_ref[...]
        h = _rms(xt, gn_ref[...] * (1.0 + sc_ref[...])) + sh_ref[...]
        h_ref[...] = h.astype(BF16)
        o_ref[...] = xt

    u = jnp.maximum(_dot(h_ref[...], w1_ref[...]), 0.0)
    o_ref[...] += g2_ref[...] * _dot((u * u).astype(BF16), w2_ref[...])


def _mlp(x2, gn, mod3, mod_row_fn, w1, w2, layer, tm, tf):
    t_tok = x2.shape[0]
    mod = lambda col: pl.BlockSpec((None, 1, D_MODEL), lambda i, j: (mod_row_fn(i), 0, col))
    return pl.pallas_call(
        _mlp_kernel,
        out_shape=jax.ShapeDtypeStruct((t_tok, D_MODEL), F32),
        grid=(t_tok // tm, D_FF // tf),
        in_specs=[pl.BlockSpec((tm, D_MODEL), lambda i, j: (i, 0)),
                  pl.BlockSpec((1, D_MODEL), lambda i, j: (0, 0)),
                  mod(3), mod(4), mod(5),
                  pl.BlockSpec((None, D_MODEL, tf), lambda i, j: (layer, 0, j)),
                  pl.BlockSpec((None, tf, D_MODEL), lambda i, j: (layer, j, 0))],
        out_specs=pl.BlockSpec((tm, D_MODEL), lambda i, j: (i, 0)),
        scratch_shapes=[pltpu.VMEM((tm, D_MODEL), BF16)],
        compiler_params=_cparams(("parallel", "arbitrary")),
        name="mlp",
    )(x2, gn, mod3, mod3, mod3, w1, w2)


F_HALF = C_GROUPS * (C_GROUP_DIM // 2)
F_MID = LANE
F_COLS = 2 * F_HALF + F_MID


def _fch_kernel(x_ref, gn_ref, sh_ref, sc_ref, t_ref, mid_ref, yc_ref, ys_ref):
    h = _rms(x_ref[...], gn_ref[...] * (1.0 + sc_ref[...])) + sh_ref[...]
    hb = h.astype(BF16)
    half = C_GROUP_DIM // 2
    for g in range(C_GROUPS):
        y = _dot(hb[:, g * C_GROUP_DIM:(g + 1) * C_GROUP_DIM], t_ref[...])
        yc_ref[:, g * half:(g + 1) * half] = y[:, :half].astype(BF16)
        ys_ref[:, g * half:(g + 1) * half] = y[:, half:].astype(BF16)
    yc_ref[:, F_HALF:] = _dot(hb, mid_ref[...]).astype(BF16)


def _fch(x2, gn, mod3, mod_row_fn, t_ch, t_mid, tm):
    t_tok = x2.shape[0]
    mod = lambda col: pl.BlockSpec((None, 1, D_MODEL), lambda i: (mod_row_fn(i), 0, col))
    tok = lambda width: pl.BlockSpec((tm, width), lambda i: (i, 0))
    return pl.pallas_call(
        _fch_kernel,
        out_shape=[jax.ShapeDtypeStruct((t_tok, F_HALF + F_MID), BF16),
                   jax.ShapeDtypeStruct((t_tok, F_HALF), BF16)],
        grid=(t_tok // tm,),
        in_specs=[tok(D_MODEL), pl.BlockSpec((1, D_MODEL), lambda i: (0, 0)), mod(0), mod(1),
                  pl.BlockSpec((C_GROUP_DIM, C_GROUP_DIM), lambda i: (0, 0)),
                  pl.BlockSpec((D_MODEL, F_MID), lambda i: (0, 0))],
        out_specs=[tok(F_HALF + F_MID), tok(F_HALF)],
        compiler_params=_cparams(("parallel",)),
        name="fch",
    )(x2, gn, mod3, mod3, t_ch, t_mid)


def _fpos_kernel(cn_ref, sn_ref, yc_ref, ys_ref, o_ref, *, scale):
    p = _dot(cn_ref[...], yc_ref[...])
    q = _dot(sn_ref[...], ys_ref[...])
    o_ref[:, :F_HALF] = ((p[:, :F_HALF] - q) * scale).astype(BF16)
    o_ref[:, F_HALF:2 * F_HALF] = ((p[:, :F_HALF] + q) * scale).astype(BF16)
    o_ref[:, 2 * F_HALF:] = (p[:, F_HALF:] * scale).astype(BF16)


def _fpos(cn, sn, yc, ys, bsz, n_lat, tk):
    nk = n_lat // tk
    scale = float((n_lat * C_GROUP_DIM) ** -0.5)
    mat = pl.BlockSpec((tk, n_lat), lambda b, k: (k, 0))
    return pl.pallas_call(
        functools.partial(_fpos_kernel, scale=scale),
        out_shape=jax.ShapeDtypeStruct((bsz * n_lat, F_COLS), BF16),
        grid=(bsz, nk),
        in_specs=[mat, mat,
                  pl.BlockSpec((n_lat, F_HALF + F_MID), lambda b, k: (b, 0)),
                  pl.BlockSpec((n_lat, F_HALF), lambda b, k: (b, 0))],
        out_specs=pl.BlockSpec((tk, F_COLS), lambda b, k: (b * nk + k, 0)),
        compiler_params=_cparams(("parallel", "parallel")),
        name="fpos",
    )(cn, sn, yc, ys)


def _linres_kernel(z_ref, w_ref, b_ref, x_ref, g1_ref, o_ref):
    y = _dot(z_ref[...], w_ref[...]) + b_ref[...]
    o_ref[...] = x_ref[...] + g1_ref[...] * y


def _linres(z, w, bias, x2, mod3, mod_row_fn, tm):
    t_tok = x2.shape[0]
    tok = lambda width: pl.BlockSpec((tm, width), lambda i: (i, 0))
    return pl.pallas_call(
        _linres_kernel,
        out_shape=jax.ShapeDtypeStruct((t_tok, D_MODEL), F32),
        grid=(t_tok // tm,),
        in_specs=[tok(z.shape[1]), pl.BlockSpec((z.shape[1], D_MODEL), lambda i: (0, 0)),
                  pl.BlockSpec((1, D_MODEL), lambda i: (0, 0)), tok(D_MODEL),
                  pl.BlockSpec((None, 1, D_MODEL), lambda i: (mod_row_fn(i), 0, 2))],
        out_specs=tok(D_MODEL),
        compiler_params=_cparams(("parallel",)),
        name="linres",
    )(z, w, bias, x2, mod3)


def _rope_tables(n_lat):
    t = jnp.arange(n_lat)
    row = (t // GRID_W).astype(F32)
    col = (t % GRID_W).astype(F32)
    n_freq = HEAD_DIM // 4
    inv_freq = ROPE_BASE ** (-jnp.arange(n_freq, dtype=F32) / n_freq)
    ang_r, ang_c = row[:, None] * inv_freq, col[:, None] * inv_freq
    cr, sr, cc, sc = jnp.cos(ang_r), jnp.sin(ang_r), jnp.cos(ang_c), jnp.sin(ang_c)
    zero = jnp.zeros_like(sr)
    cos = jnp.concatenate([cr, cr, cc, cc], axis=-1)
    sin_a = jnp.concatenate([-sr, zero, -sc, zero], axis=-1)
    sin_b = jnp.concatenate([zero, sr, zero, sc], axis=-1)
    return cos, sin_a, sin_b


def _dft_tables(n_lat):
    def cs(n):
        idx = np.arange(n, dtype=np.int64)
        ang = 2.0 * np.pi * ((idx[:, None] * idx[None, :]) % n).astype(np.float64) / n
        return np.cos(ang), np.sin(ang)

    half = C_GROUP_DIM // 2
    cm, sm = cs(C_GROUP_DIM)
    cn, sn = cs(n_lat)
    t_ch = np.concatenate([cm[:, :half], sm[:, :half]], axis=1)
    t_mid = np.zeros((D_MODEL, F_MID))
    for g in range(C_GROUPS):
        t_mid[g * C_GROUP_DIM:(g + 1) * C_GROUP_DIM, g] = cm[:, half]
    m = np.arange(half)
    base = np.arange(C_GROUPS)[:, None] * C_GROUP_DIM
    lo_rows = (base + m[None, :]).reshape(-1)
    hi_rows = (base + (C_GROUP_DIM - m[None, :]) % C_GROUP_DIM).reshape(-1)
    mid_rows = np.concatenate([np.arange(C_GROUPS) * C_GROUP_DIM + half, np.zeros(F_MID - C_GROUPS, np.int64)])
    rows = np.concatenate([lo_rows, hi_rows, mid_rows]).astype(np.int32)
    keep = np.concatenate([np.ones(F_HALF), np.tile(m > 0, C_GROUPS), np.arange(F_MID) < C_GROUPS]).astype(np.float32)
    as_bf16 = lambda t: jnp.asarray(t, dtype=F32).astype(BF16)
    return as_bf16(t_ch), as_bf16(t_mid), as_bf16(cn), as_bf16(sn), rows, keep


def kernel(x, c, ctx, c_ctx, ada_w, ada_b, norm_mix, norm_mlp, mlp_w1, mlp_w2, ab_w_in, ab_q_norm, ab_k_norm,
           ab_sink, ab_gk_f, ab_gk_f_bias, ab_gk_b, ab_gk_b_bias, ab_gla_norm, ab_w_out, c_w_out, c_b_out):
    bsz, n_lat, _ = x.shape
    n_ctx = ctx.shape[1]
    depth = ada_w.shape[0]
    assert depth == 2 and bsz < MOD_ROWS
    t_tok = bsz * n_lat

    cvec = jnp.concatenate([c, c_ctx[None, :], jnp.zeros((MOD_ROWS - bsz - 1, D_MODEL), F32)], axis=0)
    mod = _ada(cvec, ada_w, ada_b)
    mod3 = mod.reshape(depth * MOD_ROWS, 1, 6 * D_MODEL)

    x2 = x.reshape(t_tok, D_MODEL)

    tm = min(256, n_lat)
    tpb = n_lat // tm
    lat_row0 = lambda i: i // tpb
    w_in = jnp.pad(ab_w_in[0], ((0, 0), (0, IN_COLS_PAD - IN_COLS))).astype(BF16)
    gn_mix0 = norm_mix[0].reshape(1, D_MODEL)
    qn = ab_q_norm[0].reshape(1, HEAD_DIM)
    kn = ab_k_norm[0].reshape(1, HEAD_DIM)
    gkf = jnp.pad(ab_gk_f[0], ((0, LANE - B_GATE_RANK), (0, 0))).astype(BF16)
    gkb = jnp.pad(ab_gk_b[0], ((B_GATE_RANK, LANE - 2 * B_GATE_RANK), (0, 0))).astype(BF16)
    gkfb = ab_gk_f_bias[0].reshape(1, B_QK_W)
    gkbb = ab_gk_b_bias[0].reshape(1, B_QK_W)
    steps = t_tok // tm
    mlp_rows = depth * D_MODEL * D_FF // steps
    ride_along = (depth * D_MODEL) % (steps * BF16_SUBLANES) == 0 and mlp_rows * 4 <= CAST_BLOCK_BYTES
    cast = (mlp_w1.reshape(depth * D_MODEL, D_FF), mlp_w2.reshape(depth * D_FF, D_MODEL)) if ride_along else ()
    q, k, v, bq, bk, bv, gate, gf, gb, *cast_out = _proj(
        x2, gn_mix0, mod3, lat_row0, tpb, tm, w_in, qn, kn, _rope_tables(n_lat), gkf, gkfb, gkb, gkbb, True, cast)
    if ride_along:
        w1 = cast_out[0].reshape(depth, D_MODEL, D_FF)
        w2 = cast_out[1].reshape(depth, D_FF, D_MODEL)
    else:
        w1, w2 = mlp_w1.astype(BF16), mlp_w2.astype(BF16)
    tmc = min(256, n_ctx)
    kc, vc, bkc, bvc, gfc, gbc = _proj(
        ctx.reshape(bsz * n_ctx, D_MODEL), gn_mix0, mod3, lambda i: bsz, n_ctx // tmc, tmc, w_in, None, kn, None,
        gkf, gkfb, gkb, gkbb, False)

    oa = _attn(ab_sink[0], q, k, v, kc, vc, bsz, n_lat, n_ctx)
    ob = _gla(bq, bk, bv, gf, gb, bkc, bvc, gfc, gbc, bsz, n_lat, n_ctx)

    w_out = ab_w_out[0].astype(BF16)
    tmo = min(512, n_lat)
    tpbo = n_lat // tmo
    x2 = _mixout(oa, ob, gate, ab_gla_norm[0].reshape(1, B_DV), w_out[:A_Q_W], w_out[A_Q_W:], x2, mod3,
                 lambda i: i // tpbo, tmo)
    x2 = _mlp(x2, norm_mlp[0].reshape(1, D_MODEL), mod3, lambda i: i // tpbo, w1, w2, 0, tmo, 1024)

    t_ch, t_mid, cn, sn, w_rows, w_keep = _dft_tables(n_lat)
    row1 = lambda i: MOD_ROWS + i // tpbo
    yc, ys = _fch(x2, norm_mix[1].reshape(1, D_MODEL), mod3, row1, t_ch, t_mid, tmo)
    z = _fpos(cn, sn, yc, ys, bsz, n_lat, min(512, n_lat))
    w_mix = jnp.where(w_keep[:, None] > 0, c_w_out[0][w_rows], 0.0).astype(BF16)
    x2 = _linres(z, w_mix, c_b_out[0].reshape(1, D_MODEL), x2, mod3, row1, tmo)
    x2 = _mlp(x2, norm_mlp[1].reshape(1, D_MODEL), mod3, row1, w1, w2, 1, tmo, 1024)
    return x2.reshape(bsz, n_lat, D_MODEL)
```

```python
import functools

import numpy as np
import jax
import jax.numpy as jnp
from jax import lax
from jax.experimental import pallas as pl
from jax.experimental.pallas import tpu as pltpu

D_MODEL = 2048
GRID_W = 64
EPS = 1e-6
NEG_INF = -1e30
HEAD_DIM = 128
A_Q_HEADS = 8
A_KV_HEADS = 2
A_GROUP = A_Q_HEADS // A_KV_HEADS
WINDOW = 128
BLOCK_Q = 128
LOG2_E = 1.4426950408889634
Q_SCALE = HEAD_DIM ** -0.5 * LOG2_E
ROPE_BASE = 10000.0
B_HEADS = 4
B_DV = 256
B_DK = 128
B_GATE_RANK = 16
B_GATE_NORM = 16.0
B_CHUNK = 64
GLA_GROUP = 4
C_GROUPS = 8
C_GROUP_DIM = D_MODEL // C_GROUPS
D_FF = 4 * D_MODEL

A_Q_W = A_Q_HEADS * HEAD_DIM
A_KV_W = A_KV_HEADS * HEAD_DIM
B_QK_W = B_HEADS * B_DK
B_V_W = B_HEADS * B_DV
OFF_AQ = 0
OFF_AK = OFF_AQ + A_Q_W
OFF_AV = OFF_AK + A_KV_W
OFF_BQ = OFF_AV + A_KV_W
OFF_BK = OFF_BQ + B_QK_W
OFF_BV = OFF_BK + B_QK_W
OFF_GATE = OFF_BV + B_V_W
OFF_LR = OFF_GATE + B_V_W
IN_COLS = OFF_LR + 2 * B_GATE_RANK
LANE = 128
IN_COLS_PAD = OFF_LR + LANE
BF16_SUBLANES = 16
MOD_ROWS = BF16_SUBLANES
VMEM_LIMIT = 56 * 1024 * 1024
CAST_BLOCK_BYTES = 2 * 1024 * 1024

BF16 = jnp.bfloat16
F32 = jnp.float32


def _cparams(sem):
    return pltpu.CompilerParams(dimension_semantics=sem, vmem_limit_bytes=VMEM_LIMIT)


def _dot(a, b):
    return jnp.dot(a, b, preferred_element_type=F32)


def _dot_nt(a, b):
    return lax.dot_general(a, b, (((1,), (1,)), ((), ())), preferred_element_type=F32)


def _dot_tn(a, b):
    return lax.dot_general(a, b, (((0,), (0,)), ((), ())), preferred_element_type=F32)


def _tri_sum(tri, g):
    g_hi = g.astype(BF16)
    g_lo = (g - g_hi.astype(F32)).astype(BF16)
    d = g.shape[1]
    both = _dot(tri, jnp.concatenate([g_hi, g_lo], axis=1))
    return both[:, :d] + both[:, d:]


def _rms(xf, gain):
    return xf * lax.rsqrt(jnp.mean(xf * xf, axis=-1, keepdims=True) + EPS) * gain


def _ada_kernel(c_ref, w_ref, b_ref, o_ref):
    cv = c_ref[...]
    s = (cv * jax.nn.sigmoid(cv)).astype(BF16)
    o_ref[...] = _dot(s, w_ref[...].astype(BF16)) + b_ref[...]


def _ada(cvec, ada_w, ada_b):
    depth = ada_w.shape[0]
    n = ada_w.shape[2]
    tn = 1024
    return pl.pallas_call(
        _ada_kernel,
        out_shape=jax.ShapeDtypeStruct((depth, MOD_ROWS, n), F32),
        grid=(depth, n // tn),
        in_specs=[
            pl.BlockSpec((MOD_ROWS, D_MODEL), lambda l, j: (0, 0)),
            pl.BlockSpec((None, D_MODEL, tn), lambda l, j: (l, 0, j)),
            pl.BlockSpec((None, 1, tn), lambda l, j: (l, 0, j)),
        ],
        out_specs=pl.BlockSpec((None, MOD_ROWS, tn), lambda l, j: (l, 0, j)),
        compiler_params=_cparams(("parallel", "parallel")),
        name="ada",
    )(cvec, ada_w, ada_b.reshape(depth, 1, n))


def _proj_kernel(*refs, latent, n_cast):
    for src_ref, dst_ref in zip(refs[:n_cast], refs[len(refs) - n_cast:]):
        dst_ref[...] = src_ref[...].astype(BF16)
    refs = refs[n_cast:len(refs) - n_cast]
    if latent:
        (x_ref, gn_ref, sh_ref, sc_ref, w_ref, qn_ref, kn_ref, cos_ref, sa_ref, sb_ref,
         gkf_ref, gkfb_ref, gkb_ref, gkbb_ref,
         q_ref, k_ref, v_ref, bq_ref, bk_ref, bv_ref, gate_ref, gf_ref, gb_ref) = refs
    else:
        (x_ref, gn_ref, sh_ref, sc_ref, w_ref, kn_ref,
         gkf_ref, gkfb_ref, gkb_ref, gkbb_ref,
         k_ref, v_ref, bk_ref, bv_ref, gf_ref, gb_ref) = refs

    xt = x_ref[...]
    h = _rms(xt, gn_ref[...]) * (1.0 + sc_ref[...]) + sh_ref[...]
    hb = h.astype(BF16)

    def proj(off, width):
        return _dot(hb, w_ref[:, off:off + width])

    def rope(t):
        return (t * cos_ref[...] + pltpu.roll(t, HEAD_DIM - HEAD_DIM // 4, 1) * sa_ref[...]
                + pltpu.roll(t, HEAD_DIM // 4, 1) * sb_ref[...])

    if latent:
        for half in range(2):
            seg = proj(OFF_AQ + half * (A_Q_W // 2), A_Q_W // 2)
            for j in range(A_Q_HEADS // 2):
                t = _rms(seg[:, j * HEAD_DIM:(j + 1) * HEAD_DIM], qn_ref[...])
                t = rope(t) * Q_SCALE
                hd = half * (A_Q_HEADS // 2) + j
                q_ref[:, hd * HEAD_DIM:(hd + 1) * HEAD_DIM] = t.astype(BF16)

    seg = proj(OFF_AK, 2 * A_KV_W)
    for j in range(A_KV_HEADS):
        t = _rms(seg[:, j * HEAD_DIM:(j + 1) * HEAD_DIM], kn_ref[...])
        if latent:
            t = rope(t)
        k_ref[:, j * HEAD_DIM:(j + 1) * HEAD_DIM] = t.astype(BF16)
    v_ref[...] = seg[:, A_KV_W:].astype(BF16)

    if latent:
        bq_ref[...] = proj(OFF_BQ, B_QK_W) * B_DK ** -0.5
    bk_ref[...] = proj(OFF_BK, B_QK_W)
    for half in range(2):
        bv_ref[:, half * (B_V_W // 2):(half + 1) * (B_V_W // 2)] = proj(OFF_BV + half * (B_V_W // 2), B_V_W // 2)
    if latent:
        for half in range(2):
            gate_ref[:, half * (B_V_W // 2):(half + 1) * (B_V_W // 2)] = proj(
                OFF_GATE + half * (B_V_W // 2), B_V_W // 2)

    lr = proj(OFF_LR, LANE).astype(BF16)

    def log_decay(gk_ref, gkb_ref, out_ref):
        z = _dot(lr, gk_ref[...]) + gkb_ref[...]
        out_ref[...] = (jnp.minimum(z, 0.0) - jnp.log1p(jnp.exp(-jnp.abs(z)))) / B_GATE_NORM

    log_decay(gkf_ref, gkfb_ref, gf_ref)
    log_decay(gkb_ref, gkbb_ref, gb_ref)


def _proj(x2, gn, mod3, mod_row_fn, tiles_per_batch, tm, w_in, qn, kn, rope_tabs, gkf, gkfb, gkb, gkbb, latent,
          cast=()):
    t_tok = x2.shape[0]
    steps = t_tok // tm
    row = lambda width: pl.BlockSpec((1, width), lambda i: (0, 0))
    tok = lambda width: pl.BlockSpec((tm, width), lambda i: (i, 0))
    mod = lambda col: pl.BlockSpec((None, 1, D_MODEL), lambda i: (mod_row_fn(i), 0, col))
    in_specs = [tok(D_MODEL), row(D_MODEL), mod(0), mod(1),
                pl.BlockSpec((D_MODEL, IN_COLS_PAD), lambda i: (0, 0), pipeline_mode=pl.Buffered(1))]
    args = [x2, gn, mod3, mod3, w_in]
    if latent:
        in_specs += [row(HEAD_DIM), row(HEAD_DIM)]
        args += [qn, kn]
        in_specs += [pl.BlockSpec((tm, HEAD_DIM), lambda i: (i % tiles_per_batch, 0))] * 3
        args += list(rope_tabs)
    else:
        in_specs += [row(HEAD_DIM)]
        args += [kn]
    in_specs += [pl.BlockSpec((LANE, B_QK_W), lambda i: (0, 0)), row(B_QK_W)] * 2
    args += [gkf, gkfb, gkb, gkbb]

    def out(width, dtype):
        return jax.ShapeDtypeStruct((t_tok, width), dtype), tok(width)

    if latent:
        outs = [out(A_Q_W, BF16), out(A_KV_W, BF16), out(A_KV_W, BF16), out(B_QK_W, F32), out(B_QK_W, F32),
                out(B_V_W, F32), out(B_V_W, F32), out(B_QK_W, F32), out(B_QK_W, F32)]
    else:
        outs = [out(A_KV_W, BF16), out(A_KV_W, BF16), out(B_QK_W, F32), out(B_V_W, F32),
                out(B_QK_W, F32), out(B_QK_W, F32)]
    cast_specs = [pl.BlockSpec((a.shape[0] // steps, a.shape[1]), lambda i: (i, 0)) for a in cast]
    outs += [(jax.ShapeDtypeStruct(a.shape, BF16), spec) for a, spec in zip(cast, cast_specs)]
    return pl.pallas_call(
        functools.partial(_proj_kernel, latent=latent, n_cast=len(cast)),
        out_shape=[o[0] for o in outs],
        grid=(steps,),
        in_specs=cast_specs + in_specs,
        out_specs=[o[1] for o in outs],
        compiler_params=_cparams(("parallel",)),
        name="proj_lat" if latent else "proj_ctx",
    )(*cast, *args)


def _attn_kernel(sink_ref, q_ref, k_ref, v_ref, kc_ref, vc_ref, o_ref, *, n_lat):
    kvh = pl.program_id(1)
    n_win = 3 * BLOCK_Q
    heads = range(A_GROUP)
    kc = kc_ref[...]
    vc = vc_ref[...]
    sk = jnp.concatenate(
        [jnp.full((1, BLOCK_Q), sink_ref[kvh * A_GROUP + g] * LOG2_E, F32) for g in heads], axis=1)
    key_minus_query = (lax.broadcasted_iota(jnp.int32, (n_win, BLOCK_Q), 0)
                       - lax.broadcasted_iota(jnp.int32, (n_win, BLOCK_Q), 1))

    def scores(n):
        start = jnp.clip(n * BLOCK_Q - BLOCK_Q, 0, n_lat - n_win)
        start = pl.multiple_of(start, BLOCK_Q)
        rows = pl.ds(pl.multiple_of(n * BLOCK_Q, BLOCK_Q), BLOCK_Q)
        kw = k_ref[pl.ds(start, n_win), :]
        valid = jnp.abs(key_minus_query + (start - n * BLOCK_Q)) <= WINDOW
        q = jnp.concatenate([q_ref[rows, g * HEAD_DIM:(g + 1) * HEAD_DIM] for g in heads], axis=0)
        s_lat = _dot_nt(kw, q)
        s_lat = jnp.concatenate(
            [jnp.where(valid, s_lat[:, g * BLOCK_Q:(g + 1) * BLOCK_Q], NEG_INF) for g in heads], axis=1)
        s_ctx = _dot_nt(kc, q)
        return start, rows, s_lat, s_ctx

    def softmax(s_lat, s_ctx):
        m = jnp.maximum(jnp.maximum(jnp.max(s_lat, axis=0, keepdims=True),
                                    jnp.max(s_ctx, axis=0, keepdims=True)), sk)
        p_lat = jnp.exp2(s_lat - m)
        p_ctx = jnp.exp2(s_ctx - m)
        den = (jnp.sum(p_lat, axis=0, keepdims=True) + jnp.sum(p_ctx, axis=0, keepdims=True)
               + jnp.exp2(sk - m))
        return p_lat.astype(BF16), p_ctx.astype(BF16), 1.0 / den

    def weighted_values(start, rows, p_lat, p_ctx, inv_den):
        vw = v_ref[pl.ds(start, n_win), :]
        o_t = (_dot_tn(vw, p_lat) + _dot_tn(vc, p_ctx)) * inv_den
        for g in heads:
            o_ref[rows, g * HEAD_DIM:(g + 1) * HEAD_DIM] = o_t[:, g * BLOCK_Q:(g + 1) * BLOCK_Q].T.astype(BF16)

    nb = n_lat // BLOCK_Q
    per_step = 4 if nb % 4 == 0 else 1

    def step(i, carry):
        sc = [scores(i * per_step + u) for u in range(per_step)]
        pr = [softmax(s_lat, s_ctx) for _, _, s_lat, s_ctx in sc]
        for (start, rows, _, _), (p_lat, p_ctx, inv_den) in zip(sc, pr):
            weighted_values(start, rows, p_lat, p_ctx, inv_den)
        return carry

    lax.fori_loop(0, nb // per_step, step, 0)


def _attn(sink, q, k, v, kc, vc, bsz, n_lat, n_ctx):
    gw = A_GROUP * HEAD_DIM
    blk = lambda rows, width: pl.BlockSpec((rows, width), lambda b, h: (b, h))
    return pl.pallas_call(
        functools.partial(_attn_kernel, n_lat=n_lat),
        out_shape=jax.ShapeDtypeStruct((bsz * n_lat, A_Q_W), BF16),
        grid=(bsz, A_KV_HEADS),
        in_specs=[pl.BlockSpec(memory_space=pltpu.SMEM), blk(n_lat, gw), blk(n_lat, HEAD_DIM),
                  blk(n_lat, HEAD_DIM), blk(n_ctx, HEAD_DIM), blk(n_ctx, HEAD_DIM)],
        out_specs=blk(n_lat, gw),
        compiler_params=_cparams(("parallel", "parallel")),
        name="attn",
    )(sink, q, k, v, kc, vc)


def _gla_kernel(q_ref, k_ref, v_ref, gf_ref, gb_ref, kc_ref, vc_ref, gfc_ref, gbc_ref, o_ref,
                sf_ref, sb_ref, ob_ref, *, n_lat, n_ctx):
    c = B_CHUNK

    def tri(nn, fn):
        r = lax.broadcasted_iota(jnp.int32, (nn, nn), 0)
        s = lax.broadcasted_iota(jnp.int32, (nn, nn), 1)
        return fn(r, s)

    kc = kc_ref[...]
    vcb = vc_ref[...].astype(BF16)
    up_strict = tri(n_ctx, lambda r, s: s > r).astype(BF16)
    lo_strict = tri(n_ctx, lambda r, s: s < r).astype(BF16)
    kd_f = (kc * jnp.exp(_tri_sum(up_strict, gfc_ref[...]))).astype(BF16)
    kd_b = (kc * jnp.exp(_tri_sum(lo_strict, gbc_ref[...]))).astype(BF16)
    sf_ref[...] = _dot_tn(vcb, kd_f)
    sb_ref[...] = _dot_tn(vcb, kd_b)

    r = c * GLA_GROUP
    ng = n_lat // r
    same = tri(r, lambda i, j: (i // c) == (j // c))
    lo_mask = jnp.logical_and(same, tri(r, lambda i, j: j <= i))
    up_mask = jnp.logical_and(same, tri(r, lambda i, j: j >= i))
    lo_incl = lo_mask.astype(BF16)
    up_incl = up_mask.astype(BF16)
    row_chunk = lax.broadcasted_iota(jnp.int32, (r, B_DK), 0) // c

    per_dir = 4 if ng % 4 == 0 else (2 if ng % 2 == 0 else 1)

    def decay_sums(rows, g_ref, tri_incl, last):
        b = _tri_sum(tri_incl, g_ref[rows, :])
        bl = jnp.concatenate(
            [jnp.broadcast_to(b[n * c + last:n * c + last + 1, :], (c, B_DK)) for n in range(GLA_GROUP)], axis=0)
        return b, bl

    def scaled_operands(rows, b, bl):
        q = q_ref[rows, :]
        k = k_ref[rows, :]
        qe = (q * jnp.exp(b)).astype(BF16)
        ke = (k * jnp.exp(-b)).astype(BF16)
        kd = k * jnp.exp(bl - b)
        kd_blocks = jnp.concatenate(
            [jnp.where(row_chunk == n, kd, 0.0).astype(BF16) for n in range(GLA_GROUP)], axis=1)
        return qe, ke, kd_blocks, jnp.exp(bl), v_ref[rows, :].astype(BF16)

    def body(i, carry):
        fwd = [pl.ds(pl.multiple_of((i * per_dir + u) * r, r), r) for u in range(per_dir)]
        bwd = [pl.ds(pl.multiple_of((ng - 1 - i * per_dir - u) * r, r), r) for u in range(per_dir)]
        streams = ([(rows, gf_ref, lo_incl, lo_mask, c - 1) for rows in fwd]
                   + [(rows, gb_ref, up_incl, up_mask, 0) for rows in bwd])
        sums = [decay_sums(rows, g_ref, tri_incl, last) for rows, g_ref, tri_incl, _, last in streams]
        ops = [scaled_operands(strm[0], b, bl) for strm, (b, bl) in zip(streams, sums)]
        a = [jnp.where(strm[3], _dot_nt(qe, ke), 0.0).astype(BF16)
             for strm, (qe, ke, _, _, _) in zip(streams, ops)]
        o_intra = [_dot(a_s, vb) for a_s, (_, _, _, _, vb) in zip(a, ops)]
        kv = [_dot_tn(vb, kd_blocks) for _, _, kd_blocks, _, vb in ops]
        o_inter = [[None] * GLA_GROUP for _ in streams]
        st_f, st_b = sf_ref[...], sb_ref[...]
        for u in range(per_dir):
            for step in range(GLA_GROUP):
                for s, n in ((u, step), (per_dir + u, GLA_GROUP - 1 - step)):
                    qe, _, _, dec, _ = ops[s]
                    st = st_f if s < per_dir else st_b
                    o_inter[s][n] = _dot_nt(qe[n * c:(n + 1) * c], st.astype(BF16))
                    st = st * dec[n * c:n * c + 1, :] + kv[s][:, n * B_DK:(n + 1) * B_DK]
                    if s < per_dir:
                        st_f = st
                    else:
                        st_b = st
        sf_ref[...] = st_f
        sb_ref[...] = st_b
        for s, (rows, *_) in enumerate(streams):
            out_ref = o_ref if s < per_dir else ob_ref
            out_ref[rows, :] = o_intra[s] + jnp.concatenate(o_inter[s], axis=0)
        return carry

    lax.fori_loop(0, ng // per_dir, body, 0)
    o_ref[...] += ob_ref[...]


def _gla(bq, bk, bv, gf, gb, bkc, bvc, gfc, gbc, bsz, n_lat, n_ctx):
    lat = lambda width: pl.BlockSpec((n_lat, width), lambda b, h: (b, h))
    cx = lambda width: pl.BlockSpec((n_ctx, width), lambda b, h: (b, h))
    return pl.pallas_call(
        functools.partial(_gla_kernel, n_lat=n_lat, n_ctx=n_ctx),
        out_shape=jax.ShapeDtypeStruct((bsz * n_lat, B_V_W), F32),
        grid=(bsz, B_HEADS),
        in_specs=[lat(B_DK), lat(B_DK), lat(B_DV), lat(B_DK), lat(B_DK),
                  cx(B_DK), cx(B_DV), cx(B_DK), cx(B_DK)],
        out_specs=lat(B_DV),
        scratch_shapes=[pltpu.VMEM((B_DV, B_DK), F32), pltpu.VMEM((B_DV, B_DK), F32),
                        pltpu.VMEM((n_lat, B_DV), F32)],
        compiler_params=_cparams(("parallel", "parallel")),
        name="gla",
    )(bq, bk, bv, gf, gb, bkc, bvc, gfc, gbc)


def _mixout_kernel(oa_ref, ob_ref, gate_ref, gn_ref, wa_ref, wb_ref, x_ref, g1_ref, o_ref):
    y = _dot(oa_ref[...], wa_ref[...])
    for hd in range(B_HEADS):
        cols = slice(hd * B_DV, (hd + 1) * B_DV)
        gt = gate_ref[:, cols]
        t = _rms(ob_ref[:, cols], gn_ref[...]) * (gt * jax.nn.sigmoid(gt))
        y = y + _dot(t.astype(BF16), wb_ref[cols, :])
    o_ref[...] = x_ref[...] + g1_ref[...] * y


def _mixout(oa, ob, gate, gla_norm, w_a, w_b, x2, mod3, mod_row_fn, tm):
    t_tok = x2.shape[0]
    tok = lambda width: pl.BlockSpec((tm, width), lambda i: (i, 0))
    const = lambda shape: pl.BlockSpec(shape, lambda i: (0, 0))
    return pl.pallas_call(
        _mixout_kernel,
        out_shape=jax.ShapeDtypeStruct((t_tok, D_MODEL), F32),
        grid=(t_tok // tm,),
        in_specs=[tok(A_Q_W), tok(B_V_W), tok(B_V_W), const((1, B_DV)),
                  const((A_Q_W, D_MODEL)), const((B_V_W, D_MODEL)), tok(D_MODEL),
                  pl.BlockSpec((None, 1, D_MODEL), lambda i: (mod_row_fn(i), 0, 2))],
        out_specs=tok(D_MODEL),
        compiler_params=_cparams(("parallel",)),
        name="mixout",
    )(oa, ob, gate, gla_norm, w_a, w_b, x2, mod3)


def _mlp_kernel(x_ref, gn_ref, sh_ref, sc_ref, g2_ref, w1_ref, w2_ref, o_ref, h_ref):
    j = pl.program_id(1)

    @pl.when(j == 0)
    def _():
        h = _rms(x_ref[...], gn_ref[...] * (1.0 + sc_ref[...])) + sh_ref[...]
        h_ref[...] = h.astype(BF16)

    def step(acc_ref):
        u = jnp.maximum(_dot(h_ref[...], w1_ref[...]), 0.0)
        o_ref[...] = acc_ref[...] + g2_ref[...] * _dot((u * u).astype(BF16), w2_ref[...])

    pl.when(j == 0)(functools.partial(step, x_ref))
    pl.when(j > 0)(functools.partial(step, o_ref))


def _mlp(x2, gn, mod3, mod_row_fn, w1, w2, layer, tm, tf):
    t_tok = x2.shape[0]
    mod = lambda col: pl.BlockSpec((None, 1, D_MODEL), lambda i, j: (mod_row_fn(i), 0, col))
    return pl.pallas_call(
        _mlp_kernel,
        out_shape=jax.ShapeDtypeStruct((t_tok, D_MODEL), F32),
        grid=(t_tok // tm, D_FF // tf),
        in_specs=[pl.BlockSpec((tm, D_MODEL), lambda i, j: (i, 0)),
                  pl.BlockSpec((1, D_MODEL), lambda i, j: (0, 0)),
                  mod(3), mod(4), mod(5),
                  pl.BlockSpec((None, D_MODEL, tf), lambda i, j: (layer, 0, j)),
                  pl.BlockSpec((None, tf, D_MODEL), lambda i, j: (layer, j, 0))],
        out_specs=pl.BlockSpec((tm, D_MODEL), lambda i, j: (i, 0)),
        scratch_shapes=[pltpu.VMEM((tm, D_MODEL), BF16)],
        compiler_params=_cparams(("parallel", "arbitrary")),
        name="mlp",
    )(x2, gn, mod3, mod3, mod3, w1, w2)


F_HALF = C_GROUPS * (C_GROUP_DIM // 2)
F_MID = LANE
F_COLS = 2 * F_HALF + F_MID


def _fch_kernel(x_ref, gn_ref, sh_ref, sc_ref, t_ref, mid_ref, yc_ref, ys_ref):
    h = _rms(x_ref[...], gn_ref[...] * (1.0 + sc_ref[...])) + sh_ref[...]
    hb = h.astype(BF16)
    half = C_GROUP_DIM // 2
    for g in range(C_GROUPS):
        y = _dot(hb[:, g * C_GROUP_DIM:(g + 1) * C_GROUP_DIM], t_ref[...])
        yc_ref[:, g * half:(g + 1) * half] = y[:, :half].astype(BF16)
        ys_ref[:, g * half:(g + 1) * half] = y[:, half:].astype(BF16)
    yc_ref[:, F_HALF:] = _dot(hb, mid_ref[...]).astype(BF16)


def _fch(x2, gn, mod3, mod_row_fn, t_ch, t_mid, tm):
    t_tok = x2.shape[0]
    mod = lambda col: pl.BlockSpec((None, 1, D_MODEL), lambda i: (mod_row_fn(i), 0, col))
    tok = lambda width: pl.BlockSpec((tm, width), lambda i: (i, 0))
    return pl.pallas_call(
        _fch_kernel,
        out_shape=[jax.ShapeDtypeStruct((t_tok, F_HALF + F_MID), BF16),
                   jax.ShapeDtypeStruct((t_tok, F_HALF), BF16)],
        grid=(t_tok // tm,),
        in_specs=[tok(D_MODEL), pl.BlockSpec((1, D_MODEL), lambda i: (0, 0)), mod(0), mod(1),
                  pl.BlockSpec((C_GROUP_DIM, C_GROUP_DIM), lambda i: (0, 0)),
                  pl.BlockSpec((D_MODEL, F_MID), lambda i: (0, 0))],
        out_specs=[tok(F_HALF + F_MID), tok(F_HALF)],
        compiler_params=_cparams(("parallel",)),
        name="fch",
    )(x2, gn, mod3, mod3, t_ch, t_mid)


def _fpos_kernel(cn_ref, sn_ref, yc_ref, ys_ref, o_ref, *, scale):
    p = _dot(cn_ref[...], yc_ref[...])
    q = _dot(sn_ref[...], ys_ref[...])
    o_ref[:, :F_HALF] = ((p[:, :F_HALF] - q) * scale).astype(BF16)
    o_ref[:, F_HALF:2 * F_HALF] = ((p[:, :F_HALF] + q) * scale).astype(BF16)
    o_ref[:, 2 * F_HALF:] = (p[:, F_HALF:] * scale).astype(BF16)


def _fpos(cn, sn, yc, ys, bsz, n_lat, tk):
    nk = n_lat // tk
    scale = float((n_lat * C_GROUP_DIM) ** -0.5)
    mat = pl.BlockSpec((tk, n_lat), lambda b, k: (k, 0))
    return pl.pallas_call(
        functools.partial(_fpos_kernel, scale=scale),
        out_shape=jax.ShapeDtypeStruct((bsz * n_lat, F_COLS), BF16),
        grid=(bsz, nk),
        in_specs=[mat, mat,
                  pl.BlockSpec((n_lat, F_HALF + F_MID), lambda b, k: (b, 0)),
                  pl.BlockSpec((n_lat, F_HALF), lambda b, k: (b, 0))],
        out_specs=pl.BlockSpec((tk, F_COLS), lambda b, k: (b * nk + k, 0)),
        compiler_params=_cparams(("parallel", "parallel")),
        name="fpos",
    )(cn, sn, yc, ys)


def _linres_kernel(z_ref, w_ref, b_ref, x_ref, g1_ref, o_ref):
    y = _dot(z_ref[...], w_ref[...]) + b_ref[...]
    o_ref[...] = x_ref[...] + g1_ref[...] * y


def _linres(z, w, bias, x2, mod3, mod_row_fn, tm):
    t_tok = x2.shape[0]
    tok = lambda width: pl.BlockSpec((tm, width), lambda i: (i, 0))
    return pl.pallas_call(
        _linres_kernel,
        out_shape=jax.ShapeDtypeStruct((t_tok, D_MODEL), F32),
        grid=(t_tok // tm,),
        in_specs=[tok(z.shape[1]), pl.BlockSpec((z.shape[1], D_MODEL), lambda i: (0, 0)),
                  pl.BlockSpec((1, D_MODEL), lambda i: (0, 0)), tok(D_MODEL),
                  pl.BlockSpec((None, 1, D_MODEL), lambda i: (mod_row_fn(i), 0, 2))],
        out_specs=tok(D_MODEL),
        compiler_params=_cparams(("parallel",)),
        name="linres",
    )(z, w, bias, x2, mod3)


def _rope_tables(n_lat):
    t = jnp.arange(n_lat)
    row = (t // GRID_W).astype(F32)
    col = (t % GRID_W).astype(F32)
    n_freq = HEAD_DIM // 4
    inv_freq = ROPE_BASE ** (-jnp.arange(n_freq, dtype=F32) / n_freq)
    ang_r, ang_c = row[:, None] * inv_freq, col[:, None] * inv_freq
    cr, sr, cc, sc = jnp.cos(ang_r), jnp.sin(ang_r), jnp.cos(ang_c), jnp.sin(ang_c)
    zero = jnp.zeros_like(sr)
    cos = jnp.concatenate([cr, cr, cc, cc], axis=-1)
    sin_a = jnp.concatenate([-sr, zero, -sc, zero], axis=-1)
    sin_b = jnp.concatenate([zero, sr, zero, sc], axis=-1)
    return cos, sin_a, sin_b


def _dft_tables(n_lat):
    def cs(n):
        idx = np.arange(n, dtype=np.int64)
        ang = 2.0 * np.pi * ((idx[:, None] * idx[None, :]) % n).astype(np.float64) / n
        return np.cos(ang), np.sin(ang)

    half = C_GROUP_DIM // 2
    cm, sm = cs(C_GROUP_DIM)
    cn, sn = cs(n_lat)
    t_ch = np.concatenate([cm[:, :half], sm[:, :half]], axis=1)
    t_mid = np.zeros((D_MODEL, F_MID))
    for g in range(C_GROUPS):
        t_mid[g * C_GROUP_DIM:(g + 1) * C_GROUP_DIM, g] = cm[:, half]
    m = np.arange(half)
    base = np.arange(C_GROUPS)[:, None] * C_GROUP_DIM
    lo_rows = (base + m[None, :]).reshape(-1)
    hi_rows = (base + (C_GROUP_DIM - m[None, :]) % C_GROUP_DIM).reshape(-1)
    mid_rows = np.concatenate([np.arange(C_GROUPS) * C_GROUP_DIM + half, np.zeros(F_MID - C_GROUPS, np.int64)])
    rows = np.concatenate([lo_rows, hi_rows, mid_rows]).astype(np.int32)
    keep = np.concatenate([np.ones(F_HALF), np.tile(m > 0, C_GROUPS), np.arange(F_MID) < C_GROUPS]).astype(np.float32)
    as_bf16 = lambda t: jnp.asarray(t, dtype=F32).astype(BF16)
    return as_bf16(t_ch), as_bf16(t_mid), as_bf16(cn), as_bf16(sn), rows, keep


def kernel(x, c, ctx, c_ctx, ada_w, ada_b, norm_mix, norm_mlp, mlp_w1, mlp_w2, ab_w_in, ab_q_norm, ab_k_norm,
           ab_sink, ab_gk_f, ab_gk_f_bias, ab_gk_b, ab_gk_b_bias, ab_gla_norm, ab_w_out, c_w_out, c_b_out):
    bsz, n_lat, _ = x.shape
    n_ctx = ctx.shape[1]
    depth = ada_w.shape[0]
    assert depth == 2 and bsz < MOD_ROWS
    t_tok = bsz * n_lat

    cvec = jnp.concatenate([c, c_ctx[None, :], jnp.zeros((MOD_ROWS - bsz - 1, D_MODEL), F32)], axis=0)
    mod = _ada(cvec, ada_w, ada_b)
    mod3 = mod.reshape(depth * MOD_ROWS, 1, 6 * D_MODEL)

    x2 = x.reshape(t_tok, D_MODEL)

    tm = min(256, n_lat)
    tpb = n_lat // tm
    lat_row0 = lambda i: i // tpb
    w_in = jnp.pad(ab_w_in[0], ((0, 0), (0, IN_COLS_PAD - IN_COLS))).astype(BF16)
    gn_mix0 = norm_mix[0].reshape(1, D_MODEL)
    qn = ab_q_norm[0].reshape(1, HEAD_DIM)
    kn = ab_k_norm[0].reshape(1, HEAD_DIM)
    gkf = jnp.pad(ab_gk_f[0], ((0, LANE - B_GATE_RANK), (0, 0))).astype(BF16)
    gkb = jnp.pad(ab_gk_b[0], ((B_GATE_RANK, LANE - 2 * B_GATE_RANK), (0, 0))).astype(BF16)
    gkfb = ab_gk_f_bias[0].reshape(1, B_QK_W)
    gkbb = ab_gk_b_bias[0].reshape(1, B_QK_W)
    steps = t_tok // tm
    mlp_rows = depth * D_MODEL * D_FF // steps
    ride_along = (depth * D_MODEL) % (steps * BF16_SUBLANES) == 0 and mlp_rows * 4 <= CAST_BLOCK_BYTES
    cast = (mlp_w1.reshape(depth * D_MODEL, D_FF), mlp_w2.reshape(depth * D_FF, D_MODEL)) if ride_along else ()
    q, k, v, bq, bk, bv, gate, gf, gb, *cast_out = _proj(
        x2, gn_mix0, mod3, lat_row0, tpb, tm, w_in, qn, kn, _rope_tables(n_lat), gkf, gkfb, gkb, gkbb, True, cast)
    if ride_along:
        w1 = cast_out[0].reshape(depth, D_MODEL, D_FF)
        w2 = cast_out[1].reshape(depth, D_FF, D_MODEL)
    else:
        w1, w2 = mlp_w1.astype(BF16), mlp_w2.astype(BF16)
    tmc = min(256, n_ctx)
    kc, vc, bkc, bvc, gfc, gbc = _proj(
        ctx.reshape(bsz * n_ctx, D_MODEL), gn_mix0, mod3, lambda i: bsz, n_ctx // tmc, tmc, w_in, None, kn, None,
        gkf, gkfb, gkb, gkbb, False)

    oa = _attn(ab_sink[0], q, k, v, kc, vc, bsz, n_lat, n_ctx)
    ob = _gla(bq, bk, bv, gf, gb, bkc, bvc, gfc, gbc, bsz, n_lat, n_ctx)

    w_out = ab_w_out[0].astype(BF16)
    tmo = min(512, n_lat)
    tpbo = n_lat // tmo
    x2 = _mixout(oa, ob, gate, ab_gla_norm[0].reshape(1, B_DV), w_out[:A_Q_W], w_out[A_Q_W:], x2, mod3,
                 lambda i: i // tpbo, tmo)
    x2 = _mlp(x2, norm_mlp[0].reshape(1, D_MODEL), mod3, lambda i: i // tpbo, w1, w2, 0, tmo, 1024)

    t_ch, t_mid, cn, sn, w_rows, w_keep = _dft_tables(n_lat)
    row1 = lambda i: MOD_ROWS + i // tpbo
    yc, ys = _fch(x2, norm_mix[1].reshape(1, D_MODEL), mod3, row1, t_ch, t_mid, tmo)
    z = _fpos(cn, sn, yc, ys, bsz, n_lat, min(512, n_lat))
    w_mix = jnp.where(w_keep[:, None] > 0, c_w_out[0][w_rows], 0.0).astype(BF16)
    x2 = _linres(z, w_mix, c_b_out[0].reshape(1, D_MODEL), x2, mod3, row1, tmo)
    x2 = _mlp(x2, norm_mlp[1].reshape(1, D_MODEL), mod3, row1, w1, w2, 1, tmo, 1024)
    return x2.reshape(bsz, n_lat, D_MODEL)
```

```python
import functools

import numpy as np
import jax
import jax.numpy as jnp
from jax import lax
from jax.experimental import pallas as pl
from jax.experimental.pallas import tpu as pltpu

D_MODEL = 2048
GRID_W = 64
EPS = 1e-6
NEG_INF = -1e30
HEAD_DIM = 128
A_Q_HEADS = 8
A_KV_HEADS = 2
A_GROUP = A_Q_HEADS // A_KV_HEADS
WINDOW = 128
BLOCK_Q = 128
LOG2_E = 1.4426950408889634
Q_SCALE = HEAD_DIM ** -0.5 * LOG2_E
ROPE_BASE = 10000.0
B_HEADS = 4
B_DV = 256
B_DK = 128
B_GATE_RANK = 16
B_GATE_NORM = 16.0
B_CHUNK = 64
GLA_GROUP = 4
C_GROUPS = 8
C_GROUP_DIM = D_MODEL // C_GROUPS
D_FF = 4 * D_MODEL

A_Q_W = A_Q_HEADS * HEAD_DIM
A_KV_W = A_KV_HEADS * HEAD_DIM
B_QK_W = B_HEADS * B_DK
B_V_W = B_HEADS * B_DV
OFF_AQ = 0
OFF_AK = OFF_AQ + A_Q_W
OFF_AV = OFF_AK + A_KV_W
OFF_BQ = OFF_AV + A_KV_W
OFF_BK = OFF_BQ + B_QK_W
OFF_BV = OFF_BK + B_QK_W
OFF_GATE = OFF_BV + B_V_W
OFF_LR = OFF_GATE + B_V_W
IN_COLS = OFF_LR + 2 * B_GATE_RANK
LANE = 128
IN_COLS_PAD = OFF_LR + LANE
BF16_SUBLANES = 16
MOD_ROWS = BF16_SUBLANES
VMEM_LIMIT = 56 * 1024 * 1024
CAST_BLOCK_BYTES = 2 * 1024 * 1024

BF16 = jnp.bfloat16
F32 = jnp.float32


def _cparams(sem):
    return pltpu.CompilerParams(dimension_semantics=sem, vmem_limit_bytes=VMEM_LIMIT)


def _dot(a, b):
    return jnp.dot(a, b, preferred_element_type=F32)


def _dot_nt(a, b):
    return lax.dot_general(a, b, (((1,), (1,)), ((), ())), preferred_element_type=F32)


def _dot_tn(a, b):
    return lax.dot_general(a, b, (((0,), (0,)), ((), ())), preferred_element_type=F32)


def _tri_sum(tri, g):
    g_hi = g.astype(BF16)
    g_lo = (g - g_hi.astype(F32)).astype(BF16)
    d = g.shape[1]
    both = _dot(tri, jnp.concatenate([g_hi, g_lo], axis=1))
    return both[:, :d] + both[:, d:]


def _rms(xf, gain):
    return xf * lax.rsqrt(jnp.mean(xf * xf, axis=-1, keepdims=True) + EPS) * gain


def _ada_kernel(c_ref, w_ref, b_ref, o_ref):
    cv = c_ref[...]
    s = (cv * jax.nn.sigmoid(cv)).astype(BF16)
    o_ref[...] = _dot(s, w_ref[...].astype(BF16)) + b_ref[...]


def _ada(cvec, ada_w, ada_b):
    depth = ada_w.shape[0]
    n = ada_w.shape[2]
    tn = 1024
    return pl.pallas_call(
        _ada_kernel,
        out_shape=jax.ShapeDtypeStruct((depth, MOD_ROWS, n), F32),
        grid=(depth, n // tn),
        in_specs=[
            pl.BlockSpec((MOD_ROWS, D_MODEL), lambda l, j: (0, 0)),
            pl.BlockSpec((None, D_MODEL, tn), lambda l, j: (l, 0, j)),
            pl.BlockSpec((None, 1, tn), lambda l, j: (l, 0, j)),
        ],
        out_specs=pl.BlockSpec((None, MOD_ROWS, tn), lambda l, j: (l, 0, j)),
        compiler_params=_cparams(("parallel", "parallel")),
        name="ada",
    )(cvec, ada_w, ada_b.reshape(depth, 1, n))


def _proj_kernel(*refs, latent, n_cast):
    for src_ref, dst_ref in zip(refs[:n_cast], refs[len(refs) - n_cast:]):
        dst_ref[...] = src_ref[...].astype(BF16)
    refs = refs[n_cast:len(refs) - n_cast]
    if latent:
        (x_ref, gn_ref, sh_ref, sc_ref, w_ref, qn_ref, kn_ref, cos_ref, sa_ref, sb_ref,
         gkf_ref, gkfb_ref, gkb_ref, gkbb_ref,
         q_ref, k_ref, v_ref, bq_ref, bk_ref, bv_ref, gate_ref, gf_ref, gb_ref) = refs
    else:
        (x_ref, gn_ref, sh_ref, sc_ref, w_ref, kn_ref,
         gkf_ref, gkfb_ref, gkb_ref, gkbb_ref,
         k_ref, v_ref, bk_ref, bv_ref, gf_ref, gb_ref) = refs

    xt = x_ref[...]
    h = _rms(xt, gn_ref[...]) * (1.0 + sc_ref[...]) + sh_ref[...]
    hb = h.astype(BF16)

    def proj(off, width):
        return _dot(hb, w_ref[:, off:off + width])

    def rope(t):
        return (t * cos_ref[...] + pltpu.roll(t, HEAD_DIM - HEAD_DIM // 4, 1) * sa_ref[...]
                + pltpu.roll(t, HEAD_DIM // 4, 1) * sb_ref[...])

    if latent:
        for half in range(2):
            seg = proj(OFF_AQ + half * (A_Q_W // 2), A_Q_W // 2)
            for j in range(A_Q_HEADS // 2):
                t = _rms(seg[:, j * HEAD_DIM:(j + 1) * HEAD_DIM], qn_ref[...])
                t = rope(t) * Q_SCALE
                hd = half * (A_Q_HEADS // 2) + j
                q_ref[:, hd * HEAD_DIM:(hd + 1) * HEAD_DIM] = t.astype(BF16)

    seg = proj(OFF_AK, 2 * A_KV_W)
    for j in range(A_KV_HEADS):
        t = _rms(seg[:, j * HEAD_DIM:(j + 1) * HEAD_DIM], kn_ref[...])
        if latent:
            t = rope(t)
        k_ref[:, j * HEAD_DIM:(j + 1) * HEAD_DIM] = t.astype(BF16)
    v_ref[...] = seg[:, A_KV_W:].astype(BF16)

    if latent:
        bq_ref[...] = proj(OFF_BQ, B_QK_W) * B_DK ** -0.5
    bk_ref[...] = proj(OFF_BK, B_QK_W)
    for half in range(2):
        bv_ref[:, half * (B_V_W // 2):(half + 1) * (B_V_W // 2)] = proj(OFF_BV + half * (B_V_W // 2), B_V_W // 2)
    if latent:
        for half in range(2):
            gate_ref[:, half * (B_V_W // 2):(half + 1) * (B_V_W // 2)] = proj(
                OFF_GATE + half * (B_V_W // 2), B_V_W // 2)

    lr = proj(OFF_LR, LANE).astype(BF16)

    def log_decay(gk_ref, gkb_ref, out_ref):
        z = _dot(lr, gk_ref[...]) + gkb_ref[...]
        out_ref[...] = (jnp.minimum(z, 0.0) - jnp.log1p(jnp.exp(-jnp.abs(z)))) / B_GATE_NORM

    log_decay(gkf_ref, gkfb_ref, gf_ref)
    log_decay(gkb_ref, gkbb_ref, gb_ref)


def _proj(x2, gn, mod3, mod_row_fn, tiles_per_batch, tm, w_in, qn, kn, rope_tabs, gkf, gkfb, gkb, gkbb, latent,
          cast=()):
    t_tok = x2.shape[0]
    steps = t_tok // tm
    row = lambda width: pl.BlockSpec((1, width), lambda i: (0, 0))
    tok = lambda width: pl.BlockSpec((tm, width), lambda i: (i, 0))
    mod = lambda col: pl.BlockSpec((None, 1, D_MODEL), lambda i: (mod_row_fn(i), 0, col))
    in_specs = [tok(D_MODEL), row(D_MODEL), mod(0), mod(1),
                pl.BlockSpec((D_MODEL, IN_COLS_PAD), lambda i: (0, 0), pipeline_mode=pl.Buffered(1))]
    args = [x2, gn, mod3, mod3, w_in]
    if latent:
        in_specs += [row(HEAD_DIM), row(HEAD_DIM)]
        args += [qn, kn]
        in_specs += [pl.BlockSpec((tm, HEAD_DIM), lambda i: (i % tiles_per_batch, 0))] * 3
        args += list(rope_tabs)
    else:
        in_specs += [row(HEAD_DIM)]
        args += [kn]
    in_specs += [pl.BlockSpec((LANE, B_QK_W), lambda i: (0, 0)), row(B_QK_W)] * 2
    args += [gkf, gkfb, gkb, gkbb]

    def out(width, dtype):
        return jax.ShapeDtypeStruct((t_tok, width), dtype), tok(width)

    if latent:
        outs = [out(A_Q_W, BF16), out(A_KV_W, BF16), out(A_KV_W, BF16), out(B_QK_W, F32), out(B_QK_W, F32),
                out(B_V_W, F32), out(B_V_W, F32), out(B_QK_W, F32), out(B_QK_W, F32)]
    else:
        outs = [out(A_KV_W, BF16), out(A_KV_W, BF16), out(B_QK_W, F32), out(B_V_W, F32),
                out(B_QK_W, F32), out(B_QK_W, F32)]
    cast_specs = [pl.BlockSpec((a.shape[0] // steps, a.shape[1]), lambda i: (i, 0)) for a in cast]
    outs += [(jax.ShapeDtypeStruct(a.shape, BF16), spec) for a, spec in zip(cast, cast_specs)]
    return pl.pallas_call(
        functools.partial(_proj_kernel, latent=latent, n_cast=len(cast)),
        out_shape=[o[0] for o in outs],
        grid=(steps,),
        in_specs=cast_specs + in_specs,
        out_specs=[o[1] for o in outs],
        compiler_params=_cparams(("parallel",)),
        name="proj_lat" if latent else "proj_ctx",
    )(*cast, *args)


def _attn_kernel(sink_ref, q_ref, k_ref, v_ref, kc_ref, vc_ref, o_ref, *, n_lat):
    kvh = pl.program_id(1)
    n_win = 3 * BLOCK_Q
    heads = range(A_GROUP)
    kc = kc_ref[...]
    vc = vc_ref[...]
    sk = jnp.concatenate(
        [jnp.full((1, BLOCK_Q), sink_ref[kvh * A_GROUP + g] * LOG2_E, F32) for g in heads], axis=1)
    key_minus_query = (lax.broadcasted_iota(jnp.int32, (n_win, BLOCK_Q), 0)
                       - lax.broadcasted_iota(jnp.int32, (n_win, BLOCK_Q), 1))

    def scores(n):
        start = jnp.clip(n * BLOCK_Q - BLOCK_Q, 0, n_lat - n_win)
        start = pl.multiple_of(start, BLOCK_Q)
        rows = pl.ds(pl.multiple_of(n * BLOCK_Q, BLOCK_Q), BLOCK_Q)
        kw = k_ref[pl.ds(start, n_win), :]
        valid = jnp.abs(key_minus_query + (start - n * BLOCK_Q)) <= WINDOW
        q = jnp.concatenate([q_ref[rows, g * HEAD_DIM:(g + 1) * HEAD_DIM] for g in heads], axis=0)
        s_lat = _dot_nt(kw, q)
        s_lat = jnp.concatenate(
            [jnp.where(valid, s_lat[:, g * BLOCK_Q:(g + 1) * BLOCK_Q], NEG_INF) for g in heads], axis=1)
        s_ctx = _dot_nt(kc, q)
        return start, rows, s_lat, s_ctx

    def softmax(s_lat, s_ctx):
        m = jnp.maximum(jnp.maximum(jnp.max(s_lat, axis=0, keepdims=True),
                                    jnp.max(s_ctx, axis=0, keepdims=True)), sk)
        p_lat = jnp.exp2(s_lat - m)
        p_ctx = jnp.exp2(s_ctx - m)
        den = (jnp.sum(p_lat, axis=0, keepdims=True) + jnp.sum(p_ctx, axis=0, keepdims=True)
               + jnp.exp2(sk - m))
        return p_lat.astype(BF16), p_ctx.astype(BF16), 1.0 / den

    def weighted_values(start, rows, p_lat, p_ctx, inv_den):
        vw = v_ref[pl.ds(start, n_win), :]
        o_t = (_dot_tn(vw, p_lat) + _dot_tn(vc, p_ctx)) * inv_den
        for g in heads:
            o_ref[rows, g * HEAD_DIM:(g + 1) * HEAD_DIM] = o_t[:, g * BLOCK_Q:(g + 1) * BLOCK_Q].T.astype(BF16)

    nb = n_lat // BLOCK_Q
    per_step = 4 if nb % 4 == 0 else 1

    def step(i, carry):
        sc = [scores(i * per_step + u) for u in range(per_step)]
        pr = [softmax(s_lat, s_ctx) for _, _, s_lat, s_ctx in sc]
        for (start, rows, _, _), (p_lat, p_ctx, inv_den) in zip(sc, pr):
            weighted_values(start, rows, p_lat, p_ctx, inv_den)
        return carry

    lax.fori_loop(0, nb // per_step, step, 0)


def _attn(sink, q, k, v, kc, vc, bsz, n_lat, n_ctx):
    gw = A_GROUP * HEAD_DIM
    blk = lambda rows, width: pl.BlockSpec((rows, width), lambda b, h: (b, h))
    return pl.pallas_call(
        functools.partial(_attn_kernel, n_lat=n_lat),
        out_shape=jax.ShapeDtypeStruct((bsz * n_lat, A_Q_W), BF16),
        grid=(bsz, A_KV_HEADS),
        in_specs=[pl.BlockSpec(memory_space=pltpu.SMEM), blk(n_lat, gw), blk(n_lat, HEAD_DIM),
                  blk(n_lat, HEAD_DIM), blk(n_ctx, HEAD_DIM), blk(n_ctx, HEAD_DIM)],
        out_specs=blk(n_lat, gw),
        compiler_params=_cparams(("parallel", "parallel")),
        name="attn",
    )(sink, q, k, v, kc, vc)


def _gla_kernel(q_ref, k_ref, v_ref, gf_ref, gb_ref, kc_ref, vc_ref, gfc_ref, gbc_ref, o_ref,
                sf_ref, sb_ref, ob_ref, *, n_lat, n_ctx):
    c = B_CHUNK

    def tri(nn, fn):
        r = lax.broadcasted_iota(jnp.int32, (nn, nn), 0)
        s = lax.broadcasted_iota(jnp.int32, (nn, nn), 1)
        return fn(r, s)

    kc = kc_ref[...]
    vcb = vc_ref[...].astype(BF16)
    up_strict = tri(n_ctx, lambda r, s: s > r).astype(BF16)
    lo_strict = tri(n_ctx, lambda r, s: s < r).astype(BF16)
    kd_f = (kc * jnp.exp(_tri_sum(up_strict, gfc_ref[...]))).astype(BF16)
    kd_b = (kc * jnp.exp(_tri_sum(lo_strict, gbc_ref[...]))).astype(BF16)
    sf_ref[...] = _dot_tn(vcb, kd_f)
    sb_ref[...] = _dot_tn(vcb, kd_b)

    r = c * GLA_GROUP
    ng = n_lat // r
    same = tri(r, lambda i, j: (i // c) == (j // c))
    lo_mask = jnp.logical_and(same, tri(r, lambda i, j: j <= i))
    up_mask = jnp.logical_and(same, tri(r, lambda i, j: j >= i))
    lo_incl = lo_mask.astype(BF16)
    up_incl = up_mask.astype(BF16)
    row_chunk = lax.broadcasted_iota(jnp.int32, (r, B_DK), 0) // c

    per_dir = 4 if ng % 4 == 0 else (2 if ng % 2 == 0 else 1)

    def decay_sums(rows, g_ref, tri_incl, last):
        b = _tri_sum(tri_incl, g_ref[rows, :])
        bl = jnp.concatenate(
            [jnp.broadcast_to(b[n * c + last:n * c + last + 1, :], (c, B_DK)) for n in range(GLA_GROUP)], axis=0)
        return b, bl

    def scaled_operands(rows, b, bl):
        q = q_ref[rows, :]
        k = k_ref[rows, :]
        qe = (q * jnp.exp(b)).astype(BF16)
        ke = (k * jnp.exp(-b)).astype(BF16)
        kd = k * jnp.exp(bl - b)
        kd_blocks = jnp.concatenate(
            [jnp.where(row_chunk == n, kd, 0.0).astype(BF16) for n in range(GLA_GROUP)], axis=1)
        return qe, ke, kd_blocks, jnp.exp(bl), v_ref[rows, :].astype(BF16)

    def body(i, carry):
        fwd = [pl.ds(pl.multiple_of((i * per_dir + u) * r, r), r) for u in range(per_dir)]
        bwd = [pl.ds(pl.multiple_of((ng - 1 - i * per_dir - u) * r, r), r) for u in range(per_dir)]
        streams = ([(rows, gf_ref, lo_incl, lo_mask, c - 1) for rows in fwd]
                   + [(rows, gb_ref, up_incl, up_mask, 0) for rows in bwd])
        sums = [decay_sums(rows, g_ref, tri_incl, last) for rows, g_ref, tri_incl, _, last in streams]
        ops = [scaled_operands(strm[0], b, bl) for strm, (b, bl) in zip(streams, sums)]
        a = [jnp.where(strm[3], _dot_nt(qe, ke), 0.0).astype(BF16)
             for strm, (qe, ke, _, _, _) in zip(streams, ops)]
        o_intra = [_dot(a_s, vb) for a_s, (_, _, _, _, vb) in zip(a, ops)]
        kv = [_dot_tn(vb, kd_blocks) for _, _, kd_blocks, _, vb in ops]
        o_inter = [[None] * GLA_GROUP for _ in streams]
        st_f, st_b = sf_ref[...], sb_ref[...]
        for u in range(per_dir):
            for step in range(GLA_GROUP):
                for s, n in ((u, step), (per_dir + u, GLA_GROUP - 1 - step)):
                    qe, _, _, dec, _ = ops[s]
                    st = st_f if s < per_dir else st_b
                    o_inter[s][n] = _dot_nt(qe[n * c:(n + 1) * c], st.astype(BF16))
                    st = st * dec[n * c:n * c + 1, :] + kv[s][:, n * B_DK:(n + 1) * B_DK]
                    if s < per_dir:
                        st_f = st
                    else:
                        st_b = st
        sf_ref[...] = st_f
        sb_ref[...] = st_b
        for s, (rows, *_) in enumerate(streams):
            out_ref = o_ref if s < per_dir else ob_ref
            out_ref[rows, :] = o_intra[s] + jnp.concatenate(o_inter[s], axis=0)
        return carry

    lax.fori_loop(0, ng // per_dir, body, 0)
    o_ref[...] += ob_ref[...]


def _gla(bq, bk, bv, gf, gb, bkc, bvc, gfc, gbc, bsz, n_lat, n_ctx):
    lat = lambda width: pl.BlockSpec((n_lat, width), lambda b, h: (b, h))
    cx = lambda width: pl.BlockSpec((n_ctx, width), lambda b, h: (b, h))
    return pl.pallas_call(
        functools.partial(_gla_kernel, n_lat=n_lat, n_ctx=n_ctx),
        out_shape=jax.ShapeDtypeStruct((bsz * n_lat, B_V_W), F32),
        grid=(bsz, B_HEADS),
        in_specs=[lat(B_DK), lat(B_DK), lat(B_DV), lat(B_DK), lat(B_DK),
                  cx(B_DK), cx(B_DV), cx(B_DK), cx(B_DK)],
        out_specs=lat(B_DV),
        scratch_shapes=[pltpu.VMEM((B_DV, B_DK), F32), pltpu.VMEM((B_DV, B_DK), F32),
                        pltpu.VMEM((n_lat, B_DV), F32)],
        compiler_params=_cparams(("parallel", "parallel")),
        name="gla",
    )(bq, bk, bv, gf, gb, bkc, bvc, gfc, gbc)


def _mixout_kernel(oa_ref, ob_ref, gate_ref, gn_ref, wa_ref, wb_ref, x_ref, g1_ref, o_ref):
    y = _dot(oa_ref[...], wa_ref[...])
    for hd in range(B_HEADS):
        cols = slice(hd * B_DV, (hd + 1) * B_DV)
        gt = gate_ref[:, cols]
        t = _rms(ob_ref[:, cols], gn_ref[...]) * (gt * jax.nn.sigmoid(gt))
        y = y + _dot(t.astype(BF16), wb_ref[cols, :])
    o_ref[...] = x_ref[...] + g1_ref[...] * y


def _mixout(oa, ob, gate, gla_norm, w_a, w_b, x2, mod3, mod_row_fn, tm):
    t_tok = x2.shape[0]
    tok = lambda width: pl.BlockSpec((tm, width), lambda i: (i, 0))
    const = lambda shape: pl.BlockSpec(shape, lambda i: (0, 0))
    return pl.pallas_call(
        _mixout_kernel,
        out_shape=jax.ShapeDtypeStruct((t_tok, D_MODEL), F32),
        grid=(t_tok // tm,),
        in_specs=[tok(A_Q_W), tok(B_V_W), tok(B_V_W), const((1, B_DV)),
                  const((A_Q_W, D_MODEL)), const((B_V_W, D_MODEL)), tok(D_MODEL),
                  pl.BlockSpec((None, 1, D_MODEL), lambda i: (mod_row_fn(i), 0, 2))],
        out_specs=tok(D_MODEL),
        compiler_params=_cparams(("parallel",)),
        name="mixout",
    )(oa, ob, gate, gla_norm, w_a, w_b, x2, mod3)


def _mlp_kernel(x_ref, gn_ref, sh_ref, sc_ref, g2_ref, w1_ref, w2_ref, o_ref, h_ref):
    j = pl.program_id(1)

    @pl.when(j == 0)
    def _():
        h = _rms(x_ref[...], gn_ref[...] * (1.0 + sc_ref[...])) + sh_ref[...]
        h_ref[...] = h.astype(BF16)

    def step(acc_ref):
        u = jnp.maximum(_dot(h_ref[...], w1_ref[...]), 0.0)
        o_ref[...] = acc_ref[...] + g2_ref[...] * _dot((u * u).astype(BF16), w2_ref[...])

    pl.when(j == 0)(functools.partial(step, x_ref))
    pl.when(j > 0)(functools.partial(step, o_ref))


def _mlp(x2, gn, mod3, mod_row_fn, w1, w2, layer, tm, tf):
    t_tok = x2.shape[0]
    mod = lambda col: pl.BlockSpec((None, 1, D_MODEL), lambda i, j: (mod_row_fn(i), 0, col))
    return pl.pallas_call(
        _mlp_kernel,
        out_shape=jax.ShapeDtypeStruct((t_tok, D_MODEL), F32),
        grid=(t_tok // tm, D_FF // tf),
        in_specs=[pl.BlockSpec((tm, D_MODEL), lambda i, j: (i, 0)),
                  pl.BlockSpec((1, D_MODEL), lambda i, j: (0, 0)),
                  mod(3), mod(4), mod(5),
                  pl.BlockSpec((None, D_MODEL, tf), lambda i, j: (layer, 0, j)),
                  pl.BlockSpec((None, tf, D_MODEL), lambda i, j: (layer, j, 0))],
        out_specs=pl.BlockSpec((tm, D_MODEL), lambda i, j: (i, 0)),
        scratch_shapes=[pltpu.VMEM((tm, D_MODEL), BF16)],
        compiler_params=_cparams(("parallel", "arbitrary")),
        name="mlp",
    )(x2, gn, mod3, mod3, mod3, w1, w2)


F_HALF = C_GROUPS * (C_GROUP_DIM // 2)
F_MID = LANE
F_COLS = 2 * F_HALF + F_MID


def _fch_kernel(x_ref, gn_ref, sh_ref, sc_ref, t_ref, mid_ref, yc_ref, ys_ref):
    h = _rms(x_ref[...], gn_ref[...] * (1.0 + sc_ref[...])) + sh_ref[...]
    hb = h.astype(BF16)
    half = C_GROUP_DIM // 2
    for g in range(C_GROUPS):
        y = _dot(hb[:, g * C_GROUP_DIM:(g + 1) * C_GROUP_DIM], t_ref[...])
        yc_ref[:, g * half:(g + 1) * half] = y[:, :half].astype(BF16)
        ys_ref[:, g * half:(g + 1) * half] = y[:, half:].astype(BF16)
    yc_ref[:, F_HALF:] = _dot(hb, mid_ref[...]).astype(BF16)


def _fch(x2, gn, mod3, mod_row_fn, t_ch, t_mid, tm):
    t_tok = x2.shape[0]
    mod = lambda col: pl.BlockSpec((None, 1, D_MODEL), lambda i: (mod_row_fn(i), 0, col))
    tok = lambda width: pl.BlockSpec((tm, width), lambda i: (i, 0))
    return pl.pallas_call(
        _fch_kernel,
        out_shape=[jax.ShapeDtypeStruct((t_tok, F_HALF + F_MID), BF16),
                   jax.ShapeDtypeStruct((t_tok, F_HALF), BF16)],
        grid=(t_tok // tm,),
        in_specs=[tok(D_MODEL), pl.BlockSpec((1, D_MODEL), lambda i: (0, 0)), mod(0), mod(1),
                  pl.BlockSpec((C_GROUP_DIM, C_GROUP_DIM), lambda i: (0, 0)),
                  pl.BlockSpec((D_MODEL, F_MID), lambda i: (0, 0))],
        out_specs=[tok(F_HALF + F_MID), tok(F_HALF)],
        compiler_params=_cparams(("parallel",)),
        name="fch",
    )(x2, gn, mod3, mod3, t_ch, t_mid)


F_SUB = 256


def _fpos_kernel(cn_ref, cnx_ref, sn_ref, snx_ref, yc_ref, ys_ref, o_ref, *, scale):
    yc = yc_ref[...]
    ys = ys_ref[...]

    def blocks(p, q):
        return (((p[:, :F_HALF] - q) * scale).astype(BF16), ((p[:, :F_HALF] + q) * scale).astype(BF16),
                (p[:, F_HALF:] * scale).astype(BF16))

    cur = blocks(_dot(cn_ref[...], yc), _dot(sn_ref[...], ys))
    nxt = blocks(_dot(cnx_ref[...], yc), _dot(snx_ref[...], ys))
    direct_cols = (slice(0, F_HALF), slice(F_HALF, 2 * F_HALF), slice(2 * F_HALF, F_COLS))
    mirror_cols = (direct_cols[1], direct_cols[0], direct_cols[2])
    for cols, blk in zip(direct_cols, cur):
        o_ref[0, :, cols] = blk
    rev = (lax.broadcasted_iota(jnp.int32, (F_SUB, F_SUB), 0)
           + lax.broadcasted_iota(jnp.int32, (F_SUB, F_SUB), 1) == F_SUB).astype(BF16)
    n_sub = cn_ref.shape[0] // F_SUB
    for u in range(n_sub):
        src = n_sub - 1 - u
        for cols, blk, blk_nxt in zip(mirror_cols, cur, nxt):
            after = blk_nxt[0:1] if src == n_sub - 1 else blk[(src + 1) * F_SUB:(src + 1) * F_SUB + 1]
            flipped = _dot(rev, blk[src * F_SUB:(src + 1) * F_SUB])
            first = lax.broadcasted_iota(jnp.int32, flipped.shape, 0) == 0
            o_ref[1, u * F_SUB:(u + 1) * F_SUB, cols] = jnp.where(first, after.astype(F32), flipped).astype(BF16)


def _fpos(cn, sn, yc, ys, bsz, n_lat, tile_rows):
    n_half = n_lat // 2
    nk = n_half // tile_rows
    scale = float((n_lat * C_GROUP_DIM) ** -0.5)
    tile = pl.BlockSpec((tile_rows, n_lat), lambda b, k: (k, 0))
    nxt = pl.BlockSpec((BF16_SUBLANES, n_lat), lambda b, k: ((k + 1) * (tile_rows // BF16_SUBLANES), 0))
    return pl.pallas_call(
        functools.partial(_fpos_kernel, scale=scale),
        out_shape=jax.ShapeDtypeStruct((bsz, 2, n_half, F_COLS), BF16),
        grid=(bsz, nk),
        in_specs=[tile, nxt, tile, nxt,
                  pl.BlockSpec((n_lat, F_HALF + F_MID), lambda b, k: (b, 0)),
                  pl.BlockSpec((n_lat, F_HALF), lambda b, k: (b, 0))],
        out_specs=pl.BlockSpec((None, 2, tile_rows, F_COLS), lambda b, k: (b, 0, k, 0)),
        compiler_params=_cparams(("parallel", "parallel")),
        name="fpos",
    )(cn, cn, sn, sn, yc, ys)


def _linres_kernel(*refs, n_z):
    z_refs, (w_ref, b_ref, x_ref, g1_ref, o_ref) = refs[:n_z], refs[n_z:]
    z = jnp.concatenate([z_ref[...] for z_ref in z_refs], axis=0)
    y = _dot(z, w_ref[...]) + b_ref[...]
    o_ref[...] = x_ref[...] + g1_ref[...] * y


def _linres(z, z_rows, w, bias, x2, mod3, mod_row_fn, tm, n_lat):
    t_tok = x2.shape[0]
    tiles_per_batch = n_lat // tm
    n_z = tm // z_rows
    nk = z.shape[2] // z_rows

    def z_spec(r):
        def index(i):
            pos_tile = (i % tiles_per_batch) * n_z + r
            half, k = pos_tile // nk, pos_tile % nk
            return i // tiles_per_batch, half, jnp.where(half == 0, k, nk - 1 - k), 0
        return pl.BlockSpec((None, None, z_rows, F_COLS), index)

    tok = lambda width: pl.BlockSpec((tm, width), lambda i: (i, 0))
    return pl.pallas_call(
        functools.partial(_linres_kernel, n_z=n_z),
        out_shape=jax.ShapeDtypeStruct((t_tok, D_MODEL), F32),
        grid=(t_tok // tm,),
        in_specs=[z_spec(r) for r in range(n_z)] + [
            pl.BlockSpec((F_COLS, D_MODEL), lambda i: (0, 0)),
            pl.BlockSpec((1, D_MODEL), lambda i: (0, 0)), tok(D_MODEL),
            pl.BlockSpec((None, 1, D_MODEL), lambda i: (mod_row_fn(i), 0, 2))],
        out_specs=tok(D_MODEL),
        compiler_params=_cparams(("parallel",)),
        name="linres",
    )(*([z] * n_z), w, bias, x2, mod3)


def _rope_tables(n_lat):
    t = jnp.arange(n_lat)
    row = (t // GRID_W).astype(F32)
    col = (t % GRID_W).astype(F32)
    n_freq = HEAD_DIM // 4
    inv_freq = ROPE_BASE ** (-jnp.arange(n_freq, dtype=F32) / n_freq)
    ang_r, ang_c = row[:, None] * inv_freq, col[:, None] * inv_freq
    cr, sr, cc, sc = jnp.cos(ang_r), jnp.sin(ang_r), jnp.cos(ang_c), jnp.sin(ang_c)
    zero = jnp.zeros_like(sr)
    cos = jnp.concatenate([cr, cr, cc, cc], axis=-1)
    sin_a = jnp.concatenate([-sr, zero, -sc, zero], axis=-1)
    sin_b = jnp.concatenate([zero, sr, zero, sc], axis=-1)
    return cos, sin_a, sin_b


def _dft_tables(n_lat):
    def cs(n):
        idx = np.arange(n, dtype=np.int64)
        ang = 2.0 * np.pi * ((idx[:, None] * idx[None, :]) % n).astype(np.float64) / n
        return np.cos(ang), np.sin(ang)

    half = C_GROUP_DIM // 2
    cm, sm = cs(C_GROUP_DIM)
    cn, sn = cs(n_lat)
    t_ch = np.concatenate([cm[:, :half], sm[:, :half]], axis=1)
    t_mid = np.zeros((D_MODEL, F_MID))
    for g in range(C_GROUPS):
        t_mid[g * C_GROUP_DIM:(g + 1) * C_GROUP_DIM, g] = cm[:, half]
    m = np.arange(half)
    base = np.arange(C_GROUPS)[:, None] * C_GROUP_DIM
    lo_rows = (base + m[None, :]).reshape(-1)
    hi_rows = (base + (C_GROUP_DIM - m[None, :]) % C_GROUP_DIM).reshape(-1)
    mid_rows = np.concatenate([np.arange(C_GROUPS) * C_GROUP_DIM + half, np.zeros(F_MID - C_GROUPS, np.int64)])
    rows = np.concatenate([lo_rows, hi_rows, mid_rows]).astype(np.int32)
    keep = np.concatenate([np.ones(F_HALF), np.tile(m > 0, C_GROUPS), np.arange(F_MID) < C_GROUPS]).astype(np.float32)
    as_bf16 = lambda t: jnp.asarray(t, dtype=F32).astype(BF16)
    return as_bf16(t_ch), as_bf16(t_mid), as_bf16(cn), as_bf16(sn), rows, keep


def kernel(x, c, ctx, c_ctx, ada_w, ada_b, norm_mix, norm_mlp, mlp_w1, mlp_w2, ab_w_in, ab_q_norm, ab_k_norm,
           ab_sink, ab_gk_f, ab_gk_f_bias, ab_gk_b, ab_gk_b_bias, ab_gla_norm, ab_w_out, c_w_out, c_b_out):
    bsz, n_lat, _ = x.shape
    n_ctx = ctx.shape[1]
    depth = ada_w.shape[0]
    assert depth == 2 and bsz < MOD_ROWS
    t_tok = bsz * n_lat

    cvec = jnp.concatenate([c, c_ctx[None, :], jnp.zeros((MOD_ROWS - bsz - 1, D_MODEL), F32)], axis=0)
    mod = _ada(cvec, ada_w, ada_b)
    mod3 = mod.reshape(depth * MOD_ROWS, 1, 6 * D_MODEL)

    x2 = x.reshape(t_tok, D_MODEL)

    tm = min(256, n_lat)
    tpb = n_lat // tm
    lat_row0 = lambda i: i // tpb
    w_in = jnp.pad(ab_w_in[0], ((0, 0), (0, IN_COLS_PAD - IN_COLS))).astype(BF16)
    gn_mix0 = norm_mix[0].reshape(1, D_MODEL)
    qn = ab_q_norm[0].reshape(1, HEAD_DIM)
    kn = ab_k_norm[0].reshape(1, HEAD_DIM)
    gkf = jnp.pad(ab_gk_f[0], ((0, LANE - B_GATE_RANK), (0, 0))).astype(BF16)
    gkb = jnp.pad(ab_gk_b[0], ((B_GATE_RANK, LANE - 2 * B_GATE_RANK), (0, 0))).astype(BF16)
    gkfb = ab_gk_f_bias[0].reshape(1, B_QK_W)
    gkbb = ab_gk_b_bias[0].reshape(1, B_QK_W)
    steps = t_tok // tm
    mlp_rows = depth * D_MODEL * D_FF // steps
    ride_along = (depth * D_MODEL) % (steps * BF16_SUBLANES) == 0 and mlp_rows * 4 <= CAST_BLOCK_BYTES
    cast = (mlp_w1.reshape(depth * D_MODEL, D_FF), mlp_w2.reshape(depth * D_FF, D_MODEL)) if ride_along else ()
    q, k, v, bq, bk, bv, gate, gf, gb, *cast_out = _proj(
        x2, gn_mix0, mod3, lat_row0, tpb, tm, w_in, qn, kn, _rope_tables(n_lat), gkf, gkfb, gkb, gkbb, True, cast)
    if ride_along:
        w1 = cast_out[0].reshape(depth, D_MODEL, D_FF)
        w2 = cast_out[1].reshape(depth, D_FF, D_MODEL)
    else:
        w1, w2 = mlp_w1.astype(BF16), mlp_w2.astype(BF16)
    tmc = min(256, n_ctx)
    kc, vc, bkc, bvc, gfc, gbc = _proj(
        ctx.reshape(bsz * n_ctx, D_MODEL), gn_mix0, mod3, lambda i: bsz, n_ctx // tmc, tmc, w_in, None, kn, None,
        gkf, gkfb, gkb, gkbb, False)

    oa = _attn(ab_sink[0], q, k, v, kc, vc, bsz, n_lat, n_ctx)
    ob = _gla(bq, bk, bv, gf, gb, bkc, bvc, gfc, gbc, bsz, n_lat, n_ctx)

    w_out = ab_w_out[0].astype(BF16)
    tmo = min(512, n_lat)
    tpbo = n_lat // tmo
    x2 = _mixout(oa, ob, gate, ab_gla_norm[0].reshape(1, B_DV), w_out[:A_Q_W], w_out[A_Q_W:], x2, mod3,
                 lambda i: i // tpbo, tmo)
    x2 = _mlp(x2, norm_mlp[0].reshape(1, D_MODEL), mod3, lambda i: i // tpbo, w1, w2, 0, tmo, 1024)

    t_ch, t_mid, cn, sn, w_rows, w_keep = _dft_tables(n_lat)
    row1 = lambda i: MOD_ROWS + i // tpbo
    yc, ys = _fch(x2, norm_mix[1].reshape(1, D_MODEL), mod3, row1, t_ch, t_mid, tmo)
    z_rows = min(tmo, n_lat // 2)
    z = _fpos(cn, sn, yc, ys, bsz, n_lat, z_rows)
    w_mix = jnp.where(w_keep[:, None] > 0, c_w_out[0][w_rows], 0.0).astype(BF16)
    x2 = _linres(z, z_rows, w_mix, c_b_out[0].reshape(1, D_MODEL), x2, mod3, row1, tmo, n_lat)
    x2 = _mlp(x2, norm_mlp[1].reshape(1, D_MODEL), mod3, row1, w1, w2, 1, tmo, 1024)
    return x2.reshape(bsz, n_lat, D_MODEL)
```

```python
import functools

import numpy as np
import jax
import jax.numpy as jnp
from jax import lax
from jax.experimental import pallas as pl
from jax.experimental.pallas import tpu as pltpu

D_MODEL = 2048
GRID_W = 64
EPS = 1e-6
NEG_INF = -1e30
HEAD_DIM = 128
A_Q_HEADS = 8
A_KV_HEADS = 2
A_GROUP = A_Q_HEADS // A_KV_HEADS
WINDOW = 128
BLOCK_Q = 128
LOG2_E = 1.4426950408889634
Q_SCALE = HEAD_DIM ** -0.5 * LOG2_E
ROPE_BASE = 10000.0
B_HEADS = 4
B_DV = 256
B_DK = 128
B_GATE_RANK = 16
B_GATE_NORM = 16.0
B_CHUNK = 64
GLA_GROUP = 4
C_GROUPS = 8
C_GROUP_DIM = D_MODEL // C_GROUPS
D_FF = 4 * D_MODEL

A_Q_W = A_Q_HEADS * HEAD_DIM
A_KV_W = A_KV_HEADS * HEAD_DIM
B_QK_W = B_HEADS * B_DK
B_V_W = B_HEADS * B_DV
OFF_AQ = 0
OFF_AK = OFF_AQ + A_Q_W
OFF_AV = OFF_AK + A_KV_W
OFF_BQ = OFF_AV + A_KV_W
OFF_BK = OFF_BQ + B_QK_W
OFF_BV = OFF_BK + B_QK_W
OFF_GATE = OFF_BV + B_V_W
OFF_LR = OFF_GATE + B_V_W
LR_COLS = 2 * B_GATE_RANK
IN_COLS = OFF_LR + LR_COLS
LANE = 128
BF16_SUBLANES = 16
MOD_ROWS = BF16_SUBLANES
VMEM_LIMIT = 56 * 1024 * 1024
CAST_BLOCK_BYTES = 2 * 1024 * 1024

BF16 = jnp.bfloat16
F32 = jnp.float32


def _cparams(sem):
    return pltpu.CompilerParams(dimension_semantics=sem, vmem_limit_bytes=VMEM_LIMIT)


def _dot(a, b):
    return jnp.dot(a, b, preferred_element_type=F32)


def _dot_nt(a, b):
    return lax.dot_general(a, b, (((1,), (1,)), ((), ())), preferred_element_type=F32)


def _dot_tn(a, b):
    return lax.dot_general(a, b, (((0,), (0,)), ((), ())), preferred_element_type=F32)


def _tri_sum(tri, g):
    g_hi = g.astype(BF16)
    g_lo = (g - g_hi.astype(F32)).astype(BF16)
    d = g.shape[1]
    both = _dot(tri, jnp.concatenate([g_hi, g_lo], axis=1))
    return both[:, :d] + both[:, d:]


def _rms(xf, gain):
    return xf * lax.rsqrt(jnp.mean(xf * xf, axis=-1, keepdims=True) + EPS) * gain


def _ada_kernel(c_ref, w_ref, b_ref, o_ref):
    cv = c_ref[...]
    s = (cv * jax.nn.sigmoid(cv)).astype(BF16)
    o_ref[...] = _dot(s, w_ref[...].astype(BF16)) + b_ref[...]


def _ada(cvec, ada_w, ada_b):
    depth = ada_w.shape[0]
    n = ada_w.shape[2]
    tn = 1024
    return pl.pallas_call(
        _ada_kernel,
        out_shape=jax.ShapeDtypeStruct((depth, MOD_ROWS, n), F32),
        grid=(depth, n // tn),
        in_specs=[
            pl.BlockSpec((MOD_ROWS, D_MODEL), lambda l, j: (0, 0)),
            pl.BlockSpec((None, D_MODEL, tn), lambda l, j: (l, 0, j)),
            pl.BlockSpec((None, 1, tn), lambda l, j: (l, 0, j)),
        ],
        out_specs=pl.BlockSpec((None, MOD_ROWS, tn), lambda l, j: (l, 0, j)),
        compiler_params=_cparams(("parallel", "parallel")),
        name="ada",
    )(cvec, ada_w, ada_b.reshape(depth, 1, n))


def _proj_kernel(*refs, latent, n_cast):
    for src_ref, dst_ref in zip(refs[:n_cast], refs[len(refs) - n_cast:]):
        dst_ref[...] = src_ref[...].astype(BF16)
    refs = refs[n_cast:len(refs) - n_cast]
    if latent:
        (x_ref, gn_ref, sh_ref, sc_ref, w_ref, qn_ref, kn_ref, cos_ref, sa_ref, sb_ref,
         gkf_ref, gkfb_ref, gkb_ref, gkbb_ref,
         q_ref, k_ref, v_ref, bq_ref, bk_ref, bv_ref, gate_ref, gf_ref, gb_ref) = refs
    else:
        (x_ref, gn_ref, sh_ref, sc_ref, w_ref, kn_ref,
         gkf_ref, gkfb_ref, gkb_ref, gkbb_ref,
         k_ref, v_ref, bk_ref, bv_ref, gf_ref, gb_ref) = refs

    xt = x_ref[...]
    h = _rms(xt, gn_ref[...]) * (1.0 + sc_ref[...]) + sh_ref[...]
    hb = h.astype(BF16)

    def proj(off, width):
        return _dot(hb, w_ref[:, off:off + width])

    def rope(t):
        return (t * cos_ref[...] + pltpu.roll(t, HEAD_DIM - HEAD_DIM // 4, 1) * sa_ref[...]
                + pltpu.roll(t, HEAD_DIM // 4, 1) * sb_ref[...])

    if latent:
        for half in range(2):
            seg = proj(OFF_AQ + half * (A_Q_W // 2), A_Q_W // 2)
            for j in range(A_Q_HEADS // 2):
                t = _rms(seg[:, j * HEAD_DIM:(j + 1) * HEAD_DIM], qn_ref[...])
                t = rope(t) * Q_SCALE
                hd = half * (A_Q_HEADS // 2) + j
                q_ref[:, hd * HEAD_DIM:(hd + 1) * HEAD_DIM] = t.astype(BF16)

    seg = proj(OFF_AK, 2 * A_KV_W)
    for j in range(A_KV_HEADS):
        t = _rms(seg[:, j * HEAD_DIM:(j + 1) * HEAD_DIM], kn_ref[...])
        if latent:
            t = rope(t)
        k_ref[:, j * HEAD_DIM:(j + 1) * HEAD_DIM] = t.astype(BF16)
    v_ref[...] = seg[:, A_KV_W:].astype(BF16)

    if latent:
        bq_ref[...] = proj(OFF_BQ, B_QK_W) * B_DK ** -0.5
    bk_ref[...] = proj(OFF_BK, B_QK_W)
    for half in range(2):
        bv_ref[:, half * (B_V_W // 2):(half + 1) * (B_V_W // 2)] = proj(OFF_BV + half * (B_V_W // 2), B_V_W // 2)
    if latent:
        for half in range(2):
            gate_ref[:, half * (B_V_W // 2):(half + 1) * (B_V_W // 2)] = proj(
                OFF_GATE + half * (B_V_W // 2), B_V_W // 2)

    lr = proj(OFF_LR, LR_COLS).astype(BF16)

    def log_decay(gk_ref, gkb_ref, out_ref):
        z = _dot(lr, gk_ref[...]) + gkb_ref[...]
        out_ref[...] = (jnp.minimum(z, 0.0) - jnp.log1p(jnp.exp(-jnp.abs(z)))) / B_GATE_NORM

    log_decay(gkf_ref, gkfb_ref, gf_ref)
    log_decay(gkb_ref, gkbb_ref, gb_ref)


def _proj(x2, gn, mod3, mod_row_fn, tiles_per_batch, tm, w_in, qn, kn, rope_tabs, gkf, gkfb, gkb, gkbb, latent,
          cast=()):
    t_tok = x2.shape[0]
    steps = t_tok // tm
    row = lambda width: pl.BlockSpec((1, width), lambda i: (0, 0))
    tok = lambda width: pl.BlockSpec((tm, width), lambda i: (i, 0))
    mod = lambda col: pl.BlockSpec((None, 1, D_MODEL), lambda i: (mod_row_fn(i), 0, col))
    in_specs = [tok(D_MODEL), row(D_MODEL), mod(0), mod(1),
                pl.BlockSpec((D_MODEL, IN_COLS), lambda i: (0, 0), pipeline_mode=pl.Buffered(1))]
    args = [x2, gn, mod3, mod3, w_in]
    if latent:
        in_specs += [row(HEAD_DIM), row(HEAD_DIM)]
        args += [qn, kn]
        in_specs += [pl.BlockSpec((tm, HEAD_DIM), lambda i: (i % tiles_per_batch, 0))] * 3
        args += list(rope_tabs)
    else:
        in_specs += [row(HEAD_DIM)]
        args += [kn]
    in_specs += [pl.BlockSpec((LR_COLS, B_QK_W), lambda i: (0, 0)), row(B_QK_W)] * 2
    args += [gkf, gkfb, gkb, gkbb]

    def out(width, dtype):
        return jax.ShapeDtypeStruct((t_tok, width), dtype), tok(width)

    if latent:
        outs = [out(A_Q_W, BF16), out(A_KV_W, BF16), out(A_KV_W, BF16), out(B_QK_W, F32), out(B_QK_W, F32),
                out(B_V_W, F32), out(B_V_W, F32), out(B_QK_W, F32), out(B_QK_W, F32)]
    else:
        outs = [out(A_KV_W, BF16), out(A_KV_W, BF16), out(B_QK_W, F32), out(B_V_W, F32),
                out(B_QK_W, F32), out(B_QK_W, F32)]
    cast_specs = [pl.BlockSpec((a.shape[0] // steps, a.shape[1]), lambda i: (i, 0)) for a in cast]
    outs += [(jax.ShapeDtypeStruct(a.shape, BF16), spec) for a, spec in zip(cast, cast_specs)]
    return pl.pallas_call(
        functools.partial(_proj_kernel, latent=latent, n_cast=len(cast)),
        out_shape=[o[0] for o in outs],
        grid=(steps,),
        in_specs=cast_specs + in_specs,
        out_specs=[o[1] for o in outs],
        compiler_params=_cparams(("parallel",)),
        name="proj_lat" if latent else "proj_ctx",
    )(*cast, *args)


def _attn_kernel(sink_ref, q_ref, k_ref, v_ref, kc_ref, vc_ref, o_ref, *, n_lat):
    kvh = pl.program_id(1)
    n_win = 3 * BLOCK_Q
    heads = range(A_GROUP)
    kc = kc_ref[...]
    vc = vc_ref[...]
    sk = jnp.concatenate(
        [jnp.full((1, BLOCK_Q), sink_ref[kvh * A_GROUP + g] * LOG2_E, F32) for g in heads], axis=1)
    key_minus_query = (lax.broadcasted_iota(jnp.int32, (n_win, BLOCK_Q), 0)
                       - lax.broadcasted_iota(jnp.int32, (n_win, BLOCK_Q), 1))

    def scores(n):
        start = jnp.clip(n * BLOCK_Q - BLOCK_Q, 0, n_lat - n_win)
        start = pl.multiple_of(start, BLOCK_Q)
        rows = pl.ds(pl.multiple_of(n * BLOCK_Q, BLOCK_Q), BLOCK_Q)
        kw = k_ref[pl.ds(start, n_win), :]
        valid = jnp.abs(key_minus_query + (start - n * BLOCK_Q)) <= WINDOW
        q = jnp.concatenate([q_ref[rows, g * HEAD_DIM:(g + 1) * HEAD_DIM] for g in heads], axis=0)
        s_lat = _dot_nt(kw, q)
        s_lat = jnp.concatenate(
            [jnp.where(valid, s_lat[:, g * BLOCK_Q:(g + 1) * BLOCK_Q], NEG_INF) for g in heads], axis=1)
        s_ctx = _dot_nt(kc, q)
        return start, rows, s_lat, s_ctx

    def softmax(s_lat, s_ctx):
        m = jnp.maximum(jnp.maximum(jnp.max(s_lat, axis=0, keepdims=True),
                                    jnp.max(s_ctx, axis=0, keepdims=True)), sk)
        p_lat = jnp.exp2(s_lat - m)
        p_ctx = jnp.exp2(s_ctx - m)
        den = (jnp.sum(p_lat, axis=0, keepdims=True) + jnp.sum(p_ctx, axis=0, keepdims=True)
               + jnp.exp2(sk - m))
        return p_lat.astype(BF16), p_ctx.astype(BF16), 1.0 / den

    def weighted_values(start, rows, p_lat, p_ctx, inv_den):
        vw = v_ref[pl.ds(start, n_win), :]
        o_t = (_dot_tn(vw, p_lat) + _dot_tn(vc, p_ctx)) * inv_den
        for g in heads:
            o_ref[rows, g * HEAD_DIM:(g + 1) * HEAD_DIM] = o_t[:, g * BLOCK_Q:(g + 1) * BLOCK_Q].T.astype(BF16)

    nb = n_lat // BLOCK_Q
    per_step = 4 if nb % 4 == 0 else 1

    def step(i, carry):
        sc = [scores(i * per_step + u) for u in range(per_step)]
        pr = [softmax(s_lat, s_ctx) for _, _, s_lat, s_ctx in sc]
        for (start, rows, _, _), (p_lat, p_ctx, inv_den) in zip(sc, pr):
            weighted_values(start, rows, p_lat, p_ctx, inv_den)
        return carry

    lax.fori_loop(0, nb // per_step, step, 0)


def _attn(sink, q, k, v, kc, vc, bsz, n_lat, n_ctx):
    gw = A_GROUP * HEAD_DIM
    blk = lambda rows, width: pl.BlockSpec((rows, width), lambda b, h: (b, h))
    return pl.pallas_call(
        functools.partial(_attn_kernel, n_lat=n_lat),
        out_shape=jax.ShapeDtypeStruct((bsz * n_lat, A_Q_W), BF16),
        grid=(bsz, A_KV_HEADS),
        in_specs=[pl.BlockSpec(memory_space=pltpu.SMEM), blk(n_lat, gw), blk(n_lat, HEAD_DIM),
                  blk(n_lat, HEAD_DIM), blk(n_ctx, HEAD_DIM), blk(n_ctx, HEAD_DIM)],
        out_specs=blk(n_lat, gw),
        compiler_params=_cparams(("parallel", "parallel")),
        name="attn",
    )(sink, q, k, v, kc, vc)


def _gla_kernel(q_ref, k_ref, v_ref, gf_ref, gb_ref, kc_ref, vc_ref, gfc_ref, gbc_ref, o_ref,
                sf_ref, sb_ref, ob_ref, *, n_lat, n_ctx):
    c = B_CHUNK

    def tri(nn, fn):
        r = lax.broadcasted_iota(jnp.int32, (nn, nn), 0)
        s = lax.broadcasted_iota(jnp.int32, (nn, nn), 1)
        return fn(r, s)

    kc = kc_ref[...]
    vcb = vc_ref[...].astype(BF16)
    up_strict = tri(n_ctx, lambda r, s: s > r).astype(BF16)
    lo_strict = tri(n_ctx, lambda r, s: s < r).astype(BF16)
    kd_f = (kc * jnp.exp(_tri_sum(up_strict, gfc_ref[...]))).astype(BF16)
    kd_b = (kc * jnp.exp(_tri_sum(lo_strict, gbc_ref[...]))).astype(BF16)
    sf_ref[...] = _dot_tn(vcb, kd_f)
    sb_ref[...] = _dot_tn(vcb, kd_b)

    r = c * GLA_GROUP
    ng = n_lat // r
    same = tri(r, lambda i, j: (i // c) == (j // c))
    lo_mask = jnp.logical_and(same, tri(r, lambda i, j: j <= i))
    up_mask = jnp.logical_and(same, tri(r, lambda i, j: j >= i))
    lo_incl = lo_mask.astype(BF16)
    up_incl = up_mask.astype(BF16)
    row_chunk = lax.broadcasted_iota(jnp.int32, (r, B_DK), 0) // c

    per_dir = 4 if ng % 4 == 0 else (2 if ng % 2 == 0 else 1)

    def decay_sums(rows, g_ref, tri_incl, last):
        b = _tri_sum(tri_incl, g_ref[rows, :])
        bl = jnp.concatenate(
            [jnp.broadcast_to(b[n * c + last:n * c + last + 1, :], (c, B_DK)) for n in range(GLA_GROUP)], axis=0)
        return b, bl

    def scaled_operands(rows, b, bl):
        q = q_ref[rows, :]
        k = k_ref[rows, :]
        qe = (q * jnp.exp(b)).astype(BF16)
        ke = (k * jnp.exp(-b)).astype(BF16)
        kd = k * jnp.exp(bl - b)
        kd_blocks = jnp.concatenate(
            [jnp.where(row_chunk == n, kd, 0.0).astype(BF16) for n in range(GLA_GROUP)], axis=1)
        return qe, ke, kd_blocks, jnp.exp(bl), v_ref[rows, :].astype(BF16)

    def body(i, carry):
        fwd = [pl.ds(pl.multiple_of((i * per_dir + u) * r, r), r) for u in range(per_dir)]
        bwd = [pl.ds(pl.multiple_of((ng - 1 - i * per_dir - u) * r, r), r) for u in range(per_dir)]
        streams = ([(rows, gf_ref, lo_incl, lo_mask, c - 1) for rows in fwd]
                   + [(rows, gb_ref, up_incl, up_mask, 0) for rows in bwd])
        sums = [decay_sums(rows, g_ref, tri_incl, last) for rows, g_ref, tri_incl, _, last in streams]
        ops = [scaled_operands(strm[0], b, bl) for strm, (b, bl) in zip(streams, sums)]
        a = [jnp.where(strm[3], _dot_nt(qe, ke), 0.0).astype(BF16)
             for strm, (qe, ke, _, _, _) in zip(streams, ops)]
        o_intra = [_dot(a_s, vb) for a_s, (_, _, _, _, vb) in zip(a, ops)]
        kv = [_dot_tn(vb, kd_blocks) for _, _, kd_blocks, _, vb in ops]
        o_inter = [[None] * GLA_GROUP for _ in streams]
        st_f, st_b = sf_ref[...], sb_ref[...]
        for u in range(per_dir):
            for step in range(GLA_GROUP):
                for s, n in ((u, step), (per_dir + u, GLA_GROUP - 1 - step)):
                    qe, _, _, dec, _ = ops[s]
                    st = st_f if s < per_dir else st_b
                    o_inter[s][n] = _dot_nt(qe[n * c:(n + 1) * c], st.astype(BF16))
                    st = st * dec[n * c:n * c + 1, :] + kv[s][:, n * B_DK:(n + 1) * B_DK]
                    if s < per_dir:
                        st_f = st
                    else:
                        st_b = st
        sf_ref[...] = st_f
        sb_ref[...] = st_b
        for s, (rows, *_) in enumerate(streams):
            out_ref = o_ref if s < per_dir else ob_ref
            out_ref[rows, :] = o_intra[s] + jnp.concatenate(o_inter[s], axis=0)
        return carry

    lax.fori_loop(0, ng // per_dir, body, 0)
    o_ref[...] += ob_ref[...]


def _gla(bq, bk, bv, gf, gb, bkc, bvc, gfc, gbc, bsz, n_lat, n_ctx):
    lat = lambda width: pl.BlockSpec((n_lat, width), lambda b, h: (b, h))
    cx = lambda width: pl.BlockSpec((n_ctx, width), lambda b, h: (b, h))
    return pl.pallas_call(
        functools.partial(_gla_kernel, n_lat=n_lat, n_ctx=n_ctx),
        out_shape=jax.ShapeDtypeStruct((bsz * n_lat, B_V_W), F32),
        grid=(bsz, B_HEADS),
        in_specs=[lat(B_DK), lat(B_DK), lat(B_DV), lat(B_DK), lat(B_DK),
                  cx(B_DK), cx(B_DV), cx(B_DK), cx(B_DK)],
        out_specs=lat(B_DV),
        scratch_shapes=[pltpu.VMEM((B_DV, B_DK), F32), pltpu.VMEM((B_DV, B_DK), F32),
                        pltpu.VMEM((n_lat, B_DV), F32)],
        compiler_params=_cparams(("parallel", "parallel")),
        name="gla",
    )(bq, bk, bv, gf, gb, bkc, bvc, gfc, gbc)


def _mixout_kernel(oa_ref, ob_ref, gate_ref, gn_ref, wa_ref, wb_ref, x_ref, g1_ref, o_ref):
    y = _dot(oa_ref[...], wa_ref[...])
    for hd in range(B_HEADS):
        cols = slice(hd * B_DV, (hd + 1) * B_DV)
        gt = gate_ref[:, cols]
        t = _rms(ob_ref[:, cols], gn_ref[...]) * (gt * jax.nn.sigmoid(gt))
        y = y + _dot(t.astype(BF16), wb_ref[cols, :])
    o_ref[...] = x_ref[...] + g1_ref[...] * y


def _mixout(oa, ob, gate, gla_norm, w_a, w_b, x2, mod3, mod_row_fn, tm):
    t_tok = x2.shape[0]
    tok = lambda width: pl.BlockSpec((tm, width), lambda i: (i, 0))
    const = lambda shape: pl.BlockSpec(shape, lambda i: (0, 0))
    return pl.pallas_call(
        _mixout_kernel,
        out_shape=jax.ShapeDtypeStruct((t_tok, D_MODEL), F32),
        grid=(t_tok // tm,),
        in_specs=[tok(A_Q_W), tok(B_V_W), tok(B_V_W), const((1, B_DV)),
                  const((A_Q_W, D_MODEL)), const((B_V_W, D_MODEL)), tok(D_MODEL),
                  pl.BlockSpec((None, 1, D_MODEL), lambda i: (mod_row_fn(i), 0, 2))],
        out_specs=tok(D_MODEL),
        compiler_params=_cparams(("parallel",)),
        name="mixout",
    )(oa, ob, gate, gla_norm, w_a, w_b, x2, mod3)


def _mlp_kernel(x_ref, gn_ref, sh_ref, sc_ref, g2_ref, w1_ref, w2_ref, o_ref, h_ref):
    j = pl.program_id(1)

    @pl.when(j == 0)
    def _():
        h = _rms(x_ref[...], gn_ref[...] * (1.0 + sc_ref[...])) + sh_ref[...]
        h_ref[...] = h.astype(BF16)

    def step(acc_ref):
        u = jnp.maximum(_dot(h_ref[...], w1_ref[...]), 0.0)
        o_ref[...] = acc_ref[...] + g2_ref[...] * _dot((u * u).astype(BF16), w2_ref[...])

    pl.when(j == 0)(functools.partial(step, x_ref))
    pl.when(j > 0)(functools.partial(step, o_ref))


def _mlp(x2, gn, mod3, mod_row_fn, w1, w2, layer, tm, tf):
    t_tok = x2.shape[0]
    mod = lambda col: pl.BlockSpec((None, 1, D_MODEL), lambda i, j: (mod_row_fn(i), 0, col))
    return pl.pallas_call(
        _mlp_kernel,
        out_shape=jax.ShapeDtypeStruct((t_tok, D_MODEL), F32),
        grid=(t_tok // tm, D_FF // tf),
        in_specs=[pl.BlockSpec((tm, D_MODEL), lambda i, j: (i, 0)),
                  pl.BlockSpec((1, D_MODEL), lambda i, j: (0, 0)),
                  mod(3), mod(4), mod(5),
                  pl.BlockSpec((None, D_MODEL, tf), lambda i, j: (layer, 0, j)),
                  pl.BlockSpec((None, tf, D_MODEL), lambda i, j: (layer, j, 0))],
        out_specs=pl.BlockSpec((tm, D_MODEL), lambda i, j: (i, 0)),
        scratch_shapes=[pltpu.VMEM((tm, D_MODEL), BF16)],
        compiler_params=_cparams(("parallel", "arbitrary")),
        name="mlp",
    )(x2, gn, mod3, mod3, mod3, w1, w2)


F_HALF = C_GROUPS * (C_GROUP_DIM // 2)
F_MID = LANE
F_COLS = 2 * F_HALF + F_MID


def _fch_kernel(x_ref, gn_ref, sh_ref, sc_ref, t_ref, mid_ref, yc_ref, ys_ref):
    h = _rms(x_ref[...], gn_ref[...] * (1.0 + sc_ref[...])) + sh_ref[...]
    hb = h.astype(BF16)
    half = C_GROUP_DIM // 2
    for g in range(C_GROUPS):
        y = _dot(hb[:, g * C_GROUP_DIM:(g + 1) * C_GROUP_DIM], t_ref[...])
        yc_ref[:, g * half:(g + 1) * half] = y[:, :half].astype(BF16)
        ys_ref[:, g * half:(g + 1) * half] = y[:, half:].astype(BF16)
    yc_ref[:, F_HALF:] = _dot(hb, mid_ref[...]).astype(BF16)


def _fch(x2, gn, mod3, mod_row_fn, t_ch, t_mid, tm):
    t_tok = x2.shape[0]
    mod = lambda col: pl.BlockSpec((None, 1, D_MODEL), lambda i: (mod_row_fn(i), 0, col))
    tok = lambda width: pl.BlockSpec((tm, width), lambda i: (i, 0))
    return pl.pallas_call(
        _fch_kernel,
        out_shape=[jax.ShapeDtypeStruct((t_tok, F_HALF + F_MID), BF16),
                   jax.ShapeDtypeStruct((t_tok, F_HALF), BF16)],
        grid=(t_tok // tm,),
        in_specs=[tok(D_MODEL), pl.BlockSpec((1, D_MODEL), lambda i: (0, 0)), mod(0), mod(1),
                  pl.BlockSpec((C_GROUP_DIM, C_GROUP_DIM), lambda i: (0, 0)),
                  pl.BlockSpec((D_MODEL, F_MID), lambda i: (0, 0))],
        out_specs=[tok(F_HALF + F_MID), tok(F_HALF)],
        compiler_params=_cparams(("parallel",)),
        name="fch",
    )(x2, gn, mod3, mod3, t_ch, t_mid)


F_SUB = 256


def _fpos_kernel(cn_ref, cnx_ref, sn_ref, snx_ref, yc_ref, ys_ref, o_ref, *, scale):
    yc = yc_ref[...]
    ys = ys_ref[...]

    def blocks(p, q):
        return (((p[:, :F_HALF] - q) * scale).astype(BF16), ((p[:, :F_HALF] + q) * scale).astype(BF16),
                (p[:, F_HALF:] * scale).astype(BF16))

    cur = blocks(_dot(cn_ref[...], yc), _dot(sn_ref[...], ys))
    nxt = blocks(_dot(cnx_ref[...], yc), _dot(snx_ref[...], ys))
    direct_cols = (slice(0, F_HALF), slice(F_HALF, 2 * F_HALF), slice(2 * F_HALF, F_COLS))
    mirror_cols = (direct_cols[1], direct_cols[0], direct_cols[2])
    for cols, blk in zip(direct_cols, cur):
        o_ref[0, :, cols] = blk
    rev = (lax.broadcasted_iota(jnp.int32, (F_SUB, F_SUB), 0)
           + lax.broadcasted_iota(jnp.int32, (F_SUB, F_SUB), 1) == F_SUB).astype(BF16)
    n_sub = cn_ref.shape[0] // F_SUB
    for u in range(n_sub):
        src = n_sub - 1 - u
        for cols, blk, blk_nxt in zip(mirror_cols, cur, nxt):
            after = blk_nxt[0:1] if src == n_sub - 1 else blk[(src + 1) * F_SUB:(src + 1) * F_SUB + 1]
            flipped = _dot(rev, blk[src * F_SUB:(src + 1) * F_SUB])
            first = lax.broadcasted_iota(jnp.int32, flipped.shape, 0) == 0
            o_ref[1, u * F_SUB:(u + 1) * F_SUB, cols] = jnp.where(first, after.astype(F32), flipped).astype(BF16)


def _fpos(cn, sn, yc, ys, bsz, n_lat, tile_rows):
    n_half = n_lat // 2
    nk = n_half // tile_rows
    scale = float((n_lat * C_GROUP_DIM) ** -0.5)
    tile = pl.BlockSpec((tile_rows, n_lat), lambda b, k: (k, 0))
    nxt = pl.BlockSpec((BF16_SUBLANES, n_lat), lambda b, k: ((k + 1) * (tile_rows // BF16_SUBLANES), 0))
    return pl.pallas_call(
        functools.partial(_fpos_kernel, scale=scale),
        out_shape=jax.ShapeDtypeStruct((bsz, 2, n_half, F_COLS), BF16),
        grid=(bsz, nk),
        in_specs=[tile, nxt, tile, nxt,
                  pl.BlockSpec((n_lat, F_HALF + F_MID), lambda b, k: (b, 0)),
                  pl.BlockSpec((n_lat, F_HALF), lambda b, k: (b, 0))],
        out_specs=pl.BlockSpec((None, 2, tile_rows, F_COLS), lambda b, k: (b, 0, k, 0)),
        compiler_params=_cparams(("parallel", "parallel")),
        name="fpos",
    )(cn, cn, sn, sn, yc, ys)


def _linres_kernel(*refs, n_z):
    z_refs, (w_ref, b_ref, x_ref, g1_ref, o_ref, wm_ref) = refs[:n_z], refs[n_z:]

    @pl.when(pl.program_id(0) == 0)
    def _():
        half = C_GROUP_DIM // 2
        rev = (lax.broadcasted_iota(jnp.int32, (half, half), 0)
               + lax.broadcasted_iota(jnp.int32, (half, half), 1) == half).astype(BF16)
        mid = []
        for g in range(C_GROUPS):
            base = g * C_GROUP_DIM
            wm_ref[g * half:(g + 1) * half, :] = w_ref[base:base + half, :].astype(BF16)
            upper = w_ref[base + half:base + C_GROUP_DIM, :].astype(BF16)
            wm_ref[F_HALF + g * half:F_HALF + (g + 1) * half, :] = _dot(rev, upper).astype(BF16)
            mid.append(w_ref[base + half:base + half + 1, :])
        mid.append(jnp.zeros((F_MID - C_GROUPS, D_MODEL), F32))
        wm_ref[2 * F_HALF:, :] = jnp.concatenate(mid, axis=0).astype(BF16)

    z = jnp.concatenate([z_ref[...] for z_ref in z_refs], axis=0)
    y = _dot(z, wm_ref[...]) + b_ref[...]
    o_ref[...] = x_ref[...] + g1_ref[...] * y


def _linres(z, z_rows, w, bias, x2, mod3, mod_row_fn, tm, n_lat):
    t_tok = x2.shape[0]
    tiles_per_batch = n_lat // tm
    n_z = tm // z_rows
    nk = z.shape[2] // z_rows

    def z_spec(r):
        def index(i):
            pos_tile = (i % tiles_per_batch) * n_z + r
            half, k = pos_tile // nk, pos_tile % nk
            return i // tiles_per_batch, half, jnp.where(half == 0, k, nk - 1 - k), 0
        return pl.BlockSpec((None, None, z_rows, F_COLS), index)

    tok = lambda width: pl.BlockSpec((tm, width), lambda i: (i, 0))
    return pl.pallas_call(
        functools.partial(_linres_kernel, n_z=n_z),
        out_shape=jax.ShapeDtypeStruct((t_tok, D_MODEL), F32),
        grid=(t_tok // tm,),
        in_specs=[z_spec(r) for r in range(n_z)] + [
            pl.BlockSpec((D_MODEL, D_MODEL), lambda i: (0, 0), pipeline_mode=pl.Buffered(1)),
            pl.BlockSpec((1, D_MODEL), lambda i: (0, 0)), tok(D_MODEL),
            pl.BlockSpec((None, 1, D_MODEL), lambda i: (mod_row_fn(i), 0, 2))],
        out_specs=tok(D_MODEL),
        scratch_shapes=[pltpu.VMEM((F_COLS, D_MODEL), BF16)],
        compiler_params=_cparams(("arbitrary",)),
        name="linres",
    )(*([z] * n_z), w, bias, x2, mod3)


def _rope_tables(n_lat):
    t = np.arange(n_lat)
    row = (t // GRID_W).astype(np.float32)
    col = (t % GRID_W).astype(np.float32)
    n_freq = HEAD_DIM // 4
    inv_freq = np.float32(ROPE_BASE) ** (-np.arange(n_freq, dtype=np.float32) / np.float32(n_freq))
    ang_r, ang_c = row[:, None] * inv_freq, col[:, None] * inv_freq
    cr, sr, cc, sc = np.cos(ang_r), np.sin(ang_r), np.cos(ang_c), np.sin(ang_c)
    zero = np.zeros_like(sr)
    cos = np.concatenate([cr, cr, cc, cc], axis=-1)
    sin_a = np.concatenate([-sr, zero, -sc, zero], axis=-1)
    sin_b = np.concatenate([zero, sr, zero, sc], axis=-1)
    return tuple(jnp.asarray(tab, dtype=F32) for tab in (cos, sin_a, sin_b))


def _dft_tables(n_lat):
    def cs(n):
        idx = np.arange(n, dtype=np.int64)
        ang = 2.0 * np.pi * ((idx[:, None] * idx[None, :]) % n).astype(np.float64) / n
        return np.cos(ang), np.sin(ang)

    half = C_GROUP_DIM // 2
    cm, sm = cs(C_GROUP_DIM)
    cn, sn = cs(n_lat)
    t_ch = np.concatenate([cm[:, :half], sm[:, :half]], axis=1)
    t_mid = np.zeros((D_MODEL, F_MID))
    for g in range(C_GROUPS):
        t_mid[g * C_GROUP_DIM:(g + 1) * C_GROUP_DIM, g] = cm[:, half]
    as_bf16 = lambda t: jnp.asarray(t, dtype=F32).astype(BF16)
    return as_bf16(t_ch), as_bf16(t_mid), as_bf16(cn), as_bf16(sn)


def kernel(x, c, ctx, c_ctx, ada_w, ada_b, norm_mix, norm_mlp, mlp_w1, mlp_w2, ab_w_in, ab_q_norm, ab_k_norm,
           ab_sink, ab_gk_f, ab_gk_f_bias, ab_gk_b, ab_gk_b_bias, ab_gla_norm, ab_w_out, c_w_out, c_b_out):
    bsz, n_lat, _ = x.shape
    n_ctx = ctx.shape[1]
    depth = ada_w.shape[0]
    assert depth == 2 and bsz < MOD_ROWS
    t_tok = bsz * n_lat

    cvec = jnp.concatenate([c, c_ctx[None, :], jnp.zeros((MOD_ROWS - bsz - 1, D_MODEL), F32)], axis=0)
    mod = _ada(cvec, ada_w, ada_b)
    mod3 = mod.reshape(depth * MOD_ROWS, 1, 6 * D_MODEL)

    x2 = x.reshape(t_tok, D_MODEL)

    tm = min(256, n_lat)
    tpb = n_lat // tm
    lat_row0 = lambda i: i // tpb
    w_in = ab_w_in[0].astype(BF16)
    gn_mix0 = norm_mix[0].reshape(1, D_MODEL)
    qn = ab_q_norm[0].reshape(1, HEAD_DIM)
    kn = ab_k_norm[0].reshape(1, HEAD_DIM)
    gkf = jnp.pad(ab_gk_f[0], ((0, LR_COLS - B_GATE_RANK), (0, 0))).astype(BF16)
    gkb = jnp.pad(ab_gk_b[0], ((LR_COLS - B_GATE_RANK, 0), (0, 0))).astype(BF16)
    gkfb = ab_gk_f_bias[0].reshape(1, B_QK_W)
    gkbb = ab_gk_b_bias[0].reshape(1, B_QK_W)
    steps = t_tok // tm
    mlp_rows = depth * D_MODEL * D_FF // steps
    ride_along = (depth * D_MODEL) % (steps * BF16_SUBLANES) == 0 and mlp_rows * 4 <= CAST_BLOCK_BYTES
    cast = (mlp_w1.reshape(depth * D_MODEL, D_FF), mlp_w2.reshape(depth * D_FF, D_MODEL)) if ride_along else ()
    q, k, v, bq, bk, bv, gate, gf, gb, *cast_out = _proj(
        x2, gn_mix0, mod3, lat_row0, tpb, tm, w_in, qn, kn, _rope_tables(n_lat), gkf, gkfb, gkb, gkbb, True, cast)
    if ride_along:
        w1 = cast_out[0].reshape(depth, D_MODEL, D_FF)
        w2 = cast_out[1].reshape(depth, D_FF, D_MODEL)
    else:
        w1, w2 = mlp_w1.astype(BF16), mlp_w2.astype(BF16)
    tmc = min(256, n_ctx)
    kc, vc, bkc, bvc, gfc, gbc = _proj(
        ctx.reshape(bsz * n_ctx, D_MODEL), gn_mix0, mod3, lambda i: bsz, n_ctx // tmc, tmc, w_in, None, kn, None,
        gkf, gkfb, gkb, gkbb, False)

    oa = _attn(ab_sink[0], q, k, v, kc, vc, bsz, n_lat, n_ctx)
    ob = _gla(bq, bk, bv, gf, gb, bkc, bvc, gfc, gbc, bsz, n_lat, n_ctx)

    w_out = ab_w_out[0].astype(BF16)
    tmo = min(512, n_lat)
    tpbo = n_lat // tmo
    x2 = _mixout(oa, ob, gate, ab_gla_norm[0].reshape(1, B_DV), w_out[:A_Q_W], w_out[A_Q_W:], x2, mod3,
                 lambda i: i // tpbo, tmo)
    x2 = _mlp(x2, norm_mlp[0].reshape(1, D_MODEL), mod3, lambda i: i // tpbo, w1, w2, 0, tmo, 1024)

    t_ch, t_mid, cn, sn = _dft_tables(n_lat)
    row1 = lambda i: MOD_ROWS + i // tpbo
    yc, ys = _fch(x2, norm_mix[1].reshape(1, D_MODEL), mod3, row1, t_ch, t_mid, tmo)
    z_rows = min(tmo, n_lat // 2)
    z = _fpos(cn, sn, yc, ys, bsz, n_lat, z_rows)
    x2 = _linres(z, z_rows, c_w_out[0], c_b_out[0].reshape(1, D_MODEL), x2, mod3, row1, tmo, n_lat)
    x2 = _mlp(x2, norm_mlp[1].reshape(1, D_MODEL), mod3, row1, w1, w2, 1, tmo, 1024)
    return x2.reshape(bsz, n_lat, D_MODEL)
```

```python
import functools

import numpy as np
import jax
import jax.numpy as jnp
from jax import lax
from jax.experimental import pallas as pl
from jax.experimental.pallas import tpu as pltpu

D_MODEL = 2048
GRID_W = 64
EPS = 1e-6
NEG_INF = -1e30
HEAD_DIM = 128
A_Q_HEADS = 8
A_KV_HEADS = 2
A_GROUP = A_Q_HEADS // A_KV_HEADS
WINDOW = 128
BLOCK_Q = 128
LOG2_E = 1.4426950408889634
Q_SCALE = HEAD_DIM ** -0.5 * LOG2_E
ROPE_BASE = 10000.0
B_HEADS = 4
B_DV = 256
B_DK = 128
B_GATE_RANK = 16
B_GATE_NORM = 16.0
B_CHUNK = 64
GLA_GROUP = 4
C_GROUPS = 8
C_GROUP_DIM = D_MODEL // C_GROUPS
D_FF = 4 * D_MODEL

A_Q_W = A_Q_HEADS * HEAD_DIM
A_KV_W = A_KV_HEADS * HEAD_DIM
B_QK_W = B_HEADS * B_DK
B_V_W = B_HEADS * B_DV
OFF_AQ = 0
OFF_AK = OFF_AQ + A_Q_W
OFF_AV = OFF_AK + A_KV_W
OFF_BQ = OFF_AV + A_KV_W
OFF_BK = OFF_BQ + B_QK_W
OFF_BV = OFF_BK + B_QK_W
OFF_GATE = OFF_BV + B_V_W
OFF_LR = OFF_GATE + B_V_W
LR_COLS = 2 * B_GATE_RANK
IN_COLS = OFF_LR + LR_COLS
LANE = 128
BF16_SUBLANES = 16
MOD_ROWS = BF16_SUBLANES
VMEM_LIMIT = 56 * 1024 * 1024
CAST_BLOCK_BYTES = 2 * 1024 * 1024
MLP_COL_BLOCK = 1024
BF16 = jnp.bfloat16
F32 = jnp.float32


def _cparams(sem):
    return pltpu.CompilerParams(dimension_semantics=sem, vmem_limit_bytes=VMEM_LIMIT)


def _dot(a, b):
    return jnp.dot(a, b, preferred_element_type=F32)


def _dot_nt(a, b):
    return lax.dot_general(a, b, (((1,), (1,)), ((), ())), preferred_element_type=F32)


def _dot_tn(a, b):
    return lax.dot_general(a, b, (((0,), (0,)), ((), ())), preferred_element_type=F32)


def _tri_sum(tri, g):
    g_hi = g.astype(BF16)
    g_lo = (g - g_hi.astype(F32)).astype(BF16)
    d = g.shape[1]
    both = _dot(tri, jnp.concatenate([g_hi, g_lo], axis=1))
    return both[:, :d] + both[:, d:]


def _rms(xf, gain):
    return xf * lax.rsqrt(jnp.mean(xf * xf, axis=-1, keepdims=True) + EPS) * gain


def _ada_kernel(c_ref, w_ref, b_ref, o_ref):
    cv = c_ref[...]
    s = (cv * jax.nn.sigmoid(cv)).astype(BF16)
    o_ref[...] = _dot(s, w_ref[...].astype(BF16)) + b_ref[...]


def _ada(cvec, ada_w, ada_b):
    depth = ada_w.shape[0]
    n = ada_w.shape[2]
    tn = 1024
    return pl.pallas_call(
        _ada_kernel,
        out_shape=jax.ShapeDtypeStruct((depth, MOD_ROWS, n), F32),
        grid=(depth, n // tn),
        in_specs=[
            pl.BlockSpec((MOD_ROWS, D_MODEL), lambda l, j: (0, 0)),
            pl.BlockSpec((None, D_MODEL, tn), lambda l, j: (l, 0, j)),
            pl.BlockSpec((None, 1, tn), lambda l, j: (l, 0, j)),
        ],
        out_specs=pl.BlockSpec((None, MOD_ROWS, tn), lambda l, j: (l, 0, j)),
        compiler_params=_cparams(("parallel", "parallel")),
        name="ada",
    )(cvec, ada_w, ada_b.reshape(depth, 1, n))


def _proj_kernel(*refs, latent, n_cast):
    for src_ref, dst_ref in zip(refs[:n_cast], refs[len(refs) - n_cast:]):
        dst_ref[...] = src_ref[...].astype(BF16)
    refs = refs[n_cast:len(refs) - n_cast]
    if latent:
        (x_ref, gn_ref, sh_ref, sc_ref, w_ref, qn_ref, kn_ref, cos_ref, sa_ref, sb_ref,
         gkf_ref, gkfb_ref, gkb_ref, gkbb_ref,
         q_ref, k_ref, v_ref, bq_ref, bk_ref, bv_ref, gate_ref, gf_ref, gb_ref) = refs
    else:
        (x_ref, gn_ref, sh_ref, sc_ref, w_ref, kn_ref,
         gkf_ref, gkfb_ref, gkb_ref, gkbb_ref,
         k_ref, v_ref, bk_ref, bv_ref, gf_ref, gb_ref) = refs

    xt = x_ref[...]
    h = _rms(xt, gn_ref[...]) * (1.0 + sc_ref[...]) + sh_ref[...]
    hb = h.astype(BF16)

    def proj(off, width):
        return _dot(hb, w_ref[:, off:off + width])

    def rope(t):
        return (t * cos_ref[...] + pltpu.roll(t, HEAD_DIM - HEAD_DIM // 4, 1) * sa_ref[...]
                + pltpu.roll(t, HEAD_DIM // 4, 1) * sb_ref[...])

    if latent:
        for half in range(2):
            seg = proj(OFF_AQ + half * (A_Q_W // 2), A_Q_W // 2)
            for j in range(A_Q_HEADS // 2):
                t = _rms(seg[:, j * HEAD_DIM:(j + 1) * HEAD_DIM], qn_ref[...])
                t = rope(t) * Q_SCALE
                hd = half * (A_Q_HEADS // 2) + j
                q_ref[:, hd * HEAD_DIM:(hd + 1) * HEAD_DIM] = t.astype(BF16)

    seg = proj(OFF_AK, 2 * A_KV_W)
    for j in range(A_KV_HEADS):
        t = _rms(seg[:, j * HEAD_DIM:(j + 1) * HEAD_DIM], kn_ref[...])
        if latent:
            t = rope(t)
        k_ref[:, j * HEAD_DIM:(j + 1) * HEAD_DIM] = t.astype(BF16)
    v_ref[...] = seg[:, A_KV_W:].astype(BF16)

    if latent:
        bq_ref[...] = proj(OFF_BQ, B_QK_W) * B_DK ** -0.5
    bk_ref[...] = proj(OFF_BK, B_QK_W)
    for half in range(2):
        bv_ref[:, half * (B_V_W // 2):(half + 1) * (B_V_W // 2)] = proj(OFF_BV + half * (B_V_W // 2), B_V_W // 2)
    if latent:
        for half in range(2):
            gate_ref[:, half * (B_V_W // 2):(half + 1) * (B_V_W // 2)] = proj(
                OFF_GATE + half * (B_V_W // 2), B_V_W // 2)

    lr = proj(OFF_LR, LR_COLS).astype(BF16)

    def log_decay(gk_ref, gkb_ref, out_ref):
        z = _dot(lr, gk_ref[...]) + gkb_ref[...]
        out_ref[...] = (jnp.minimum(z, 0.0) - jnp.log1p(jnp.exp(-jnp.abs(z)))) / B_GATE_NORM

    log_decay(gkf_ref, gkfb_ref, gf_ref)
    log_decay(gkb_ref, gkbb_ref, gb_ref)


def _proj(x2, gn, mod3, mod_row_fn, tiles_per_batch, tm, w_in, qn, kn, rope_tabs, gkf, gkfb, gkb, gkbb, latent,
          cast=()):
    t_tok = x2.shape[0]
    steps = t_tok // tm
    row = lambda width: pl.BlockSpec((1, width), lambda i: (0, 0))
    tok = lambda width: pl.BlockSpec((tm, width), lambda i: (i, 0))
    mod = lambda col: pl.BlockSpec((None, 1, D_MODEL), lambda i: (mod_row_fn(i), 0, col))
    in_specs = [tok(D_MODEL), row(D_MODEL), mod(0), mod(1),
                pl.BlockSpec((D_MODEL, IN_COLS), lambda i: (0, 0), pipeline_mode=pl.Buffered(1))]
    args = [x2, gn, mod3, mod3, w_in]
    if latent:
        in_specs += [row(HEAD_DIM), row(HEAD_DIM)]
        args += [qn, kn]
        in_specs += [pl.BlockSpec((tm, HEAD_DIM), lambda i: (i % tiles_per_batch, 0))] * 3
        args += list(rope_tabs)
    else:
        in_specs += [row(HEAD_DIM)]
        args += [kn]
    in_specs += [pl.BlockSpec((LR_COLS, B_QK_W), lambda i: (0, 0)), row(B_QK_W)] * 2
    args += [gkf, gkfb, gkb, gkbb]

    def out(width, dtype):
        return jax.ShapeDtypeStruct((t_tok, width), dtype), tok(width)

    if latent:
        outs = [out(A_Q_W, BF16), out(A_KV_W, BF16), out(A_KV_W, BF16), out(B_QK_W, F32), out(B_QK_W, F32),
                out(B_V_W, F32), out(B_V_W, F32), out(B_QK_W, F32), out(B_QK_W, F32)]
    else:
        outs = [out(A_KV_W, BF16), out(A_KV_W, BF16), out(B_QK_W, F32), out(B_V_W, F32),
                out(B_QK_W, F32), out(B_QK_W, F32)]
    cast_specs = [pl.BlockSpec((a.shape[0] // steps, a.shape[1]), lambda i: (i, 0)) for a in cast]
    outs += [(jax.ShapeDtypeStruct(a.shape, BF16), spec) for a, spec in zip(cast, cast_specs)]
    return pl.pallas_call(
        functools.partial(_proj_kernel, latent=latent, n_cast=len(cast)),
        out_shape=[o[0] for o in outs],
        grid=(steps,),
        in_specs=cast_specs + in_specs,
        out_specs=[o[1] for o in outs],
        compiler_params=_cparams(("parallel",)),
        name="proj_lat" if latent else "proj_ctx",
    )(*cast, *args)


def _attn_kernel(sink_ref, q_ref, k_ref, v_ref, kc_ref, vc_ref, o_ref, *, n_lat):
    kvh = pl.program_id(1)
    n_win = 3 * BLOCK_Q
    heads = range(A_GROUP)
    kc = kc_ref[...]
    vc = vc_ref[...]
    sk = jnp.concatenate(
        [jnp.full((1, BLOCK_Q), sink_ref[kvh * A_GROUP + g] * LOG2_E, F32) for g in heads], axis=1)
    key_minus_query = (lax.broadcasted_iota(jnp.int32, (n_win, BLOCK_Q), 0)
                       - lax.broadcasted_iota(jnp.int32, (n_win, BLOCK_Q), 1))

    def scores(n):
        start = jnp.clip(n * BLOCK_Q - BLOCK_Q, 0, n_lat - n_win)
        start = pl.multiple_of(start, BLOCK_Q)
        rows = pl.ds(pl.multiple_of(n * BLOCK_Q, BLOCK_Q), BLOCK_Q)
        kw = k_ref[pl.ds(start, n_win), :]
        valid = jnp.abs(key_minus_query + (start - n * BLOCK_Q)) <= WINDOW
        q = jnp.concatenate([q_ref[rows, g * HEAD_DIM:(g + 1) * HEAD_DIM] for g in heads], axis=0)
        s_lat = _dot_nt(kw, q)
        s_lat = jnp.concatenate(
            [jnp.where(valid, s_lat[:, g * BLOCK_Q:(g + 1) * BLOCK_Q], NEG_INF) for g in heads], axis=1)
        s_ctx = _dot_nt(kc, q)
        return start, rows, s_lat, s_ctx

    def softmax(s_lat, s_ctx):
        m = jnp.maximum(jnp.maximum(jnp.max(s_lat, axis=0, keepdims=True),
                                    jnp.max(s_ctx, axis=0, keepdims=True)), sk)
        p_lat = jnp.exp2(s_lat - m)
        p_ctx = jnp.exp2(s_ctx - m)
        den = (jnp.sum(p_lat, axis=0, keepdims=True) + jnp.sum(p_ctx, axis=0, keepdims=True)
               + jnp.exp2(sk - m))
        return p_lat.astype(BF16), p_ctx.astype(BF16), 1.0 / den

    def weighted_values(start, rows, p_lat, p_ctx, inv_den):
        vw = v_ref[pl.ds(start, n_win), :]
        o_t = (_dot_tn(vw, p_lat) + _dot_tn(vc, p_ctx)) * inv_den
        for g in heads:
            o_ref[rows, g * HEAD_DIM:(g + 1) * HEAD_DIM] = o_t[:, g * BLOCK_Q:(g + 1) * BLOCK_Q].T.astype(BF16)

    nb = n_lat // BLOCK_Q
    per_step = 4 if nb % 4 == 0 else 1

    def step(i, carry):
        sc = [scores(i * per_step + u) for u in range(per_step)]
        pr = [softmax(s_lat, s_ctx) for _, _, s_lat, s_ctx in sc]
        for (start, rows, _, _), (p_lat, p_ctx, inv_den) in zip(sc, pr):
            weighted_values(start, rows, p_lat, p_ctx, inv_den)
        return carry

    lax.fori_loop(0, nb // per_step, step, 0)


def _attn(sink, q, k, v, kc, vc, bsz, n_lat, n_ctx):
    gw = A_GROUP * HEAD_DIM
    blk = lambda rows, width: pl.BlockSpec((rows, width), lambda b, h: (b, h))
    return pl.pallas_call(
        functools.partial(_attn_kernel, n_lat=n_lat),
        out_shape=jax.ShapeDtypeStruct((bsz * n_lat, A_Q_W), BF16),
        grid=(bsz, A_KV_HEADS),
        in_specs=[pl.BlockSpec(memory_space=pltpu.SMEM), blk(n_lat, gw), blk(n_lat, HEAD_DIM),
                  blk(n_lat, HEAD_DIM), blk(n_ctx, HEAD_DIM), blk(n_ctx, HEAD_DIM)],
        out_specs=blk(n_lat, gw),
        compiler_params=_cparams(("parallel", "parallel")),
        name="attn",
    )(sink, q, k, v, kc, vc)


def _gla_kernel(q_ref, k_ref, v_ref, gf_ref, gb_ref, kc_ref, vc_ref, gfc_ref, gbc_ref, o_ref,
                sf_ref, sb_ref, ob_ref, *, n_lat, n_ctx):
    c = B_CHUNK

    def tri(nn, fn):
        r = lax.broadcasted_iota(jnp.int32, (nn, nn), 0)
        s = lax.broadcasted_iota(jnp.int32, (nn, nn), 1)
        return fn(r, s)

    kc = kc_ref[...]
    vcb = vc_ref[...].astype(BF16)
    up_strict = tri(n_ctx, lambda r, s: s > r).astype(BF16)
    lo_strict = tri(n_ctx, lambda r, s: s < r).astype(BF16)
    kd_f = (kc * jnp.exp(_tri_sum(up_strict, gfc_ref[...]))).astype(BF16)
    kd_b = (kc * jnp.exp(_tri_sum(lo_strict, gbc_ref[...]))).astype(BF16)
    sf_ref[...] = _dot_tn(vcb, kd_f)
    sb_ref[...] = _dot_tn(vcb, kd_b)

    r = c * GLA_GROUP
    ng = n_lat // r
    same = tri(r, lambda i, j: (i // c) == (j // c))
    lo_mask = jnp.logical_and(same, tri(r, lambda i, j: j <= i))
    up_mask = jnp.logical_and(same, tri(r, lambda i, j: j >= i))
    lo_incl = lo_mask.astype(BF16)
    up_incl = up_mask.astype(BF16)
    row_chunk = lax.broadcasted_iota(jnp.int32, (r, B_DK), 0) // c

    per_dir = 4 if ng % 4 == 0 else (2 if ng % 2 == 0 else 1)

    def decay_sums(rows, g_ref, tri_incl, last):
        b = _tri_sum(tri_incl, g_ref[rows, :])
        bl = jnp.concatenate(
            [jnp.broadcast_to(b[n * c + last:n * c + last + 1, :], (c, B_DK)) for n in range(GLA_GROUP)], axis=0)
        return b, bl

    def scaled_operands(rows, b, bl):
        q = q_ref[rows, :]
        k = k_ref[rows, :]
        qe = (q * jnp.exp(b)).astype(BF16)
        ke = (k * jnp.exp(-b)).astype(BF16)
        kd = k * jnp.exp(bl - b)
        kd_blocks = jnp.concatenate(
            [jnp.where(row_chunk == n, kd, 0.0).astype(BF16) for n in range(GLA_GROUP)], axis=1)
        return qe, ke, kd_blocks, jnp.exp(bl), v_ref[rows, :].astype(BF16)

    def body(i, carry):
        fwd = [pl.ds(pl.multiple_of((i * per_dir + u) * r, r), r) for u in range(per_dir)]
        bwd = [pl.ds(pl.multiple_of((ng - 1 - i * per_dir - u) * r, r), r) for u in range(per_dir)]
        streams = ([(rows, gf_ref, lo_incl, lo_mask, c - 1) for rows in fwd]
                   + [(rows, gb_ref, up_incl, up_mask, 0) for rows in bwd])
        sums = [decay_sums(rows, g_ref, tri_incl, last) for rows, g_ref, tri_incl, _, last in streams]
        ops = [scaled_operands(strm[0], b, bl) for strm, (b, bl) in zip(streams, sums)]
        a = [jnp.where(strm[3], _dot_nt(qe, ke), 0.0).astype(BF16)
             for strm, (qe, ke, _, _, _) in zip(streams, ops)]
        o_intra = [_dot(a_s, vb) for a_s, (_, _, _, _, vb) in zip(a, ops)]
        kv = [_dot_tn(vb, kd_blocks) for _, _, kd_blocks, _, vb in ops]
        o_inter = [[None] * GLA_GROUP for _ in streams]
        st_f, st_b = sf_ref[...], sb_ref[...]
        for u in range(per_dir):
            for step in range(GLA_GROUP):
                for s, n in ((u, step), (per_dir + u, GLA_GROUP - 1 - step)):
                    qe, _, _, dec, _ = ops[s]
                    st = st_f if s < per_dir else st_b
                    o_inter[s][n] = _dot_nt(qe[n * c:(n + 1) * c], st.astype(BF16))
                    st = st * dec[n * c:n * c + 1, :] + kv[s][:, n * B_DK:(n + 1) * B_DK]
                    if s < per_dir:
                        st_f = st
                    else:
                        st_b = st
        sf_ref[...] = st_f
        sb_ref[...] = st_b
        for s, (rows, *_) in enumerate(streams):
            out_ref = o_ref if s < per_dir else ob_ref
            out_ref[rows, :] = o_intra[s] + jnp.concatenate(o_inter[s], axis=0)
        return carry

    lax.fori_loop(0, ng // per_dir, body, 0)
    o_ref[...] += ob_ref[...]


def _gla(bq, bk, bv, gf, gb, bkc, bvc, gfc, gbc, bsz, n_lat, n_ctx):
    lat = lambda width: pl.BlockSpec((n_lat, width), lambda b, h: (b, h))
    cx = lambda width: pl.BlockSpec((n_ctx, width), lambda b, h: (b, h))
    return pl.pallas_call(
        functools.partial(_gla_kernel, n_lat=n_lat, n_ctx=n_ctx),
        out_shape=jax.ShapeDtypeStruct((bsz * n_lat, B_V_W), F32),
        grid=(bsz, B_HEADS),
        in_specs=[lat(B_DK), lat(B_DK), lat(B_DV), lat(B_DK), lat(B_DK),
                  cx(B_DK), cx(B_DV), cx(B_DK), cx(B_DK)],
        out_specs=lat(B_DV),
        scratch_shapes=[pltpu.VMEM((B_DV, B_DK), F32), pltpu.VMEM((B_DV, B_DK), F32),
                        pltpu.VMEM((n_lat, B_DV), F32)],
        compiler_params=_cparams(("parallel", "parallel")),
        name="gla",
    )(bq, bk, bv, gf, gb, bkc, bvc, gfc, gbc)


def _mixout_kernel(oa_ref, ob_ref, gate_ref, gn_ref, wa_ref, wb_ref, x_ref, g1_ref, o_ref):
    y = _dot(oa_ref[...], wa_ref[...])
    for hd in range(B_HEADS):
        cols = slice(hd * B_DV, (hd + 1) * B_DV)
        gt = gate_ref[:, cols]
        t = _rms(ob_ref[:, cols], gn_ref[...]) * (gt * jax.nn.sigmoid(gt))
        y = y + _dot(t.astype(BF16), wb_ref[cols, :])
    o_ref[...] = x_ref[...] + g1_ref[...] * y


def _mixout(oa, ob, gate, gla_norm, w_a, w_b, x2, mod3, mod_row_fn, tm):
    t_tok = x2.shape[0]
    tok = lambda width: pl.BlockSpec((tm, width), lambda i: (i, 0))
    const = lambda shape: pl.BlockSpec(shape, lambda i: (0, 0))
    return pl.pallas_call(
        _mixout_kernel,
        out_shape=jax.ShapeDtypeStruct((t_tok, D_MODEL), F32),
        grid=(t_tok // tm,),
        in_specs=[tok(A_Q_W), tok(B_V_W), tok(B_V_W), const((1, B_DV)),
                  const((A_Q_W, D_MODEL)), const((B_V_W, D_MODEL)), tok(D_MODEL),
                  pl.BlockSpec((None, 1, D_MODEL), lambda i: (mod_row_fn(i), 0, 2))],
        out_specs=tok(D_MODEL),
        compiler_params=_cparams(("parallel",)),
        name="mixout",
    )(oa, ob, gate, gla_norm, w_a, w_b, x2, mod3)


def _mlp_kernel(x_ref, gn_ref, sh_ref, sc_ref, g2_ref, w1_ref, w2_ref, o_ref, h_ref):
    j = pl.program_id(1)

    @pl.when(j == 0)
    def _():
        h = _rms(x_ref[...], gn_ref[...] * (1.0 + sc_ref[...])) + sh_ref[...]
        h_ref[...] = h.astype(BF16)

    def step(acc_ref):
        part = None
        for s in range(w1_ref.shape[1] // MLP_COL_BLOCK):
            cols = slice(s * MLP_COL_BLOCK, (s + 1) * MLP_COL_BLOCK)
            u = jnp.maximum(_dot(h_ref[...], w1_ref[:, cols]), 0.0)
            p = _dot((u * u).astype(BF16), w2_ref[cols, :])
            part = p if part is None else part + p
        o_ref[...] = acc_ref[...] + g2_ref[...] * part

    pl.when(j == 0)(functools.partial(step, x_ref))
    pl.when(j > 0)(functools.partial(step, o_ref))


def _mlp(x2, gn, mod3, mod_row_fn, w1, w2, layer, tm, tf):
    t_tok = x2.shape[0]
    mod = lambda col: pl.BlockSpec((None, 1, D_MODEL), lambda i, j: (mod_row_fn(i), 0, col))
    return pl.pallas_call(
        _mlp_kernel,
        out_shape=jax.ShapeDtypeStruct((t_tok, D_MODEL), F32),
        grid=(t_tok // tm, D_FF // tf),
        in_specs=[pl.BlockSpec((tm, D_MODEL), lambda i, j: (i, 0)),
                  pl.BlockSpec((1, D_MODEL), lambda i, j: (0, 0)),
                  mod(3), mod(4), mod(5),
                  pl.BlockSpec((None, D_MODEL, tf), lambda i, j: (layer, 0, j)),
                  pl.BlockSpec((None, tf, D_MODEL), lambda i, j: (layer, j, 0))],
        out_specs=pl.BlockSpec((tm, D_MODEL), lambda i, j: (i, 0)),
        scratch_shapes=[pltpu.VMEM((tm, D_MODEL), BF16)],
        compiler_params=_cparams(("parallel", "arbitrary")),
        name="mlp",
    )(x2, gn, mod3, mod3, mod3, w1, w2)


F_HALF = C_GROUPS * (C_GROUP_DIM // 2)
F_MID = LANE
F_COLS = 2 * F_HALF + F_MID


def _fch_kernel(x_ref, gn_ref, sh_ref, sc_ref, t_ref, mid_ref, yc_ref, ys_ref):
    h = _rms(x_ref[...], gn_ref[...] * (1.0 + sc_ref[...])) + sh_ref[...]
    hb = h.astype(BF16)
    half = C_GROUP_DIM // 2
    for g in range(C_GROUPS):
        y = _dot(hb[:, g * C_GROUP_DIM:(g + 1) * C_GROUP_DIM], t_ref[...])
        yc_ref[:, g * half:(g + 1) * half] = y[:, :half].astype(BF16)
        ys_ref[:, g * half:(g + 1) * half] = y[:, half:].astype(BF16)
    yc_ref[:, F_HALF:] = _dot(hb, mid_ref[...]).astype(BF16)


def _fch(x2, gn, mod3, mod_row_fn, t_ch, t_mid, tm):
    t_tok = x2.shape[0]
    mod = lambda col: pl.BlockSpec((None, 1, D_MODEL), lambda i: (mod_row_fn(i), 0, col))
    tok = lambda width: pl.BlockSpec((tm, width), lambda i: (i, 0))
    return pl.pallas_call(
        _fch_kernel,
        out_shape=[jax.ShapeDtypeStruct((t_tok, F_HALF + F_MID), BF16),
                   jax.ShapeDtypeStruct((t_tok, F_HALF), BF16)],
        grid=(t_tok // tm,),
        in_specs=[tok(D_MODEL), pl.BlockSpec((1, D_MODEL), lambda i: (0, 0)), mod(0), mod(1),
                  pl.BlockSpec((C_GROUP_DIM, C_GROUP_DIM), lambda i: (0, 0)),
                  pl.BlockSpec((D_MODEL, F_MID), lambda i: (0, 0))],
        out_specs=[tok(F_HALF + F_MID), tok(F_HALF)],
        compiler_params=_cparams(("parallel",)),
        name="fch",
    )(x2, gn, mod3, mod3, t_ch, t_mid)


F_SUB = 256


def _fpos_kernel(cn_ref, cnx_ref, sn_ref, snx_ref, yc_ref, ys_ref, o_ref, *, scale):
    yc = yc_ref[...]
    ys = ys_ref[...]

    def blocks(p, q):
        return (((p[:, :F_HALF] - q) * scale).astype(BF16), ((p[:, :F_HALF] + q) * scale).astype(BF16),
                (p[:, F_HALF:] * scale).astype(BF16))

    cur = blocks(_dot(cn_ref[...], yc), _dot(sn_ref[...], ys))
    nxt = blocks(_dot(cnx_ref[...], yc), _dot(snx_ref[...], ys))
    direct_cols = (slice(0, F_HALF), slice(F_HALF, 2 * F_HALF), slice(2 * F_HALF, F_COLS))
    mirror_cols = (direct_cols[1], direct_cols[0], direct_cols[2])
    for cols, blk in zip(direct_cols, cur):
        o_ref[0, :, cols] = blk
    rev = (lax.broadcasted_iota(jnp.int32, (F_SUB, F_SUB), 0)
           + lax.broadcasted_iota(jnp.int32, (F_SUB, F_SUB), 1) == F_SUB).astype(BF16)
    n_sub = cn_ref.shape[0] // F_SUB
    for u in range(n_sub):
        src = n_sub - 1 - u
        for cols, blk, blk_nxt in zip(mirror_cols, cur, nxt):
            after = blk_nxt[0:1] if src == n_sub - 1 else blk[(src + 1) * F_SUB:(src + 1) * F_SUB + 1]
            flipped = _dot(rev, blk[src * F_SUB:(src + 1) * F_SUB])
            first = lax.broadcasted_iota(jnp.int32, flipped.shape, 0) == 0
            o_ref[1, u * F_SUB:(u + 1) * F_SUB, cols] = jnp.where(first, after.astype(F32), flipped).astype(BF16)


def _fpos(cn, sn, yc, ys, bsz, n_lat, tile_rows):
    n_half = n_lat // 2
    nk = n_half // tile_rows
    scale = float((n_lat * C_GROUP_DIM) ** -0.5)
    tile = pl.BlockSpec((tile_rows, n_lat), lambda b, k: (k, 0))
    nxt = pl.BlockSpec((BF16_SUBLANES, n_lat), lambda b, k: ((k + 1) * (tile_rows // BF16_SUBLANES), 0))
    return pl.pallas_call(
        functools.partial(_fpos_kernel, scale=scale),
        out_shape=jax.ShapeDtypeStruct((bsz, 2, n_half, F_COLS), BF16),
        grid=(bsz, nk),
        in_specs=[tile, nxt, tile, nxt,
                  pl.BlockSpec((n_lat, F_HALF + F_MID), lambda b, k: (b, 0)),
                  pl.BlockSpec((n_lat, F_HALF), lambda b, k: (b, 0))],
        out_specs=pl.BlockSpec((None, 2, tile_rows, F_COLS), lambda b, k: (b, 0, k, 0)),
        compiler_params=_cparams(("parallel", "parallel")),
        name="fpos",
    )(cn, cn, sn, sn, yc, ys)


def _linres_kernel(*refs, n_z):
    z_refs, (w_ref, b_ref, x_ref, g1_ref, o_ref, wm_ref) = refs[:n_z], refs[n_z:]

    @pl.when(pl.program_id(0) == 0)
    def _():
        half = C_GROUP_DIM // 2
        rev = (lax.broadcasted_iota(jnp.int32, (half, half), 0)
               + lax.broadcasted_iota(jnp.int32, (half, half), 1) == half).astype(BF16)
        mid = []
        for g in range(C_GROUPS):
            base = g * C_GROUP_DIM
            wm_ref[g * half:(g + 1) * half, :] = w_ref[base:base + half, :].astype(BF16)
            upper = w_ref[base + half:base + C_GROUP_DIM, :].astype(BF16)
            wm_ref[F_HALF + g * half:F_HALF + (g + 1) * half, :] = _dot(rev, upper).astype(BF16)
            mid.append(w_ref[base + half:base + half + 1, :])
        mid.append(jnp.zeros((F_MID - C_GROUPS, D_MODEL), F32))
        wm_ref[2 * F_HALF:, :] = jnp.concatenate(mid, axis=0).astype(BF16)

    z = jnp.concatenate([z_ref[...] for z_ref in z_refs], axis=0)
    y = _dot(z, wm_ref[...]) + b_ref[...]
    o_ref[...] = x_ref[...] + g1_ref[...] * y


def _linres(z, z_rows, w, bias, x2, mod3, mod_row_fn, tm, n_lat):
    t_tok = x2.shape[0]
    tiles_per_batch = n_lat // tm
    n_z = tm // z_rows
    nk = z.shape[2] // z_rows

    def z_spec(r):
        def index(i):
            pos_tile = (i % tiles_per_batch) * n_z + r
            half, k = pos_tile // nk, pos_tile % nk
            return i // tiles_per_batch, half, jnp.where(half == 0, k, nk - 1 - k), 0
        return pl.BlockSpec((None, None, z_rows, F_COLS), index)

    tok = lambda width: pl.BlockSpec((tm, width), lambda i: (i, 0))
    return pl.pallas_call(
        functools.partial(_linres_kernel, n_z=n_z),
        out_shape=jax.ShapeDtypeStruct((t_tok, D_MODEL), F32),
        grid=(t_tok // tm,),
        in_specs=[z_spec(r) for r in range(n_z)] + [
            pl.BlockSpec((D_MODEL, D_MODEL), lambda i: (0, 0), pipeline_mode=pl.Buffered(1)),
            pl.BlockSpec((1, D_MODEL), lambda i: (0, 0)), tok(D_MODEL),
            pl.BlockSpec((None, 1, D_MODEL), lambda i: (mod_row_fn(i), 0, 2))],
        out_specs=tok(D_MODEL),
        scratch_shapes=[pltpu.VMEM((F_COLS, D_MODEL), BF16)],
        compiler_params=_cparams(("arbitrary",)),
        name="linres",
    )(*([z] * n_z), w, bias, x2, mod3)


def _rope_tables(n_lat):
    t = np.arange(n_lat)
    row = (t // GRID_W).astype(np.float32)
    col = (t % GRID_W).astype(np.float32)
    n_freq = HEAD_DIM // 4
    inv_freq = np.float32(ROPE_BASE) ** (-np.arange(n_freq, dtype=np.float32) / np.float32(n_freq))
    ang_r, ang_c = row[:, None] * inv_freq, col[:, None] * inv_freq
    cr, sr, cc, sc = np.cos(ang_r), np.sin(ang_r), np.cos(ang_c), np.sin(ang_c)
    zero = np.zeros_like(sr)
    cos = np.concatenate([cr, cr, cc, cc], axis=-1)
    sin_a = np.concatenate([-sr, zero, -sc, zero], axis=-1)
    sin_b = np.concatenate([zero, sr, zero, sc], axis=-1)
    return tuple(jnp.asarray(tab, dtype=F32) for tab in (cos, sin_a, sin_b))


def _dft_tables(n_lat):
    def cs(n):
        idx = np.arange(n, dtype=np.int64)
        ang = 2.0 * np.pi * ((idx[:, None] * idx[None, :]) % n).astype(np.float64) / n
        return np.cos(ang), np.sin(ang)

    half = C_GROUP_DIM // 2
    cm, sm = cs(C_GROUP_DIM)
    cn, sn = cs(n_lat)
    t_ch = np.concatenate([cm[:, :half], sm[:, :half]], axis=1)
    t_mid = np.zeros((D_MODEL, F_MID))
    for g in range(C_GROUPS):
        t_mid[g * C_GROUP_DIM:(g + 1) * C_GROUP_DIM, g] = cm[:, half]
    as_bf16 = lambda t: jnp.asarray(t, dtype=F32).astype(BF16)
    return as_bf16(t_ch), as_bf16(t_mid), as_bf16(cn), as_bf16(sn)


def kernel(x, c, ctx, c_ctx, ada_w, ada_b, norm_mix, norm_mlp, mlp_w1, mlp_w2, ab_w_in, ab_q_norm, ab_k_norm,
           ab_sink, ab_gk_f, ab_gk_f_bias, ab_gk_b, ab_gk_b_bias, ab_gla_norm, ab_w_out, c_w_out, c_b_out):
    bsz, n_lat, _ = x.shape
    n_ctx = ctx.shape[1]
    depth = ada_w.shape[0]
    assert depth == 2 and bsz < MOD_ROWS
    t_tok = bsz * n_lat

    cvec = jnp.concatenate([c, c_ctx[None, :], jnp.zeros((MOD_ROWS - bsz - 1, D_MODEL), F32)], axis=0)
    mod = _ada(cvec, ada_w, ada_b)
    mod3 = mod.reshape(depth * MOD_ROWS, 1, 6 * D_MODEL)

    x2 = x.reshape(t_tok, D_MODEL)

    tm = min(256, n_lat)
    tpb = n_lat // tm
    lat_row0 = lambda i: i // tpb
    w_in = ab_w_in[0].astype(BF16)
    gn_mix0 = norm_mix[0].reshape(1, D_MODEL)
    qn = ab_q_norm[0].reshape(1, HEAD_DIM)
    kn = ab_k_norm[0].reshape(1, HEAD_DIM)
    gkf = jnp.pad(ab_gk_f[0], ((0, LR_COLS - B_GATE_RANK), (0, 0))).astype(BF16)
    gkb = jnp.pad(ab_gk_b[0], ((LR_COLS - B_GATE_RANK, 0), (0, 0))).astype(BF16)
    gkfb = ab_gk_f_bias[0].reshape(1, B_QK_W)
    gkbb = ab_gk_b_bias[0].reshape(1, B_QK_W)
    steps = t_tok // tm
    mlp_rows = depth * D_MODEL * D_FF // steps
    ride_along = (depth * D_MODEL) % (steps * BF16_SUBLANES) == 0 and mlp_rows * 4 <= CAST_BLOCK_BYTES
    cast = (mlp_w1.reshape(depth * D_MODEL, D_FF), mlp_w2.reshape(depth * D_FF, D_MODEL)) if ride_along else ()
    q, k, v, bq, bk, bv, gate, gf, gb, *cast_out = _proj(
        x2, gn_mix0, mod3, lat_row0, tpb, tm, w_in, qn, kn, _rope_tables(n_lat), gkf, gkfb, gkb, gkbb, True, cast)
    if ride_along:
        w1 = cast_out[0].reshape(depth, D_MODEL, D_FF)
        w2 = cast_out[1].reshape(depth, D_FF, D_MODEL)
    else:
        w1, w2 = mlp_w1.astype(BF16), mlp_w2.astype(BF16)
    tmc = min(256, n_ctx)
    kc, vc, bkc, bvc, gfc, gbc = _proj(
        ctx.reshape(bsz * n_ctx, D_MODEL), gn_mix0, mod3, lambda i: bsz, n_ctx // tmc, tmc, w_in, None, kn, None,
        gkf, gkfb, gkb, gkbb, False)

    oa = _attn(ab_sink[0], q, k, v, kc, vc, bsz, n_lat, n_ctx)
    ob = _gla(bq, bk, bv, gf, gb, bkc, bvc, gfc, gbc, bsz, n_lat, n_ctx)

    w_out = ab_w_out[0].astype(BF16)
    tmo = min(512, n_lat)
    tpbo = n_lat // tmo
    x2 = _mixout(oa, ob, gate, ab_gla_norm[0].reshape(1, B_DV), w_out[:A_Q_W], w_out[A_Q_W:], x2, mod3,
                 lambda i: i // tpbo, tmo)
    x2 = _mlp(x2, norm_mlp[0].reshape(1, D_MODEL), mod3, lambda i: i // tpbo, w1, w2, 0, tmo, 2048)

    t_ch, t_mid, cn, sn = _dft_tables(n_lat)
    row1 = lambda i: MOD_ROWS + i // tpbo
    yc, ys = _fch(x2, norm_mix[1].reshape(1, D_MODEL), mod3, row1, t_ch, t_mid, tmo)
    z_rows = min(tmo, n_lat // 2)
    z = _fpos(cn, sn, yc, ys, bsz, n_lat, z_rows)
    x2 = _linres(z, z_rows, c_w_out[0], c_b_out[0].reshape(1, D_MODEL), x2, mod3, row1, tmo, n_lat)
    x2 = _mlp(x2, norm_mlp[1].reshape(1, D_MODEL), mod3, row1, w1, w2, 1, tmo, 2048)
    return x2.reshape(bsz, n_lat, D_MODEL)
```

```python
import functools

import numpy as np
import jax
import jax.numpy as jnp
from jax import lax
from jax.experimental import pallas as pl
from jax.experimental.pallas import tpu as pltpu

D_MODEL = 2048
GRID_W = 64
EPS = 1e-6
NEG_INF = -1e30
HEAD_DIM = 128
A_Q_HEADS = 8
A_KV_HEADS = 2
A_GROUP = A_Q_HEADS // A_KV_HEADS
WINDOW = 128
BLOCK_Q = 128
LOG2_E = 1.4426950408889634
Q_SCALE = HEAD_DIM ** -0.5 * LOG2_E
ROPE_BASE = 10000.0
B_HEADS = 4
B_DV = 256
B_DK = 128
B_GATE_RANK = 16
B_GATE_NORM = 16.0
B_CHUNK = 64
GLA_GROUP = 4
C_GROUPS = 8
C_GROUP_DIM = D_MODEL // C_GROUPS
D_FF = 4 * D_MODEL

A_Q_W = A_Q_HEADS * HEAD_DIM
A_KV_W = A_KV_HEADS * HEAD_DIM
B_QK_W = B_HEADS * B_DK
B_V_W = B_HEADS * B_DV
OFF_AQ = 0
OFF_AK = OFF_AQ + A_Q_W
OFF_AV = OFF_AK + A_KV_W
OFF_BQ = OFF_AV + A_KV_W
OFF_BK = OFF_BQ + B_QK_W
OFF_BV = OFF_BK + B_QK_W
OFF_GATE = OFF_BV + B_V_W
OFF_LR = OFF_GATE + B_V_W
LR_COLS = 2 * B_GATE_RANK
IN_COLS = OFF_LR + LR_COLS
LANE = 128
BF16_SUBLANES = 16
MOD_ROWS = BF16_SUBLANES
VMEM_LIMIT = 56 * 1024 * 1024
CAST_BLOCK_BYTES = 2 * 1024 * 1024
MLP_COL_BLOCK = 1024
BF16 = jnp.bfloat16
F32 = jnp.float32


def _cparams(sem):
    return pltpu.CompilerParams(dimension_semantics=sem, vmem_limit_bytes=VMEM_LIMIT)


def _dot(a, b):
    return jnp.dot(a, b, preferred_element_type=F32)


def _dot_nt(a, b):
    return lax.dot_general(a, b, (((1,), (1,)), ((), ())), preferred_element_type=F32)


def _dot_tn(a, b):
    return lax.dot_general(a, b, (((0,), (0,)), ((), ())), preferred_element_type=F32)


def _tri_sum(tri, g):
    g_hi = g.astype(BF16)
    g_lo = (g - g_hi.astype(F32)).astype(BF16)
    d = g.shape[1]
    both = _dot(tri, jnp.concatenate([g_hi, g_lo], axis=1))
    return both[:, :d] + both[:, d:]


def _rms(xf, gain):
    return xf * lax.rsqrt(jnp.mean(xf * xf, axis=-1, keepdims=True) + EPS) * gain


def _ada_kernel(c_ref, w_ref, b_ref, o_ref):
    cv = c_ref[...]
    s = (cv * jax.nn.sigmoid(cv)).astype(BF16)
    o_ref[...] = _dot(s, w_ref[...].astype(BF16)) + b_ref[...]


def _ada(cvec, ada_w, ada_b):
    depth = ada_w.shape[0]
    n = ada_w.shape[2]
    tn = 1024
    return pl.pallas_call(
        _ada_kernel,
        out_shape=jax.ShapeDtypeStruct((depth, MOD_ROWS, n), F32),
        grid=(depth, n // tn),
        in_specs=[
            pl.BlockSpec((MOD_ROWS, D_MODEL), lambda l, j: (0, 0)),
            pl.BlockSpec((None, D_MODEL, tn), lambda l, j: (l, 0, j)),
            pl.BlockSpec((None, 1, tn), lambda l, j: (l, 0, j)),
        ],
        out_specs=pl.BlockSpec((None, MOD_ROWS, tn), lambda l, j: (l, 0, j)),
        compiler_params=_cparams(("parallel", "parallel")),
        name="ada",
    )(cvec, ada_w, ada_b.reshape(depth, 1, n))


def _proj_kernel(*refs, latent, n_cast):
    for src_ref, dst_ref in zip(refs[:n_cast], refs[len(refs) - n_cast:]):
        dst_ref[...] = src_ref[...].astype(BF16)
    refs = refs[n_cast:len(refs) - n_cast]
    if latent:
        (x_ref, gn_ref, sh_ref, sc_ref, w_ref, qn_ref, kn_ref, cos_ref, sa_ref, sb_ref,
         gkf_ref, gkfb_ref, gkb_ref, gkbb_ref,
         q_ref, k_ref, v_ref, bq_ref, bk_ref, bv_ref, gate_ref, gf_ref, gb_ref) = refs
    else:
        (x_ref, gn_ref, sh_ref, sc_ref, w_ref, kn_ref,
         gkf_ref, gkfb_ref, gkb_ref, gkbb_ref,
         k_ref, v_ref, bk_ref, bv_ref, gf_ref, gb_ref) = refs

    xt = x_ref[...]
    h = _rms(xt, gn_ref[...]) * (1.0 + sc_ref[...]) + sh_ref[...]
    hb = h.astype(BF16)

    def proj(off, width):
        return _dot(hb, w_ref[:, off:off + width])

    def rope(t):
        return (t * cos_ref[...] + pltpu.roll(t, HEAD_DIM - HEAD_DIM // 4, 1) * sa_ref[...]
                + pltpu.roll(t, HEAD_DIM // 4, 1) * sb_ref[...])

    if latent:
        for half in range(2):
            seg = proj(OFF_AQ + half * (A_Q_W // 2), A_Q_W // 2)
            for j in range(A_Q_HEADS // 2):
                t = _rms(seg[:, j * HEAD_DIM:(j + 1) * HEAD_DIM], qn_ref[...])
                t = rope(t) * Q_SCALE
                hd = half * (A_Q_HEADS // 2) + j
                q_ref[:, hd * HEAD_DIM:(hd + 1) * HEAD_DIM] = t.astype(BF16)

    seg = proj(OFF_AK, 2 * A_KV_W)
    for j in range(A_KV_HEADS):
        t = _rms(seg[:, j * HEAD_DIM:(j + 1) * HEAD_DIM], kn_ref[...])
        if latent:
            t = rope(t)
        k_ref[:, j * HEAD_DIM:(j + 1) * HEAD_DIM] = t.astype(BF16)
    v_ref[...] = seg[:, A_KV_W:].astype(BF16)

    if latent:
        bq_ref[...] = proj(OFF_BQ, B_QK_W) * B_DK ** -0.5
    bk_ref[...] = proj(OFF_BK, B_QK_W)
    for half in range(2):
        bv_ref[:, half * (B_V_W // 2):(half + 1) * (B_V_W // 2)] = proj(OFF_BV + half * (B_V_W // 2), B_V_W // 2)
    if latent:
        for half in range(2):
            gate_ref[:, half * (B_V_W // 2):(half + 1) * (B_V_W // 2)] = proj(
                OFF_GATE + half * (B_V_W // 2), B_V_W // 2)

    lr = proj(OFF_LR, LR_COLS).astype(BF16)

    def log_decay(gk_ref, gkb_ref, out_ref):
        z = _dot(lr, gk_ref[...]) + gkb_ref[...]
        out_ref[...] = (jnp.minimum(z, 0.0) - jnp.log1p(jnp.exp(-jnp.abs(z)))) / B_GATE_NORM

    log_decay(gkf_ref, gkfb_ref, gf_ref)
    log_decay(gkb_ref, gkbb_ref, gb_ref)


def _proj(x2, gn, mod3, mod_row_fn, tiles_per_batch, tm, w_in, qn, kn, rope_tabs, gkf, gkfb, gkb, gkbb, latent,
          cast=()):
    t_tok = x2.shape[0]
    steps = t_tok // tm
    row = lambda width: pl.BlockSpec((1, width), lambda i: (0, 0))
    tok = lambda width: pl.BlockSpec((tm, width), lambda i: (i, 0))
    mod = lambda col: pl.BlockSpec((None, 1, D_MODEL), lambda i: (mod_row_fn(i), 0, col))
    in_specs = [tok(D_MODEL), row(D_MODEL), mod(0), mod(1),
                pl.BlockSpec((D_MODEL, IN_COLS), lambda i: (0, 0), pipeline_mode=pl.Buffered(1))]
    args = [x2, gn, mod3, mod3, w_in]
    if latent:
        in_specs += [row(HEAD_DIM), row(HEAD_DIM)]
        args += [qn, kn]
        in_specs += [pl.BlockSpec((tm, HEAD_DIM), lambda i: (i % tiles_per_batch, 0))] * 3
        args += list(rope_tabs)
    else:
        in_specs += [row(HEAD_DIM)]
        args += [kn]
    in_specs += [pl.BlockSpec((LR_COLS, B_QK_W), lambda i: (0, 0)), row(B_QK_W)] * 2
    args += [gkf, gkfb, gkb, gkbb]

    def out(width, dtype):
        return jax.ShapeDtypeStruct((t_tok, width), dtype), tok(width)

    if latent:
        outs = [out(A_Q_W, BF16), out(A_KV_W, BF16), out(A_KV_W, BF16), out(B_QK_W, F32), out(B_QK_W, F32),
                out(B_V_W, F32), out(B_V_W, F32), out(B_QK_W, F32), out(B_QK_W, F32)]
    else:
        outs = [out(A_KV_W, BF16), out(A_KV_W, BF16), out(B_QK_W, F32), out(B_V_W, F32),
                out(B_QK_W, F32), out(B_QK_W, F32)]
    cast_specs = [pl.BlockSpec((a.shape[0] // steps, a.shape[1]), lambda i: (i, 0)) for a in cast]
    outs += [(jax.ShapeDtypeStruct(a.shape, BF16), spec) for a, spec in zip(cast, cast_specs)]
    return pl.pallas_call(
        functools.partial(_proj_kernel, latent=latent, n_cast=len(cast)),
        out_shape=[o[0] for o in outs],
        grid=(steps,),
        in_specs=cast_specs + in_specs,
        out_specs=[o[1] for o in outs],
        compiler_params=_cparams(("parallel",)),
        name="proj_lat" if latent else "proj_ctx",
    )(*cast, *args)


def _attn_kernel(sink_ref, q_ref, k_ref, v_ref, kc_ref, vc_ref, o_ref, *, n_lat):
    kvh = pl.program_id(1)
    n_win = 3 * BLOCK_Q
    heads = range(A_GROUP)
    kc = kc_ref[...]
    vc = vc_ref[...]
    sk = jnp.concatenate(
        [jnp.full((1, BLOCK_Q), sink_ref[kvh * A_GROUP + g] * LOG2_E, F32) for g in heads], axis=1)
    key_minus_query = (lax.broadcasted_iota(jnp.int32, (n_win, BLOCK_Q), 0)
                       - lax.broadcasted_iota(jnp.int32, (n_win, BLOCK_Q), 1))

    def scores(n):
        start = jnp.clip(n * BLOCK_Q - BLOCK_Q, 0, n_lat - n_win)
        start = pl.multiple_of(start, BLOCK_Q)
        rows = pl.ds(pl.multiple_of(n * BLOCK_Q, BLOCK_Q), BLOCK_Q)
        kw = k_ref[pl.ds(start, n_win), :]
        valid = jnp.abs(key_minus_query + (start - n * BLOCK_Q)) <= WINDOW
        q = jnp.concatenate([q_ref[rows, g * HEAD_DIM:(g + 1) * HEAD_DIM] for g in heads], axis=0)
        s_lat = _dot_nt(kw, q)
        s_lat = jnp.concatenate(
            [jnp.where(valid, s_lat[:, g * BLOCK_Q:(g + 1) * BLOCK_Q], NEG_INF) for g in heads], axis=1)
        s_ctx = _dot_nt(kc, q)
        return start, rows, s_lat, s_ctx

    def softmax(s_lat, s_ctx):
        m = jnp.maximum(jnp.maximum(jnp.max(s_lat, axis=0, keepdims=True),
                                    jnp.max(s_ctx, axis=0, keepdims=True)), sk)
        p_lat = jnp.exp2(s_lat - m)
        p_ctx = jnp.exp2(s_ctx - m)
        den = (jnp.sum(p_lat, axis=0, keepdims=True) + jnp.sum(p_ctx, axis=0, keepdims=True)
               + jnp.exp2(sk - m))
        return p_lat.astype(BF16), p_ctx.astype(BF16), 1.0 / den

    def weighted_values(start, rows, p_lat, p_ctx, inv_den):
        vw = v_ref[pl.ds(start, n_win), :]
        o_t = (_dot_tn(vw, p_lat) + _dot_tn(vc, p_ctx)) * inv_den
        for g in heads:
            o_ref[rows, g * HEAD_DIM:(g + 1) * HEAD_DIM] = o_t[:, g * BLOCK_Q:(g + 1) * BLOCK_Q].T.astype(BF16)

    nb = n_lat // BLOCK_Q
    per_step = 8 if nb % 8 == 0 else (4 if nb % 4 == 0 else 1)

    def step(i, carry):
        sc = [scores(i * per_step + u) for u in range(per_step)]
        pr = [softmax(s_lat, s_ctx) for _, _, s_lat, s_ctx in sc]
        for (start, rows, _, _), (p_lat, p_ctx, inv_den) in zip(sc, pr):
            weighted_values(start, rows, p_lat, p_ctx, inv_den)
        return carry

    lax.fori_loop(0, nb // per_step, step, 0)


def _attn(sink, q, k, v, kc, vc, bsz, n_lat, n_ctx):
    gw = A_GROUP * HEAD_DIM
    blk = lambda rows, width: pl.BlockSpec((rows, width), lambda b, h: (b, h))
    return pl.pallas_call(
        functools.partial(_attn_kernel, n_lat=n_lat),
        out_shape=jax.ShapeDtypeStruct((bsz * n_lat, A_Q_W), BF16),
        grid=(bsz, A_KV_HEADS),
        in_specs=[pl.BlockSpec(memory_space=pltpu.SMEM), blk(n_lat, gw), blk(n_lat, HEAD_DIM),
                  blk(n_lat, HEAD_DIM), blk(n_ctx, HEAD_DIM), blk(n_ctx, HEAD_DIM)],
        out_specs=blk(n_lat, gw),
        compiler_params=_cparams(("parallel", "parallel")),
        name="attn",
    )(sink, q, k, v, kc, vc)


def _gla_kernel(q_ref, k_ref, v_ref, gf_ref, gb_ref, kc_ref, vc_ref, gfc_ref, gbc_ref, o_ref,
                sf_ref, sb_ref, ob_ref, *, n_lat, n_ctx):
    c = B_CHUNK

    def tri(nn, fn):
        r = lax.broadcasted_iota(jnp.int32, (nn, nn), 0)
        s = lax.broadcasted_iota(jnp.int32, (nn, nn), 1)
        return fn(r, s)

    kc = kc_ref[...]
    vcb = vc_ref[...].astype(BF16)
    up_strict = tri(n_ctx, lambda r, s: s > r).astype(BF16)
    lo_strict = tri(n_ctx, lambda r, s: s < r).astype(BF16)
    kd_f = (kc * jnp.exp(_tri_sum(up_strict, gfc_ref[...]))).astype(BF16)
    kd_b = (kc * jnp.exp(_tri_sum(lo_strict, gbc_ref[...]))).astype(BF16)
    sf_ref[...] = _dot_tn(vcb, kd_f)
    sb_ref[...] = _dot_tn(vcb, kd_b)

    r = c * GLA_GROUP
    ng = n_lat // r
    same = tri(r, lambda i, j: (i // c) == (j // c))
    lo_mask = jnp.logical_and(same, tri(r, lambda i, j: j <= i))
    up_mask = jnp.logical_and(same, tri(r, lambda i, j: j >= i))
    lo_incl = lo_mask.astype(BF16)
    up_incl = up_mask.astype(BF16)
    row_chunk = lax.broadcasted_iota(jnp.int32, (r, B_DK), 0) // c

    per_dir = 8 if ng % 8 == 0 else (4 if ng % 4 == 0 else (2 if ng % 2 == 0 else 1))

    def decay_sums(rows, g_ref, tri_incl, last):
        b = _tri_sum(tri_incl, g_ref[rows, :])
        bl = jnp.concatenate(
            [jnp.broadcast_to(b[n * c + last:n * c + last + 1, :], (c, B_DK)) for n in range(GLA_GROUP)], axis=0)
        return b, bl

    def scaled_operands(rows, b, bl):
        q = q_ref[rows, :]
        k = k_ref[rows, :]
        qe = (q * jnp.exp(b)).astype(BF16)
        ke = (k * jnp.exp(-b)).astype(BF16)
        kd = k * jnp.exp(bl - b)
        kd_blocks = jnp.concatenate(
            [jnp.where(row_chunk == n, kd, 0.0).astype(BF16) for n in range(GLA_GROUP)], axis=1)
        return qe, ke, kd_blocks, jnp.exp(bl), v_ref[rows, :].astype(BF16)

    def body(i, carry):
        fwd = [pl.ds(pl.multiple_of((i * per_dir + u) * r, r), r) for u in range(per_dir)]
        bwd = [pl.ds(pl.multiple_of((ng - 1 - i * per_dir - u) * r, r), r) for u in range(per_dir)]
        streams = ([(rows, gf_ref, lo_incl, lo_mask, c - 1) for rows in fwd]
                   + [(rows, gb_ref, up_incl, up_mask, 0) for rows in bwd])
        sums = [decay_sums(rows, g_ref, tri_incl, last) for rows, g_ref, tri_incl, _, last in streams]
        ops = [scaled_operands(strm[0], b, bl) for strm, (b, bl) in zip(streams, sums)]
        a = [jnp.where(strm[3], _dot_nt(qe, ke), 0.0).astype(BF16)
             for strm, (qe, ke, _, _, _) in zip(streams, ops)]
        o_intra = [_dot(a_s, vb) for a_s, (_, _, _, _, vb) in zip(a, ops)]
        kv = [_dot_tn(vb, kd_blocks) for _, _, kd_blocks, _, vb in ops]
        o_inter = [[None] * GLA_GROUP for _ in streams]
        st_f, st_b = sf_ref[...], sb_ref[...]
        for u in range(per_dir):
            for step in range(GLA_GROUP):
                for s, n in ((u, step), (per_dir + u, GLA_GROUP - 1 - step)):
                    qe, _, _, dec, _ = ops[s]
                    st = st_f if s < per_dir else st_b
                    o_inter[s][n] = _dot_nt(qe[n * c:(n + 1) * c], st.astype(BF16))
                    st = st * dec[n * c:n * c + 1, :] + kv[s][:, n * B_DK:(n + 1) * B_DK]
                    if s < per_dir:
                        st_f = st
                    else:
                        st_b = st
        sf_ref[...] = st_f
        sb_ref[...] = st_b
        for s, (rows, *_) in enumerate(streams):
            out_ref = o_ref if s < per_dir else ob_ref
            out_ref[rows, :] = o_intra[s] + jnp.concatenate(o_inter[s], axis=0)
        return carry

    lax.fori_loop(0, ng // per_dir, body, 0)
    o_ref[...] += ob_ref[...]


def _gla(bq, bk, bv, gf, gb, bkc, bvc, gfc, gbc, bsz, n_lat, n_ctx):
    lat = lambda width: pl.BlockSpec((n_lat, width), lambda b, h: (b, h))
    cx = lambda width: pl.BlockSpec((n_ctx, width), lambda b, h: (b, h))
    return pl.pallas_call(
        functools.partial(_gla_kernel, n_lat=n_lat, n_ctx=n_ctx),
        out_shape=jax.ShapeDtypeStruct((bsz * n_lat, B_V_W), F32),
        grid=(bsz, B_HEADS),
        in_specs=[lat(B_DK), lat(B_DK), lat(B_DV), lat(B_DK), lat(B_DK),
                  cx(B_DK), cx(B_DV), cx(B_DK), cx(B_DK)],
        out_specs=lat(B_DV),
        scratch_shapes=[pltpu.VMEM((B_DV, B_DK), F32), pltpu.VMEM((B_DV, B_DK), F32),
                        pltpu.VMEM((n_lat, B_DV), F32)],
        compiler_params=_cparams(("parallel", "parallel")),
        name="gla",
    )(bq, bk, bv, gf, gb, bkc, bvc, gfc, gbc)


def _mixout_kernel(oa_ref, ob_ref, gate_ref, gn_ref, wa_ref, wb_ref, x_ref, g1_ref, o_ref):
    y = _dot(oa_ref[...], wa_ref[...])
    for hd in range(B_HEADS):
        cols = slice(hd * B_DV, (hd + 1) * B_DV)
        gt = gate_ref[:, cols]
        t = _rms(ob_ref[:, cols], gn_ref[...]) * (gt * jax.nn.sigmoid(gt))
        y = y + _dot(t.astype(BF16), wb_ref[cols, :])
    o_ref[...] = x_ref[...] + g1_ref[...] * y


def _mixout(oa, ob, gate, gla_norm, w_a, w_b, x2, mod3, mod_row_fn, tm):
    t_tok = x2.shape[0]
    tok = lambda width: pl.BlockSpec((tm, width), lambda i: (i, 0))
    const = lambda shape: pl.BlockSpec(shape, lambda i: (0, 0))
    return pl.pallas_call(
        _mixout_kernel,
        out_shape=jax.ShapeDtypeStruct((t_tok, D_MODEL), F32),
        grid=(t_tok // tm,),
        in_specs=[tok(A_Q_W), tok(B_V_W), tok(B_V_W), const((1, B_DV)),
                  const((A_Q_W, D_MODEL)), const((B_V_W, D_MODEL)), tok(D_MODEL),
                  pl.BlockSpec((None, 1, D_MODEL), lambda i: (mod_row_fn(i), 0, 2))],
        out_specs=tok(D_MODEL),
        compiler_params=_cparams(("parallel",)),
        name="mixout",
    )(oa, ob, gate, gla_norm, w_a, w_b, x2, mod3)


def _mlp_kernel(x_ref, gn_ref, sh_ref, sc_ref, g2_ref, w1_ref, w2_ref, o_ref, h_ref):
    j = pl.program_id(1)

    @pl.when(j == 0)
    def _():
        h = _rms(x_ref[...], gn_ref[...] * (1.0 + sc_ref[...])) + sh_ref[...]
        h_ref[...] = h.astype(BF16)

    def step(acc_ref):
        part = None
        for s in range(w1_ref.shape[1] // MLP_COL_BLOCK):
            cols = slice(s * MLP_COL_BLOCK, (s + 1) * MLP_COL_BLOCK)
            u = jnp.maximum(_dot(h_ref[...], w1_ref[:, cols]), 0.0)
            p = _dot((u * u).astype(BF16), w2_ref[cols, :])
            part = p if part is None else part + p
        o_ref[...] = acc_ref[...] + g2_ref[...] * part

    pl.when(j == 0)(functools.partial(step, x_ref))
    pl.when(j > 0)(functools.partial(step, o_ref))


def _mlp(x2, gn, mod3, mod_row_fn, w1, w2, layer, tm, tf):
    t_tok = x2.shape[0]
    mod = lambda col: pl.BlockSpec((None, 1, D_MODEL), lambda i, j: (mod_row_fn(i), 0, col))
    return pl.pallas_call(
        _mlp_kernel,
        out_shape=jax.ShapeDtypeStruct((t_tok, D_MODEL), F32),
        grid=(t_tok // tm, D_FF // tf),
        in_specs=[pl.BlockSpec((tm, D_MODEL), lambda i, j: (i, 0)),
                  pl.BlockSpec((1, D_MODEL), lambda i, j: (0, 0)),
                  mod(3), mod(4), mod(5),
                  pl.BlockSpec((None, D_MODEL, tf), lambda i, j: (layer, 0, j)),
                  pl.BlockSpec((None, tf, D_MODEL), lambda i, j: (layer, j, 0))],
        out_specs=pl.BlockSpec((tm, D_MODEL), lambda i, j: (i, 0)),
        scratch_shapes=[pltpu.VMEM((tm, D_MODEL), BF16)],
        compiler_params=_cparams(("parallel", "arbitrary")),
        name="mlp",
    )(x2, gn, mod3, mod3, mod3, w1, w2)


F_HALF = C_GROUPS * (C_GROUP_DIM // 2)
F_MID = LANE
F_COLS = 2 * F_HALF + F_MID


def _fch_kernel(x_ref, gn_ref, sh_ref, sc_ref, t_ref, mid_ref, yc_ref, ys_ref):
    h = _rms(x_ref[...], gn_ref[...] * (1.0 + sc_ref[...])) + sh_ref[...]
    hb = h.astype(BF16)
    half = C_GROUP_DIM // 2
    for g in range(C_GROUPS):
        y = _dot(hb[:, g * C_GROUP_DIM:(g + 1) * C_GROUP_DIM], t_ref[...])
        yc_ref[:, g * half:(g + 1) * half] = y[:, :half].astype(BF16)
        ys_ref[:, g * half:(g + 1) * half] = y[:, half:].astype(BF16)
    yc_ref[:, F_HALF:] = _dot(hb, mid_ref[...]).astype(BF16)


def _fch(x2, gn, mod3, mod_row_fn, t_ch, t_mid, tm):
    t_tok = x2.shape[0]
    mod = lambda col: pl.BlockSpec((None, 1, D_MODEL), lambda i: (mod_row_fn(i), 0, col))
    tok = lambda width: pl.BlockSpec((tm, width), lambda i: (i, 0))
    return pl.pallas_call(
        _fch_kernel,
        out_shape=[jax.ShapeDtypeStruct((t_tok, F_HALF + F_MID), BF16),
                   jax.ShapeDtypeStruct((t_tok, F_HALF), BF16)],
        grid=(t_tok // tm,),
        in_specs=[tok(D_MODEL), pl.BlockSpec((1, D_MODEL), lambda i: (0, 0)), mod(0), mod(1),
                  pl.BlockSpec((C_GROUP_DIM, C_GROUP_DIM), lambda i: (0, 0)),
                  pl.BlockSpec((D_MODEL, F_MID), lambda i: (0, 0))],
        out_specs=[tok(F_HALF + F_MID), tok(F_HALF)],
        compiler_params=_cparams(("parallel",)),
        name="fch",
    )(x2, gn, mod3, mod3, t_ch, t_mid)


F_SUB = 256


def _fpos_kernel(cn_ref, cnx_ref, sn_ref, snx_ref, yc_ref, ys_ref, o_ref, *, scale):
    yc = yc_ref[...]
    ys = ys_ref[...]

    def blocks(p, q):
        return (((p[:, :F_HALF] - q) * scale).astype(BF16), ((p[:, :F_HALF] + q) * scale).astype(BF16),
                (p[:, F_HALF:] * scale).astype(BF16))

    cur = blocks(_dot(cn_ref[...], yc), _dot(sn_ref[...], ys))
    nxt = blocks(_dot(cnx_ref[...], yc), _dot(snx_ref[...], ys))
    direct_cols = (slice(0, F_HALF), slice(F_HALF, 2 * F_HALF), slice(2 * F_HALF, F_COLS))
    mirror_cols = (direct_cols[1], direct_cols[0], direct_cols[2])
    for cols, blk in zip(direct_cols, cur):
        o_ref[0, :, cols] = blk
    rev = (lax.broadcasted_iota(jnp.int32, (F_SUB, F_SUB), 0)
           + lax.broadcasted_iota(jnp.int32, (F_SUB, F_SUB), 1) == F_SUB).astype(BF16)
    n_sub = cn_ref.shape[0] // F_SUB
    for u in range(n_sub):
        src = n_sub - 1 - u
        for cols, blk, blk_nxt in zip(mirror_cols, cur, nxt):
            after = blk_nxt[0:1] if src == n_sub - 1 else blk[(src + 1) * F_SUB:(src + 1) * F_SUB + 1]
            flipped = _dot(rev, blk[src * F_SUB:(src + 1) * F_SUB])
            first = lax.broadcasted_iota(jnp.int32, flipped.shape, 0) == 0
            o_ref[1, u * F_SUB:(u + 1) * F_SUB, cols] = jnp.where(first, after.astype(F32), flipped).astype(BF16)


def _fpos(cn, sn, yc, ys, bsz, n_lat, tile_rows):
    n_half = n_lat // 2
    nk = n_half // tile_rows
    scale = float((n_lat * C_GROUP_DIM) ** -0.5)
    tile = pl.BlockSpec((tile_rows, n_lat), lambda b, k: (k, 0))
    nxt = pl.BlockSpec((BF16_SUBLANES, n_lat), lambda b, k: ((k + 1) * (tile_rows // BF16_SUBLANES), 0))
    return pl.pallas_call(
        functools.partial(_fpos_kernel, scale=scale),
        out_shape=jax.ShapeDtypeStruct((bsz, 2, n_half, F_COLS), BF16),
        grid=(bsz, nk),
        in_specs=[tile, nxt, tile, nxt,
                  pl.BlockSpec((n_lat, F_HALF + F_MID), lambda b, k: (b, 0)),
                  pl.BlockSpec((n_lat, F_HALF), lambda b, k: (b, 0))],
        out_specs=pl.BlockSpec((None, 2, tile_rows, F_COLS), lambda b, k: (b, 0, k, 0)),
        compiler_params=_cparams(("parallel", "parallel")),
        name="fpos",
    )(cn, cn, sn, sn, yc, ys)


def _linres_kernel(*refs, n_z):
    z_refs, (w_ref, b_ref, x_ref, g1_ref, o_ref, wm_ref) = refs[:n_z], refs[n_z:]

    @pl.when(pl.program_id(0) == 0)
    def _():
        half = C_GROUP_DIM // 2
        rev = (lax.broadcasted_iota(jnp.int32, (half, half), 0)
               + lax.broadcasted_iota(jnp.int32, (half, half), 1) == half).astype(BF16)
        mid = []
        for g in range(C_GROUPS):
            base = g * C_GROUP_DIM
            wm_ref[g * half:(g + 1) * half, :] = w_ref[base:base + half, :].astype(BF16)
            upper = w_ref[base + half:base + C_GROUP_DIM, :].astype(BF16)
            wm_ref[F_HALF + g * half:F_HALF + (g + 1) * half, :] = _dot(rev, upper).astype(BF16)
            mid.append(w_ref[base + half:base + half + 1, :])
        mid.append(jnp.zeros((F_MID - C_GROUPS, D_MODEL), F32))
        wm_ref[2 * F_HALF:, :] = jnp.concatenate(mid, axis=0).astype(BF16)

    z = jnp.concatenate([z_ref[...] for z_ref in z_refs], axis=0)
    y = _dot(z, wm_ref[...]) + b_ref[...]
    o_ref[...] = x_ref[...] + g1_ref[...] * y


def _linres(z, z_rows, w, bias, x2, mod3, mod_row_fn, tm, n_lat):
    t_tok = x2.shape[0]
    tiles_per_batch = n_lat // tm
    n_z = tm // z_rows
    nk = z.shape[2] // z_rows

    def z_spec(r):
        def index(i):
            pos_tile = (i % tiles_per_batch) * n_z + r
            half, k = pos_tile // nk, pos_tile % nk
            return i // tiles_per_batch, half, jnp.where(half == 0, k, nk - 1 - k), 0
        return pl.BlockSpec((None, None, z_rows, F_COLS), index)

    tok = lambda width: pl.BlockSpec((tm, width), lambda i: (i, 0))
    return pl.pallas_call(
        functools.partial(_linres_kernel, n_z=n_z),
        out_shape=jax.ShapeDtypeStruct((t_tok, D_MODEL), F32),
        grid=(t_tok // tm,),
        in_specs=[z_spec(r) for r in range(n_z)] + [
            pl.BlockSpec((D_MODEL, D_MODEL), lambda i: (0, 0), pipeline_mode=pl.Buffered(1)),
            pl.BlockSpec((1, D_MODEL), lambda i: (0, 0)), tok(D_MODEL),
            pl.BlockSpec((None, 1, D_MODEL), lambda i: (mod_row_fn(i), 0, 2))],
        out_specs=tok(D_MODEL),
        scratch_shapes=[pltpu.VMEM((F_COLS, D_MODEL), BF16)],
        compiler_params=_cparams(("arbitrary",)),
        name="linres",
    )(*([z] * n_z), w, bias, x2, mod3)


def _rope_tables(n_lat):
    t = np.arange(n_lat)
    row = (t // GRID_W).astype(np.float32)
    col = (t % GRID_W).astype(np.float32)
    n_freq = HEAD_DIM // 4
    inv_freq = np.float32(ROPE_BASE) ** (-np.arange(n_freq, dtype=np.float32) / np.float32(n_freq))
    ang_r, ang_c = row[:, None] * inv_freq, col[:, None] * inv_freq
    cr, sr, cc, sc = np.cos(ang_r), np.sin(ang_r), np.cos(ang_c), np.sin(ang_c)
    zero = np.zeros_like(sr)
    cos = np.concatenate([cr, cr, cc, cc], axis=-1)
    sin_a = np.concatenate([-sr, zero, -sc, zero], axis=-1)
    sin_b = np.concatenate([zero, sr, zero, sc], axis=-1)
    return tuple(jnp.asarray(tab, dtype=F32) for tab in (cos, sin_a, sin_b))


def _dft_tables(n_lat):
    def cs(n):
        idx = np.arange(n, dtype=np.int64)
        ang = 2.0 * np.pi * ((idx[:, None] * idx[None, :]) % n).astype(np.float64) / n
        return np.cos(ang), np.sin(ang)

    half = C_GROUP_DIM // 2
    cm, sm = cs(C_GROUP_DIM)
    cn, sn = cs(n_lat)
    t_ch = np.concatenate([cm[:, :half], sm[:, :half]], axis=1)
    t_mid = np.zeros((D_MODEL, F_MID))
    for g in range(C_GROUPS):
        t_mid[g * C_GROUP_DIM:(g + 1) * C_GROUP_DIM, g] = cm[:, half]
    as_bf16 = lambda t: jnp.asarray(t, dtype=F32).astype(BF16)
    return as_bf16(t_ch), as_bf16(t_mid), as_bf16(cn), as_bf16(sn)


def kernel(x, c, ctx, c_ctx, ada_w, ada_b, norm_mix, norm_mlp, mlp_w1, mlp_w2, ab_w_in, ab_q_norm, ab_k_norm,
           ab_sink, ab_gk_f, ab_gk_f_bias, ab_gk_b, ab_gk_b_bias, ab_gla_norm, ab_w_out, c_w_out, c_b_out):
    bsz, n_lat, _ = x.shape
    n_ctx = ctx.shape[1]
    depth = ada_w.shape[0]
    assert depth == 2 and bsz < MOD_ROWS
    t_tok = bsz * n_lat

    cvec = jnp.concatenate([c, c_ctx[None, :], jnp.zeros((MOD_ROWS - bsz - 1, D_MODEL), F32)], axis=0)
    mod = _ada(cvec, ada_w, ada_b)
    mod3 = mod.reshape(depth * MOD_ROWS, 1, 6 * D_MODEL)

    x2 = x.reshape(t_tok, D_MODEL)

    tm = min(256, n_lat)
    tpb = n_lat // tm
    lat_row0 = lambda i: i // tpb
    w_in = ab_w_in[0].astype(BF16)
    gn_mix0 = norm_mix[0].reshape(1, D_MODEL)
    qn = ab_q_norm[0].reshape(1, HEAD_DIM)
    kn = ab_k_norm[0].reshape(1, HEAD_DIM)
    gkf = jnp.pad(ab_gk_f[0], ((0, LR_COLS - B_GATE_RANK), (0, 0))).astype(BF16)
    gkb = jnp.pad(ab_gk_b[0], ((LR_COLS - B_GATE_RANK, 0), (0, 0))).astype(BF16)
    gkfb = ab_gk_f_bias[0].reshape(1, B_QK_W)
    gkbb = ab_gk_b_bias[0].reshape(1, B_QK_W)
    steps = t_tok // tm
    mlp_rows = depth * D_MODEL * D_FF // steps
    ride_along = (depth * D_MODEL) % (steps * BF16_SUBLANES) == 0 and mlp_rows * 4 <= CAST_BLOCK_BYTES
    cast = (mlp_w1.reshape(depth * D_MODEL, D_FF), mlp_w2.reshape(depth * D_FF, D_MODEL)) if ride_along else ()
    q, k, v, bq, bk, bv, gate, gf, gb, *cast_out = _proj(
        x2, gn_mix0, mod3, lat_row0, tpb, tm, w_in, qn, kn, _rope_tables(n_lat), gkf, gkfb, gkb, gkbb, True, cast)
    if ride_along:
        w1 = cast_out[0].reshape(depth, D_MODEL, D_FF)
        w2 = cast_out[1].reshape(depth, D_FF, D_MODEL)
    else:
        w1, w2 = mlp_w1.astype(BF16), mlp_w2.astype(BF16)
    tmc = min(256, n_ctx)
    kc, vc, bkc, bvc, gfc, gbc = _proj(
        ctx.reshape(bsz * n_ctx, D_MODEL), gn_mix0, mod3, lambda i: bsz, n_ctx // tmc, tmc, w_in, None, kn, None,
        gkf, gkfb, gkb, gkbb, False)

    oa = _attn(ab_sink[0], q, k, v, kc, vc, bsz, n_lat, n_ctx)
    ob = _gla(bq, bk, bv, gf, gb, bkc, bvc, gfc, gbc, bsz, n_lat, n_ctx)

    w_out = ab_w_out[0].astype(BF16)
    tmo = min(512, n_lat)
    tpbo = n_lat // tmo
    x2 = _mixout(oa, ob, gate, ab_gla_norm[0].reshape(1, B_DV), w_out[:A_Q_W], w_out[A_Q_W:], x2, mod3,
                 lambda i: i // tpbo, tmo)
    x2 = _mlp(x2, norm_mlp[0].reshape(1, D_MODEL), mod3, lambda i: i // tpbo, w1, w2, 0, tmo, 2048)

    t_ch, t_mid, cn, sn = _dft_tables(n_lat)
    row1 = lambda i: MOD_ROWS + i // tpbo
    yc, ys = _fch(x2, norm_mix[1].reshape(1, D_MODEL), mod3, row1, t_ch, t_mid, tmo)
    z_rows = min(tmo, n_lat // 2)
    z = _fpos(cn, sn, yc, ys, bsz, n_lat, z_rows)
    x2 = _linres(z, z_rows, c_w_out[0], c_b_out[0].reshape(1, D_MODEL), x2, mod3, row1, tmo, n_lat)
    x2 = _mlp(x2, norm_mlp[1].reshape(1, D_MODEL), mod3, row1, w1, w2, 1, tmo, 2048)
    return x2.reshape(bsz, n_lat, D_MODEL)
```

```python
import functools

import numpy as np
import jax
import jax.numpy as jnp
from jax import lax
from jax.experimental import pallas as pl
from jax.experimental.pallas import tpu as pltpu

D_MODEL = 2048
GRID_W = 64
EPS = 1e-6
NEG_INF = -1e30
HEAD_DIM = 128
A_Q_HEADS = 8
A_KV_HEADS = 2
A_GROUP = A_Q_HEADS // A_KV_HEADS
WINDOW = 128
BLOCK_Q = 128
LOG2_E = 1.4426950408889634
Q_SCALE = HEAD_DIM ** -0.5 * LOG2_E
ROPE_BASE = 10000.0
B_HEADS = 4
B_DV = 256
B_DK = 128
B_GATE_RANK = 16
B_GATE_NORM = 16.0
B_CHUNK = 64
GLA_GROUP = 4
C_GROUPS = 8
C_GROUP_DIM = D_MODEL // C_GROUPS
D_FF = 4 * D_MODEL

A_Q_W = A_Q_HEADS * HEAD_DIM
A_KV_W = A_KV_HEADS * HEAD_DIM
B_QK_W = B_HEADS * B_DK
B_V_W = B_HEADS * B_DV
OFF_AQ = 0
OFF_AK = OFF_AQ + A_Q_W
OFF_AV = OFF_AK + A_KV_W
OFF_BQ = OFF_AV + A_KV_W
OFF_BK = OFF_BQ + B_QK_W
OFF_BV = OFF_BK + B_QK_W
OFF_GATE = OFF_BV + B_V_W
OFF_LR = OFF_GATE + B_V_W
LR_COLS = 2 * B_GATE_RANK
IN_COLS = OFF_LR + LR_COLS
LANE = 128
BF16_SUBLANES = 16
MOD_ROWS = BF16_SUBLANES
VMEM_LIMIT = 56 * 1024 * 1024
CAST_BLOCK_BYTES = 2 * 1024 * 1024
MLP_COL_BLOCK = 1024
MLP_STEP_COLS = 2048
PROJ_ROWS = 256
TOKEN_ROWS = 512
BF16 = jnp.bfloat16
F32 = jnp.float32


def _cparams(sem):
    return pltpu.CompilerParams(dimension_semantics=sem, vmem_limit_bytes=VMEM_LIMIT)


def _dot(a, b):
    return jnp.dot(a, b, preferred_element_type=F32)


def _dot_nt(a, b):
    return lax.dot_general(a, b, (((1,), (1,)), ((), ())), preferred_element_type=F32)


def _dot_tn(a, b):
    return lax.dot_general(a, b, (((0,), (0,)), ((), ())), preferred_element_type=F32)


def _tri_sum(tri, g):
    g_hi = g.astype(BF16)
    g_lo = (g - g_hi.astype(F32)).astype(BF16)
    d = g.shape[1]
    both = _dot(tri, jnp.concatenate([g_hi, g_lo], axis=1))
    return both[:, :d] + both[:, d:]


def _rms(xf, gain):
    return xf * lax.rsqrt(jnp.mean(xf * xf, axis=-1, keepdims=True) + EPS) * gain


def _ada_kernel(c_ref, w_ref, b_ref, o_ref):
    cv = c_ref[...]
    s = (cv * jax.nn.sigmoid(cv)).astype(BF16)
    o_ref[...] = _dot(s, w_ref[...].astype(BF16)) + b_ref[...]


def _ada(cvec, ada_w, ada_b):
    depth = ada_w.shape[0]
    n = ada_w.shape[2]
    tn = 1024
    return pl.pallas_call(
        _ada_kernel,
        out_shape=jax.ShapeDtypeStruct((depth, MOD_ROWS, n), F32),
        grid=(depth, n // tn),
        in_specs=[
            pl.BlockSpec((MOD_ROWS, D_MODEL), lambda l, j: (0, 0)),
            pl.BlockSpec((None, D_MODEL, tn), lambda l, j: (l, 0, j)),
            pl.BlockSpec((None, 1, tn), lambda l, j: (l, 0, j)),
        ],
        out_specs=pl.BlockSpec((None, MOD_ROWS, tn), lambda l, j: (l, 0, j)),
        compiler_params=_cparams(("parallel", "parallel")),
        name="ada",
    )(cvec, ada_w, ada_b.reshape(depth, 1, n))


def _proj_kernel(*refs, latent, n_cast):
    for src_ref, dst_ref in zip(refs[:n_cast], refs[len(refs) - n_cast:]):
        dst_ref[...] = src_ref[...].astype(BF16)
    refs = refs[n_cast:len(refs) - n_cast]
    if latent:
        (x_ref, gn_ref, sh_ref, sc_ref, w_ref, qn_ref, kn_ref, cos_ref, sa_ref, sb_ref,
         gkf_ref, gkfb_ref, gkb_ref, gkbb_ref,
         q_ref, k_ref, v_ref, bq_ref, bk_ref, bv_ref, gate_ref, gf_ref, gb_ref) = refs
    else:
        (x_ref, gn_ref, sh_ref, sc_ref, w_ref, kn_ref,
         gkf_ref, gkfb_ref, gkb_ref, gkbb_ref,
         k_ref, v_ref, bk_ref, bv_ref, gf_ref, gb_ref) = refs

    xt = x_ref[...]
    h = _rms(xt, gn_ref[...]) * (1.0 + sc_ref[...]) + sh_ref[...]
    hb = h.astype(BF16)

    def proj(off, width):
        return _dot(hb, w_ref[:, off:off + width])

    def rope(t):
        return (t * cos_ref[...] + pltpu.roll(t, HEAD_DIM - HEAD_DIM // 4, 1) * sa_ref[...]
                + pltpu.roll(t, HEAD_DIM // 4, 1) * sb_ref[...])

    if latent:
        for half in range(2):
            seg = proj(OFF_AQ + half * (A_Q_W // 2), A_Q_W // 2)
            for j in range(A_Q_HEADS // 2):
                t = _rms(seg[:, j * HEAD_DIM:(j + 1) * HEAD_DIM], qn_ref[...])
                t = rope(t) * Q_SCALE
                hd = half * (A_Q_HEADS // 2) + j
                q_ref[:, hd * HEAD_DIM:(hd + 1) * HEAD_DIM] = t.astype(BF16)

    seg = proj(OFF_AK, 2 * A_KV_W)
    for j in range(A_KV_HEADS):
        t = _rms(seg[:, j * HEAD_DIM:(j + 1) * HEAD_DIM], kn_ref[...])
        if latent:
            t = rope(t)
        k_ref[:, j * HEAD_DIM:(j + 1) * HEAD_DIM] = t.astype(BF16)
    v_ref[...] = seg[:, A_KV_W:].astype(BF16)

    if latent:
        bq_ref[...] = proj(OFF_BQ, B_QK_W) * B_DK ** -0.5
    bk_ref[...] = proj(OFF_BK, B_QK_W)
    for half in range(2):
        bv_ref[:, half * (B_V_W // 2):(half + 1) * (B_V_W // 2)] = proj(OFF_BV + half * (B_V_W // 2), B_V_W // 2)
    if latent:
        for half in range(2):
            gate_ref[:, half * (B_V_W // 2):(half + 1) * (B_V_W // 2)] = proj(
                OFF_GATE + half * (B_V_W // 2), B_V_W // 2)

    lr = proj(OFF_LR, LR_COLS).astype(BF16)

    def log_decay(gk_ref, gkb_ref, out_ref):
        z = _dot(lr, gk_ref[...]) + gkb_ref[...]
        out_ref[...] = (jnp.minimum(z, 0.0) - jnp.log1p(jnp.exp(-jnp.abs(z)))) / B_GATE_NORM

    log_decay(gkf_ref, gkfb_ref, gf_ref)
    log_decay(gkb_ref, gkbb_ref, gb_ref)


def _proj(x2, gn, mod3, mod_row_fn, tiles_per_batch, tm, w_in, qn, kn, rope_tabs, gkf, gkfb, gkb, gkbb, latent,
          cast=()):
    t_tok = x2.shape[0]
    steps = t_tok // tm
    row = lambda width: pl.BlockSpec((1, width), lambda i: (0, 0))
    tok = lambda width: pl.BlockSpec((tm, width), lambda i: (i, 0))
    mod = lambda col: pl.BlockSpec((None, 1, D_MODEL), lambda i: (mod_row_fn(i), 0, col))
    in_specs = [tok(D_MODEL), row(D_MODEL), mod(0), mod(1),
                pl.BlockSpec((D_MODEL, IN_COLS), lambda i: (0, 0), pipeline_mode=pl.Buffered(1))]
    args = [x2, gn, mod3, mod3, w_in]
    if latent:
        in_specs += [row(HEAD_DIM), row(HEAD_DIM)]
        args += [qn, kn]
        in_specs += [pl.BlockSpec((tm, HEAD_DIM), lambda i: (i % tiles_per_batch, 0))] * 3
        args += list(rope_tabs)
    else:
        in_specs += [row(HEAD_DIM)]
        args += [kn]
    in_specs += [pl.BlockSpec((LR_COLS, B_QK_W), lambda i: (0, 0)), row(B_QK_W)] * 2
    args += [gkf, gkfb, gkb, gkbb]

    def out(width, dtype):
        return jax.ShapeDtypeStruct((t_tok, width), dtype), tok(width)

    if latent:
        outs = [out(A_Q_W, BF16), out(A_KV_W, BF16), out(A_KV_W, BF16), out(B_QK_W, F32), out(B_QK_W, F32),
                out(B_V_W, F32), out(B_V_W, F32), out(B_QK_W, F32), out(B_QK_W, F32)]
    else:
        outs = [out(A_KV_W, BF16), out(A_KV_W, BF16), out(B_QK_W, F32), out(B_V_W, F32),
                out(B_QK_W, F32), out(B_QK_W, F32)]
    cast_specs = [pl.BlockSpec((a.shape[0] // steps, a.shape[1]), lambda i: (i, 0)) for a in cast]
    outs += [(jax.ShapeDtypeStruct(a.shape, BF16), spec) for a, spec in zip(cast, cast_specs)]
    return pl.pallas_call(
        functools.partial(_proj_kernel, latent=latent, n_cast=len(cast)),
        out_shape=[o[0] for o in outs],
        grid=(steps,),
        in_specs=cast_specs + in_specs,
        out_specs=[o[1] for o in outs],
        compiler_params=_cparams(("parallel",)),
        name="proj_lat" if latent else "proj_ctx",
    )(*cast, *args)


def _attn_kernel(sink_ref, q_ref, k_ref, v_ref, kc_ref, vc_ref, o_ref, *, n_lat):
    kvh = pl.program_id(1)
    n_win = 3 * BLOCK_Q
    heads = range(A_GROUP)
    kc = kc_ref[...]
    vc = vc_ref[...]
    sk = jnp.concatenate(
        [jnp.full((1, BLOCK_Q), sink_ref[kvh * A_GROUP + g] * LOG2_E, F32) for g in heads], axis=1)
    key_minus_query = (lax.broadcasted_iota(jnp.int32, (n_win, BLOCK_Q), 0)
                       - lax.broadcasted_iota(jnp.int32, (n_win, BLOCK_Q), 1))

    def scores(n):
        start = jnp.clip(n * BLOCK_Q - BLOCK_Q, 0, n_lat - n_win)
        start = pl.multiple_of(start, BLOCK_Q)
        rows = pl.ds(pl.multiple_of(n * BLOCK_Q, BLOCK_Q), BLOCK_Q)
        kw = k_ref[pl.ds(start, n_win), :]
        valid = jnp.abs(key_minus_query + (start - n * BLOCK_Q)) <= WINDOW
        q = jnp.concatenate([q_ref[rows, g * HEAD_DIM:(g + 1) * HEAD_DIM] for g in heads], axis=0)
        s_lat = _dot_nt(kw, q)
        s_lat = jnp.concatenate(
            [jnp.where(valid, s_lat[:, g * BLOCK_Q:(g + 1) * BLOCK_Q], NEG_INF) for g in heads], axis=1)
        s_ctx = _dot_nt(kc, q)
        return start, rows, s_lat, s_ctx

    def softmax(s_lat, s_ctx):
        m = jnp.maximum(jnp.maximum(jnp.max(s_lat, axis=0, keepdims=True),
                                    jnp.max(s_ctx, axis=0, keepdims=True)), sk)
        p_lat = jnp.exp2(s_lat - m)
        p_ctx = jnp.exp2(s_ctx - m)
        den = (jnp.sum(p_lat, axis=0, keepdims=True) + jnp.sum(p_ctx, axis=0, keepdims=True)
               + jnp.exp2(sk - m))
        return p_lat.astype(BF16), p_ctx.astype(BF16), 1.0 / den

    def weighted_values(start, rows, p_lat, p_ctx, inv_den):
        vw = v_ref[pl.ds(start, n_win), :]
        o_t = (_dot_tn(vw, p_lat) + _dot_tn(vc, p_ctx)) * inv_den
        for g in heads:
            o_ref[rows, g * HEAD_DIM:(g + 1) * HEAD_DIM] = o_t[:, g * BLOCK_Q:(g + 1) * BLOCK_Q].T.astype(BF16)

    nb = n_lat // BLOCK_Q
    per_step = 8 if nb % 8 == 0 else (4 if nb % 4 == 0 else 1)

    def step(i, carry):
        sc = [scores(i * per_step + u) for u in range(per_step)]
        pr = [softmax(s_lat, s_ctx) for _, _, s_lat, s_ctx in sc]
        for (start, rows, _, _), (p_lat, p_ctx, inv_den) in zip(sc, pr):
            weighted_values(start, rows, p_lat, p_ctx, inv_den)
        return carry

    lax.fori_loop(0, nb // per_step, step, 0)


def _attn(sink, q, k, v, kc, vc, bsz, n_lat, n_ctx):
    gw = A_GROUP * HEAD_DIM
    blk = lambda rows, width: pl.BlockSpec((rows, width), lambda b, h: (b, h))
    return pl.pallas_call(
        functools.partial(_attn_kernel, n_lat=n_lat),
        out_shape=jax.ShapeDtypeStruct((bsz * n_lat, A_Q_W), BF16),
        grid=(bsz, A_KV_HEADS),
        in_specs=[pl.BlockSpec(memory_space=pltpu.SMEM), blk(n_lat, gw), blk(n_lat, HEAD_DIM),
                  blk(n_lat, HEAD_DIM), blk(n_ctx, HEAD_DIM), blk(n_ctx, HEAD_DIM)],
        out_specs=blk(n_lat, gw),
        compiler_params=_cparams(("parallel", "parallel")),
        name="attn",
    )(sink, q, k, v, kc, vc)


def _gla_kernel(q_ref, k_ref, v_ref, gf_ref, gb_ref, kc_ref, vc_ref, gfc_ref, gbc_ref, o_ref,
                sf_ref, sb_ref, ob_ref, *, n_lat, n_ctx):
    c = B_CHUNK

    def tri(nn, fn):
        r = lax.broadcasted_iota(jnp.int32, (nn, nn), 0)
        s = lax.broadcasted_iota(jnp.int32, (nn, nn), 1)
        return fn(r, s)

    kc = kc_ref[...]
    vcb = vc_ref[...].astype(BF16)
    up_strict = tri(n_ctx, lambda r, s: s > r).astype(BF16)
    lo_strict = tri(n_ctx, lambda r, s: s < r).astype(BF16)
    kd_f = (kc * jnp.exp(_tri_sum(up_strict, gfc_ref[...]))).astype(BF16)
    kd_b = (kc * jnp.exp(_tri_sum(lo_strict, gbc_ref[...]))).astype(BF16)
    sf_ref[...] = _dot_tn(vcb, kd_f)
    sb_ref[...] = _dot_tn(vcb, kd_b)

    r = c * GLA_GROUP
    ng = n_lat // r
    same = tri(r, lambda i, j: (i // c) == (j // c))
    lo_mask = jnp.logical_and(same, tri(r, lambda i, j: j <= i))
    up_mask = jnp.logical_and(same, tri(r, lambda i, j: j >= i))
    lo_incl = lo_mask.astype(BF16)
    up_incl = up_mask.astype(BF16)
    row_chunk = lax.broadcasted_iota(jnp.int32, (r, B_DK), 0) // c

    per_dir = 8 if ng % 8 == 0 else (4 if ng % 4 == 0 else (2 if ng % 2 == 0 else 1))

    def decay_sums(rows, g_ref, tri_incl, last):
        b = _tri_sum(tri_incl, g_ref[rows, :])
        bl = jnp.concatenate(
            [jnp.broadcast_to(b[n * c + last:n * c + last + 1, :], (c, B_DK)) for n in range(GLA_GROUP)], axis=0)
        return b, bl

    def scaled_operands(rows, b, bl):
        q = q_ref[rows, :]
        k = k_ref[rows, :]
        qe = (q * jnp.exp(b)).astype(BF16)
        ke = (k * jnp.exp(-b)).astype(BF16)
        kd = k * jnp.exp(bl - b)
        kd_blocks = jnp.concatenate(
            [jnp.where(row_chunk == n, kd, 0.0).astype(BF16) for n in range(GLA_GROUP)], axis=1)
        return qe, ke, kd_blocks, jnp.exp(bl), v_ref[rows, :].astype(BF16)

    def body(i, carry):
        fwd = [pl.ds(pl.multiple_of((i * per_dir + u) * r, r), r) for u in range(per_dir)]
        bwd = [pl.ds(pl.multiple_of((ng - 1 - i * per_dir - u) * r, r), r) for u in range(per_dir)]
        streams = ([(rows, gf_ref, lo_incl, lo_mask, c - 1) for rows in fwd]
                   + [(rows, gb_ref, up_incl, up_mask, 0) for rows in bwd])
        sums = [decay_sums(rows, g_ref, tri_incl, last) for rows, g_ref, tri_incl, _, last in streams]
        ops = [scaled_operands(strm[0], b, bl) for strm, (b, bl) in zip(streams, sums)]
        a = [jnp.where(strm[3], _dot_nt(qe, ke), 0.0).astype(BF16)
             for strm, (qe, ke, _, _, _) in zip(streams, ops)]
        o_intra = [_dot(a_s, vb) for a_s, (_, _, _, _, vb) in zip(a, ops)]
        kv = [_dot_tn(vb, kd_blocks) for _, _, kd_blocks, _, vb in ops]
        o_inter = [[None] * GLA_GROUP for _ in streams]
        st_f, st_b = sf_ref[...], sb_ref[...]
        for u in range(per_dir):
            for step in range(GLA_GROUP):
                for s, n in ((u, step), (per_dir + u, GLA_GROUP - 1 - step)):
                    qe, _, _, dec, _ = ops[s]
                    st = st_f if s < per_dir else st_b
                    o_inter[s][n] = _dot_nt(qe[n * c:(n + 1) * c], st.astype(BF16))
                    st = st * dec[n * c:n * c + 1, :] + kv[s][:, n * B_DK:(n + 1) * B_DK]
                    if s < per_dir:
                        st_f = st
                    else:
                        st_b = st
        sf_ref[...] = st_f
        sb_ref[...] = st_b
        for s, (rows, *_) in enumerate(streams):
            out_ref = o_ref if s < per_dir else ob_ref
            out_ref[rows, :] = o_intra[s] + jnp.concatenate(o_inter[s], axis=0)
        return carry

    lax.fori_loop(0, ng // per_dir, body, 0)
    o_ref[...] += ob_ref[...]


def _gla(bq, bk, bv, gf, gb, bkc, bvc, gfc, gbc, bsz, n_lat, n_ctx):
    lat = lambda width: pl.BlockSpec((n_lat, width), lambda b, h: (b, h))
    cx = lambda width: pl.BlockSpec((n_ctx, width), lambda b, h: (b, h))
    return pl.pallas_call(
        functools.partial(_gla_kernel, n_lat=n_lat, n_ctx=n_ctx),
        out_shape=jax.ShapeDtypeStruct((bsz * n_lat, B_V_W), F32),
        grid=(bsz, B_HEADS),
        in_specs=[lat(B_DK), lat(B_DK), lat(B_DV), lat(B_DK), lat(B_DK),
                  cx(B_DK), cx(B_DV), cx(B_DK), cx(B_DK)],
        out_specs=lat(B_DV),
        scratch_shapes=[pltpu.VMEM((B_DV, B_DK), F32), pltpu.VMEM((B_DV, B_DK), F32),
                        pltpu.VMEM((n_lat, B_DV), F32)],
        compiler_params=_cparams(("parallel", "parallel")),
        name="gla",
    )(bq, bk, bv, gf, gb, bkc, bvc, gfc, gbc)


def _mixout_kernel(oa_ref, ob_ref, gate_ref, gn_ref, wa_ref, wb_ref, x_ref, g1_ref, o_ref):
    y = _dot(oa_ref[...], wa_ref[...])
    for hd in range(B_HEADS):
        cols = slice(hd * B_DV, (hd + 1) * B_DV)
        gt = gate_ref[:, cols]
        t = _rms(ob_ref[:, cols], gn_ref[...]) * (gt * jax.nn.sigmoid(gt))
        y = y + _dot(t.astype(BF16), wb_ref[cols, :])
    o_ref[...] = x_ref[...] + g1_ref[...] * y


def _mixout(oa, ob, gate, gla_norm, w_a, w_b, x2, mod3, mod_row_fn, tm):
    t_tok = x2.shape[0]
    tok = lambda width: pl.BlockSpec((tm, width), lambda i: (i, 0))
    const = lambda shape: pl.BlockSpec(shape, lambda i: (0, 0))
    return pl.pallas_call(
        _mixout_kernel,
        out_shape=jax.ShapeDtypeStruct((t_tok, D_MODEL), F32),
        grid=(t_tok // tm,),
        in_specs=[tok(A_Q_W), tok(B_V_W), tok(B_V_W), const((1, B_DV)),
                  const((A_Q_W, D_MODEL)), const((B_V_W, D_MODEL)), tok(D_MODEL),
                  pl.BlockSpec((None, 1, D_MODEL), lambda i: (mod_row_fn(i), 0, 2))],
        out_specs=tok(D_MODEL),
        compiler_params=_cparams(("parallel",)),
        name="mixout",
    )(oa, ob, gate, gla_norm, w_a, w_b, x2, mod3)


def _mlp_kernel(x_ref, gn_ref, sh_ref, sc_ref, g2_ref, w1_ref, w2_ref, o_ref, h_ref):
    j = pl.program_id(1)

    @pl.when(j == 0)
    def _():
        h = _rms(x_ref[...], gn_ref[...] * (1.0 + sc_ref[...])) + sh_ref[...]
        h_ref[...] = h.astype(BF16)

    def step(acc_ref):
        part = None
        for s in range(w1_ref.shape[1] // MLP_COL_BLOCK):
            cols = slice(s * MLP_COL_BLOCK, (s + 1) * MLP_COL_BLOCK)
            u = jnp.maximum(_dot(h_ref[...], w1_ref[:, cols]), 0.0)
            p = _dot((u * u).astype(BF16), w2_ref[cols, :])
            part = p if part is None else part + p
        o_ref[...] = acc_ref[...] + g2_ref[...] * part

    pl.when(j == 0)(functools.partial(step, x_ref))
    pl.when(j > 0)(functools.partial(step, o_ref))


def _mlp(x2, gn, mod3, mod_row_fn, w1, w2, layer, tm, tf):
    t_tok = x2.shape[0]
    mod = lambda col: pl.BlockSpec((None, 1, D_MODEL), lambda i, j: (mod_row_fn(i), 0, col))
    return pl.pallas_call(
        _mlp_kernel,
        out_shape=jax.ShapeDtypeStruct((t_tok, D_MODEL), F32),
        grid=(t_tok // tm, D_FF // tf),
        in_specs=[pl.BlockSpec((tm, D_MODEL), lambda i, j: (i, 0)),
                  pl.BlockSpec((1, D_MODEL), lambda i, j: (0, 0)),
                  mod(3), mod(4), mod(5),
                  pl.BlockSpec((None, D_MODEL, tf), lambda i, j: (layer, 0, j)),
                  pl.BlockSpec((None, tf, D_MODEL), lambda i, j: (layer, j, 0))],
        out_specs=pl.BlockSpec((tm, D_MODEL), lambda i, j: (i, 0)),
        scratch_shapes=[pltpu.VMEM((tm, D_MODEL), BF16)],
        compiler_params=_cparams(("parallel", "arbitrary")),
        name="mlp",
    )(x2, gn, mod3, mod3, mod3, w1, w2)


F_HALF = C_GROUPS * (C_GROUP_DIM // 2)
F_MID = LANE
F_COLS = 2 * F_HALF + F_MID


def _fch_kernel(x_ref, gn_ref, sh_ref, sc_ref, t_ref, mid_ref, yc_ref, ys_ref):
    h = _rms(x_ref[...], gn_ref[...] * (1.0 + sc_ref[...])) + sh_ref[...]
    hb = h.astype(BF16)
    half = C_GROUP_DIM // 2
    for g in range(C_GROUPS):
        y = _dot(hb[:, g * C_GROUP_DIM:(g + 1) * C_GROUP_DIM], t_ref[...])
        yc_ref[:, g * half:(g + 1) * half] = y[:, :half].astype(BF16)
        ys_ref[:, g * half:(g + 1) * half] = y[:, half:].astype(BF16)
    yc_ref[:, F_HALF:] = _dot(hb, mid_ref[...]).astype(BF16)


def _fch(x2, gn, mod3, mod_row_fn, t_ch, t_mid, tm):
    t_tok = x2.shape[0]
    mod = lambda col: pl.BlockSpec((None, 1, D_MODEL), lambda i: (mod_row_fn(i), 0, col))
    tok = lambda width: pl.BlockSpec((tm, width), lambda i: (i, 0))
    return pl.pallas_call(
        _fch_kernel,
        out_shape=[jax.ShapeDtypeStruct((t_tok, F_HALF + F_MID), BF16),
                   jax.ShapeDtypeStruct((t_tok, F_HALF), BF16)],
        grid=(t_tok // tm,),
        in_specs=[tok(D_MODEL), pl.BlockSpec((1, D_MODEL), lambda i: (0, 0)), mod(0), mod(1),
                  pl.BlockSpec((C_GROUP_DIM, C_GROUP_DIM), lambda i: (0, 0)),
                  pl.BlockSpec((D_MODEL, F_MID), lambda i: (0, 0))],
        out_specs=[tok(F_HALF + F_MID), tok(F_HALF)],
        compiler_params=_cparams(("parallel",)),
        name="fch",
    )(x2, gn, mod3, mod3, t_ch, t_mid)


F_SUB = 256


def _fpos_kernel(cn_ref, cnx_ref, sn_ref, snx_ref, yc_ref, ys_ref, o_ref, *, scale):
    yc = yc_ref[...]
    ys = ys_ref[...]

    def blocks(p, q):
        return (((p[:, :F_HALF] - q) * scale).astype(BF16), ((p[:, :F_HALF] + q) * scale).astype(BF16),
                (p[:, F_HALF:] * scale).astype(BF16))

    cur = blocks(_dot(cn_ref[...], yc), _dot(sn_ref[...], ys))
    nxt = blocks(_dot(cnx_ref[...], yc), _dot(snx_ref[...], ys))
    direct_cols = (slice(0, F_HALF), slice(F_HALF, 2 * F_HALF), slice(2 * F_HALF, F_COLS))
    mirror_cols = (direct_cols[1], direct_cols[0], direct_cols[2])
    for cols, blk in zip(direct_cols, cur):
        o_ref[0, :, cols] = blk
    rev = (lax.broadcasted_iota(jnp.int32, (F_SUB, F_SUB), 0)
           + lax.broadcasted_iota(jnp.int32, (F_SUB, F_SUB), 1) == F_SUB).astype(BF16)
    n_sub = cn_ref.shape[0] // F_SUB
    for u in range(n_sub):
        src = n_sub - 1 - u
        for cols, blk, blk_nxt in zip(mirror_cols, cur, nxt):
            after = blk_nxt[0:1] if src == n_sub - 1 else blk[(src + 1) * F_SUB:(src + 1) * F_SUB + 1]
            flipped = _dot(rev, blk[src * F_SUB:(src + 1) * F_SUB])
            first = lax.broadcasted_iota(jnp.int32, flipped.shape, 0) == 0
            o_ref[1, u * F_SUB:(u + 1) * F_SUB, cols] = jnp.where(first, after.astype(F32), flipped).astype(BF16)


def _fpos(cn, sn, yc, ys, bsz, n_lat, tile_rows):
    n_half = n_lat // 2
    nk = n_half // tile_rows
    scale = float((n_lat * C_GROUP_DIM) ** -0.5)
    tile = pl.BlockSpec((tile_rows, n_lat), lambda b, k: (k, 0))
    nxt = pl.BlockSpec((BF16_SUBLANES, n_lat), lambda b, k: ((k + 1) * (tile_rows // BF16_SUBLANES), 0))
    return pl.pallas_call(
        functools.partial(_fpos_kernel, scale=scale),
        out_shape=jax.ShapeDtypeStruct((bsz, 2, n_half, F_COLS), BF16),
        grid=(bsz, nk),
        in_specs=[tile, nxt, tile, nxt,
                  pl.BlockSpec((n_lat, F_HALF + F_MID), lambda b, k: (b, 0)),
                  pl.BlockSpec((n_lat, F_HALF), lambda b, k: (b, 0))],
        out_specs=pl.BlockSpec((None, 2, tile_rows, F_COLS), lambda b, k: (b, 0, k, 0)),
        compiler_params=_cparams(("parallel", "parallel")),
        name="fpos",
    )(cn, cn, sn, sn, yc, ys)


def _linres_kernel(*refs, n_z):
    z_refs, (w_ref, b_ref, x_ref, g1_ref, o_ref, wm_ref) = refs[:n_z], refs[n_z:]

    @pl.when(pl.program_id(0) == 0)
    def _():
        half = C_GROUP_DIM // 2
        rev = (lax.broadcasted_iota(jnp.int32, (half, half), 0)
               + lax.broadcasted_iota(jnp.int32, (half, half), 1) == half).astype(BF16)
        mid = []
        for g in range(C_GROUPS):
            base = g * C_GROUP_DIM
            wm_ref[g * half:(g + 1) * half, :] = w_ref[base:base + half, :].astype(BF16)
            upper = w_ref[base + half:base + C_GROUP_DIM, :].astype(BF16)
            wm_ref[F_HALF + g * half:F_HALF + (g + 1) * half, :] = _dot(rev, upper).astype(BF16)
            mid.append(w_ref[base + half:base + half + 1, :])
        mid.append(jnp.zeros((F_MID - C_GROUPS, D_MODEL), F32))
        wm_ref[2 * F_HALF:, :] = jnp.concatenate(mid, axis=0).astype(BF16)

    z = jnp.concatenate([z_ref[...] for z_ref in z_refs], axis=0)
    y = _dot(z, wm_ref[...]) + b_ref[...]
    o_ref[...] = x_ref[...] + g1_ref[...] * y


def _linres(z, z_rows, w, bias, x2, mod3, mod_row_fn, tm, n_lat):
    t_tok = x2.shape[0]
    tiles_per_batch = n_lat // tm
    n_z = tm // z_rows
    nk = z.shape[2] // z_rows

    def z_spec(r):
        def index(i):
            pos_tile = (i % tiles_per_batch) * n_z + r
            half, k = pos_tile // nk, pos_tile % nk
            return i // tiles_per_batch, half, jnp.where(half == 0, k, nk - 1 - k), 0
        return pl.BlockSpec((None, None, z_rows, F_COLS), index)

    tok = lambda width: pl.BlockSpec((tm, width), lambda i: (i, 0))
    return pl.pallas_call(
        functools.partial(_linres_kernel, n_z=n_z),
        out_shape=jax.ShapeDtypeStruct((t_tok, D_MODEL), F32),
        grid=(t_tok // tm,),
        in_specs=[z_spec(r) for r in range(n_z)] + [
            pl.BlockSpec((D_MODEL, D_MODEL), lambda i: (0, 0), pipeline_mode=pl.Buffered(1)),
            pl.BlockSpec((1, D_MODEL), lambda i: (0, 0)), tok(D_MODEL),
            pl.BlockSpec((None, 1, D_MODEL), lambda i: (mod_row_fn(i), 0, 2))],
        out_specs=tok(D_MODEL),
        scratch_shapes=[pltpu.VMEM((F_COLS, D_MODEL), BF16)],
        compiler_params=_cparams(("arbitrary",)),
        name="linres",
    )(*([z] * n_z), w, bias, x2, mod3)


def _rope_tables(n_lat):
    t = np.arange(n_lat)
    row = (t // GRID_W).astype(np.float32)
    col = (t % GRID_W).astype(np.float32)
    n_freq = HEAD_DIM // 4
    inv_freq = np.float32(ROPE_BASE) ** (-np.arange(n_freq, dtype=np.float32) / np.float32(n_freq))
    ang_r, ang_c = row[:, None] * inv_freq, col[:, None] * inv_freq
    cr, sr, cc, sc = np.cos(ang_r), np.sin(ang_r), np.cos(ang_c), np.sin(ang_c)
    zero = np.zeros_like(sr)
    cos = np.concatenate([cr, cr, cc, cc], axis=-1)
    sin_a = np.concatenate([-sr, zero, -sc, zero], axis=-1)
    sin_b = np.concatenate([zero, sr, zero, sc], axis=-1)
    return tuple(jnp.asarray(tab, dtype=F32) for tab in (cos, sin_a, sin_b))


def _dft_tables(n_lat):
    def cs(n):
        idx = np.arange(n, dtype=np.int64)
        ang = 2.0 * np.pi * ((idx[:, None] * idx[None, :]) % n).astype(np.float64) / n
        return np.cos(ang), np.sin(ang)

    half = C_GROUP_DIM // 2
    cm, sm = cs(C_GROUP_DIM)
    cn, sn = cs(n_lat)
    t_ch = np.concatenate([cm[:, :half], sm[:, :half]], axis=1)
    t_mid = np.zeros((D_MODEL, F_MID))
    for g in range(C_GROUPS):
        t_mid[g * C_GROUP_DIM:(g + 1) * C_GROUP_DIM, g] = cm[:, half]
    as_bf16 = lambda t: jnp.asarray(t, dtype=F32).astype(BF16)
    return as_bf16(t_ch), as_bf16(t_mid), as_bf16(cn), as_bf16(sn)


def kernel(x, c, ctx, c_ctx, ada_w, ada_b, norm_mix, norm_mlp, mlp_w1, mlp_w2, ab_w_in, ab_q_norm, ab_k_norm,
           ab_sink, ab_gk_f, ab_gk_f_bias, ab_gk_b, ab_gk_b_bias, ab_gla_norm, ab_w_out, c_w_out, c_b_out):
    bsz, n_lat, _ = x.shape
    n_ctx = ctx.shape[1]
    depth = ada_w.shape[0]
    assert depth == 2 and bsz < MOD_ROWS
    t_tok = bsz * n_lat

    cvec = jnp.concatenate([c, c_ctx[None, :], jnp.zeros((MOD_ROWS - bsz - 1, D_MODEL), F32)], axis=0)
    mod = _ada(cvec, ada_w, ada_b)
    mod3 = mod.reshape(depth * MOD_ROWS, 1, 6 * D_MODEL)

    x2 = x.reshape(t_tok, D_MODEL)

    tm = min(PROJ_ROWS, n_lat)
    tpb = n_lat // tm
    lat_row0 = lambda i: i // tpb
    w_in = ab_w_in[0].astype(BF16)
    gn_mix0 = norm_mix[0].reshape(1, D_MODEL)
    qn = ab_q_norm[0].reshape(1, HEAD_DIM)
    kn = ab_k_norm[0].reshape(1, HEAD_DIM)
    gkf = jnp.pad(ab_gk_f[0], ((0, LR_COLS - B_GATE_RANK), (0, 0))).astype(BF16)
    gkb = jnp.pad(ab_gk_b[0], ((LR_COLS - B_GATE_RANK, 0), (0, 0))).astype(BF16)
    gkfb = ab_gk_f_bias[0].reshape(1, B_QK_W)
    gkbb = ab_gk_b_bias[0].reshape(1, B_QK_W)
    steps = t_tok // tm
    mlp_rows = depth * D_MODEL * D_FF // steps
    ride_along = (depth * D_MODEL) % (steps * BF16_SUBLANES) == 0 and mlp_rows * 4 <= CAST_BLOCK_BYTES
    cast = (mlp_w1.reshape(depth * D_MODEL, D_FF), mlp_w2.reshape(depth * D_FF, D_MODEL)) if ride_along else ()
    q, k, v, bq, bk, bv, gate, gf, gb, *cast_out = _proj(
        x2, gn_mix0, mod3, lat_row0, tpb, tm, w_in, qn, kn, _rope_tables(n_lat), gkf, gkfb, gkb, gkbb, True, cast)
    if ride_along:
        w1 = cast_out[0].reshape(depth, D_MODEL, D_FF)
        w2 = cast_out[1].reshape(depth, D_FF, D_MODEL)
    else:
        w1, w2 = mlp_w1.astype(BF16), mlp_w2.astype(BF16)
    tmc = min(PROJ_ROWS, n_ctx)
    kc, vc, bkc, bvc, gfc, gbc = _proj(
        ctx.reshape(bsz * n_ctx, D_MODEL), gn_mix0, mod3, lambda i: bsz, n_ctx // tmc, tmc, w_in, None, kn, None,
        gkf, gkfb, gkb, gkbb, False)

    oa = _attn(ab_sink[0], q, k, v, kc, vc, bsz, n_lat, n_ctx)
    ob = _gla(bq, bk, bv, gf, gb, bkc, bvc, gfc, gbc, bsz, n_lat, n_ctx)

    w_out = ab_w_out[0].astype(BF16)
    tmo = min(TOKEN_ROWS, n_lat)
    tpbo = n_lat // tmo
    x2 = _mixout(oa, ob, gate, ab_gla_norm[0].reshape(1, B_DV), w_out[:A_Q_W], w_out[A_Q_W:], x2, mod3,
                 lambda i: i // tpbo, tmo)
    x2 = _mlp(x2, norm_mlp[0].reshape(1, D_MODEL), mod3, lambda i: i // tpbo, w1, w2, 0, tmo, MLP_STEP_COLS)

    t_ch, t_mid, cn, sn = _dft_tables(n_lat)
    row1 = lambda i: MOD_ROWS + i // tpbo
    yc, ys = _fch(x2, norm_mix[1].reshape(1, D_MODEL), mod3, row1, t_ch, t_mid, tmo)
    z_rows = min(tmo, n_lat // 2)
    z = _fpos(cn, sn, yc, ys, bsz, n_lat, z_rows)
    x2 = _linres(z, z_rows, c_w_out[0], c_b_out[0].reshape(1, D_MODEL), x2, mod3, row1, tmo, n_lat)
    x2 = _mlp(x2, norm_mlp[1].reshape(1, D_MODEL), mod3, row1, w1, w2, 1, tmo, MLP_STEP_COLS)
    return x2.reshape(bsz, n_lat, D_MODEL)
```

```python
import functools

import numpy as np
import jax
import jax.numpy as jnp
from jax import lax
from jax.experimental import pallas as pl
from jax.experimental.pallas import tpu as pltpu

D_MODEL = 2048
GRID_W = 64
EPS = 1e-6
NEG_INF = -1e30
HEAD_DIM = 128
A_Q_HEADS = 8
A_KV_HEADS = 2
A_GROUP = A_Q_HEADS // A_KV_HEADS
WINDOW = 128
BLOCK_Q = 128
LOG2_E = 1.4426950408889634
Q_SCALE = HEAD_DIM ** -0.5 * LOG2_E
ROPE_BASE = 10000.0
B_HEADS = 4
B_DV = 256
B_DK = 128
B_GATE_RANK = 16
B_GATE_NORM = 16.0
B_CHUNK = 64
GLA_GROUP = 4
C_GROUPS = 8
C_GROUP_DIM = D_MODEL // C_GROUPS
D_FF = 4 * D_MODEL

A_Q_W = A_Q_HEADS * HEAD_DIM
A_KV_W = A_KV_HEADS * HEAD_DIM
B_QK_W = B_HEADS * B_DK
B_V_W = B_HEADS * B_DV
OFF_AQ = 0
OFF_AK = OFF_AQ + A_Q_W
OFF_AV = OFF_AK + A_KV_W
OFF_BQ = OFF_AV + A_KV_W
OFF_BK = OFF_BQ + B_QK_W
OFF_BV = OFF_BK + B_QK_W
OFF_GATE = OFF_BV + B_V_W
OFF_LR = OFF_GATE + B_V_W
LR_COLS = 2 * B_GATE_RANK
IN_COLS = OFF_LR + LR_COLS
LANE = 128
BF16_SUBLANES = 16
MOD_ROWS = BF16_SUBLANES
VMEM_LIMIT = 56 * 1024 * 1024
CAST_BLOCK_BYTES = 2 * 1024 * 1024
MLP_COL_BLOCK = 1024
MLP_STEP_COLS = 2048
PROJ_ROWS = 256
TOKEN_ROWS = 512
BF16 = jnp.bfloat16
F32 = jnp.float32


def _cparams(sem):
    return pltpu.CompilerParams(dimension_semantics=sem, vmem_limit_bytes=VMEM_LIMIT)


def _dot(a, b):
    return jnp.dot(a, b, preferred_element_type=F32)


def _dot_nt(a, b):
    return lax.dot_general(a, b, (((1,), (1,)), ((), ())), preferred_element_type=F32)


def _dot_tn(a, b):
    return lax.dot_general(a, b, (((0,), (0,)), ((), ())), preferred_element_type=F32)


def _tri_sum(tri, g):
    g_hi = g.astype(BF16)
    g_lo = (g - g_hi.astype(F32)).astype(BF16)
    d = g.shape[1]
    both = _dot(tri, jnp.concatenate([g_hi, g_lo], axis=1))
    return both[:, :d] + both[:, d:]


def _rms(xf, gain):
    return xf * lax.rsqrt(jnp.mean(xf * xf, axis=-1, keepdims=True) + EPS) * gain


def _ada_kernel(c_ref, w_ref, b_ref, o_ref):
    cv = c_ref[...]
    s = (cv * jax.nn.sigmoid(cv)).astype(BF16)
    o_ref[...] = _dot(s, w_ref[...].astype(BF16)) + b_ref[...]


def _ada(cvec, ada_w, ada_b):
    depth = ada_w.shape[0]
    n = ada_w.shape[2]
    tn = 1024
    return pl.pallas_call(
        _ada_kernel,
        out_shape=jax.ShapeDtypeStruct((depth, MOD_ROWS, n), F32),
        grid=(depth, n // tn),
        in_specs=[
            pl.BlockSpec((MOD_ROWS, D_MODEL), lambda l, j: (0, 0)),
            pl.BlockSpec((None, D_MODEL, tn), lambda l, j: (l, 0, j)),
            pl.BlockSpec((None, 1, tn), lambda l, j: (l, 0, j)),
        ],
        out_specs=pl.BlockSpec((None, MOD_ROWS, tn), lambda l, j: (l, 0, j)),
        compiler_params=_cparams(("parallel", "parallel")),
        name="ada",
    )(cvec, ada_w, ada_b.reshape(depth, 1, n))


def _proj_kernel(*refs, latent, n_cast):
    for src_ref, dst_ref in zip(refs[:n_cast], refs[len(refs) - n_cast:]):
        dst_ref[...] = src_ref[...].astype(BF16)
    refs = refs[n_cast:len(refs) - n_cast]
    if latent:
        (x_ref, gn_ref, sh_ref, sc_ref, w_ref, qn_ref, kn_ref, cos_ref, sa_ref, sb_ref,
         gkf_ref, gkfb_ref, gkb_ref, gkbb_ref,
         q_ref, k_ref, v_ref, bq_ref, bk_ref, bv_ref, gate_ref, gf_ref, gb_ref) = refs
    else:
        (x_ref, gn_ref, sh_ref, sc_ref, w_ref, kn_ref,
         gkf_ref, gkfb_ref, gkb_ref, gkbb_ref,
         k_ref, v_ref, bk_ref, bv_ref, gf_ref, gb_ref) = refs

    xt = x_ref[...]
    h = _rms(xt, gn_ref[...]) * (1.0 + sc_ref[...]) + sh_ref[...]
    hb = h.astype(BF16)

    def proj(off, width):
        return _dot(hb, w_ref[:, off:off + width])

    def rope(t):
        return (t * cos_ref[...] + pltpu.roll(t, HEAD_DIM - HEAD_DIM // 4, 1) * sa_ref[...]
                + pltpu.roll(t, HEAD_DIM // 4, 1) * sb_ref[...])

    if latent:
        for half in range(2):
            seg = proj(OFF_AQ + half * (A_Q_W // 2), A_Q_W // 2)
            for j in range(A_Q_HEADS // 2):
                t = _rms(seg[:, j * HEAD_DIM:(j + 1) * HEAD_DIM], qn_ref[...])
                t = rope(t) * Q_SCALE
                hd = half * (A_Q_HEADS // 2) + j
                q_ref[:, hd * HEAD_DIM:(hd + 1) * HEAD_DIM] = t.astype(BF16)

    seg = proj(OFF_AK, 2 * A_KV_W)
    for j in range(A_KV_HEADS):
        t = _rms(seg[:, j * HEAD_DIM:(j + 1) * HEAD_DIM], kn_ref[...])
        if latent:
            t = rope(t)
        k_ref[:, j * HEAD_DIM:(j + 1) * HEAD_DIM] = t.astype(BF16)
    v_ref[...] = seg[:, A_KV_W:].astype(BF16)

    if latent:
        bq_ref[...] = proj(OFF_BQ, B_QK_W) * B_DK ** -0.5
    bk_ref[...] = proj(OFF_BK, B_QK_W)
    for half in range(2):
        bv_ref[:, half * (B_V_W // 2):(half + 1) * (B_V_W // 2)] = proj(OFF_BV + half * (B_V_W // 2), B_V_W // 2)
    if latent:
        for half in range(2):
            gate_ref[:, half * (B_V_W // 2):(half + 1) * (B_V_W // 2)] = proj(
                OFF_GATE + half * (B_V_W // 2), B_V_W // 2)

    lr = proj(OFF_LR, LR_COLS).astype(BF16)

    def log_decay(gk_ref, gkb_ref, out_ref):
        z = _dot(lr, gk_ref[...]) + gkb_ref[...]
        out_ref[...] = (jnp.minimum(z, 0.0) - jnp.log1p(jnp.exp(-jnp.abs(z)))) / B_GATE_NORM

    log_decay(gkf_ref, gkfb_ref, gf_ref)
    log_decay(gkb_ref, gkbb_ref, gb_ref)


def _proj(x2, gn, mod3, mod_row_fn, tiles_per_batch, tm, w_in, qn, kn, rope_tabs, gkf, gkfb, gkb, gkbb, latent,
          cast=()):
    t_tok = x2.shape[0]
    steps = t_tok // tm
    row = lambda width: pl.BlockSpec((1, width), lambda i: (0, 0))
    tok = lambda width: pl.BlockSpec((tm, width), lambda i: (i, 0))
    mod = lambda col: pl.BlockSpec((None, 1, D_MODEL), lambda i: (mod_row_fn(i), 0, col))
    in_specs = [tok(D_MODEL), row(D_MODEL), mod(0), mod(1),
                pl.BlockSpec((D_MODEL, IN_COLS), lambda i: (0, 0), pipeline_mode=pl.Buffered(1))]
    args = [x2, gn, mod3, mod3, w_in]
    if latent:
        in_specs += [row(HEAD_DIM), row(HEAD_DIM)]
        args += [qn, kn]
        in_specs += [pl.BlockSpec((tm, HEAD_DIM), lambda i: (i % tiles_per_batch, 0))] * 3
        args += list(rope_tabs)
    else:
        in_specs += [row(HEAD_DIM)]
        args += [kn]
    in_specs += [pl.BlockSpec((LR_COLS, B_QK_W), lambda i: (0, 0)), row(B_QK_W)] * 2
    args += [gkf, gkfb, gkb, gkbb]

    def out(width, dtype):
        return jax.ShapeDtypeStruct((t_tok, width), dtype), tok(width)

    if latent:
        outs = [out(A_Q_W, BF16), out(A_KV_W, BF16), out(A_KV_W, BF16), out(B_QK_W, F32), out(B_QK_W, F32),
                out(B_V_W, F32), out(B_V_W, F32), out(B_QK_W, F32), out(B_QK_W, F32)]
    else:
        outs = [out(A_KV_W, BF16), out(A_KV_W, BF16), out(B_QK_W, F32), out(B_V_W, F32),
                out(B_QK_W, F32), out(B_QK_W, F32)]
    cast_specs = [pl.BlockSpec((a.shape[0] // steps, a.shape[1]), lambda i: (i, 0)) for a in cast]
    outs += [(jax.ShapeDtypeStruct(a.shape, BF16), spec) for a, spec in zip(cast, cast_specs)]
    return pl.pallas_call(
        functools.partial(_proj_kernel, latent=latent, n_cast=len(cast)),
        out_shape=[o[0] for o in outs],
        grid=(steps,),
        in_specs=cast_specs + in_specs,
        out_specs=[o[1] for o in outs],
        compiler_params=_cparams(("parallel",)),
        name="proj_lat" if latent else "proj_ctx",
    )(*cast, *args)


def _attn_kernel(sink_ref, q_ref, k_ref, v_ref, kc_ref, vc_ref, o_ref, *, n_lat):
    kvh = pl.program_id(1)
    n_win = 3 * BLOCK_Q
    heads = range(A_GROUP)
    kc = kc_ref[...]
    vc = vc_ref[...]
    sk = jnp.concatenate(
        [jnp.full((1, BLOCK_Q), sink_ref[kvh * A_GROUP + g] * LOG2_E, F32) for g in heads], axis=1)
    key_minus_query = (lax.broadcasted_iota(jnp.int32, (n_win, BLOCK_Q), 0)
                       - lax.broadcasted_iota(jnp.int32, (n_win, BLOCK_Q), 1))

    def scores(n):
        start = jnp.clip(n * BLOCK_Q - BLOCK_Q, 0, n_lat - n_win)
        start = pl.multiple_of(start, BLOCK_Q)
        rows = pl.ds(pl.multiple_of(n * BLOCK_Q, BLOCK_Q), BLOCK_Q)
        kw = k_ref[pl.ds(start, n_win), :]
        valid = jnp.abs(key_minus_query + (start - n * BLOCK_Q)) <= WINDOW
        q = jnp.concatenate([q_ref[rows, g * HEAD_DIM:(g + 1) * HEAD_DIM] for g in heads], axis=0)
        s_lat = _dot_nt(kw, q)
        s_lat = jnp.concatenate(
            [jnp.where(valid, s_lat[:, g * BLOCK_Q:(g + 1) * BLOCK_Q], NEG_INF) for g in heads], axis=1)
        s_ctx = _dot_nt(kc, q)
        return start, rows, s_lat, s_ctx

    def softmax(s_lat, s_ctx):
        m = jnp.maximum(jnp.maximum(jnp.max(s_lat, axis=0, keepdims=True),
                                    jnp.max(s_ctx, axis=0, keepdims=True)), sk)
        p_lat = jnp.exp2(s_lat - m)
        p_ctx = jnp.exp2(s_ctx - m)
        den = (jnp.sum(p_lat, axis=0, keepdims=True) + jnp.sum(p_ctx, axis=0, keepdims=True)
               + jnp.exp2(sk - m))
        return p_lat.astype(BF16), p_ctx.astype(BF16), 1.0 / den

    def weighted_values(start, rows, p_lat, p_ctx, inv_den):
        vw = v_ref[pl.ds(start, n_win), :]
        o_t = (_dot_tn(vw, p_lat) + _dot_tn(vc, p_ctx)) * inv_den
        for g in heads:
            o_ref[rows, g * HEAD_DIM:(g + 1) * HEAD_DIM] = o_t[:, g * BLOCK_Q:(g + 1) * BLOCK_Q].T.astype(BF16)

    nb = n_lat // BLOCK_Q
    per_step = 16 if nb % 16 == 0 else (4 if nb % 4 == 0 else 1)

    def step(i, carry):
        sc = [scores(i * per_step + u) for u in range(per_step)]
        pr = [softmax(s_lat, s_ctx) for _, _, s_lat, s_ctx in sc]
        for (start, rows, _, _), (p_lat, p_ctx, inv_den) in zip(sc, pr):
            weighted_values(start, rows, p_lat, p_ctx, inv_den)
        return carry

    lax.fori_loop(0, nb // per_step, step, 0)


def _attn(sink, q, k, v, kc, vc, bsz, n_lat, n_ctx):
    gw = A_GROUP * HEAD_DIM
    blk = lambda rows, width: pl.BlockSpec((rows, width), lambda b, h: (b, h))
    return pl.pallas_call(
        functools.partial(_attn_kernel, n_lat=n_lat),
        out_shape=jax.ShapeDtypeStruct((bsz * n_lat, A_Q_W), BF16),
        grid=(bsz, A_KV_HEADS),
        in_specs=[pl.BlockSpec(memory_space=pltpu.SMEM), blk(n_lat, gw), blk(n_lat, HEAD_DIM),
                  blk(n_lat, HEAD_DIM), blk(n_ctx, HEAD_DIM), blk(n_ctx, HEAD_DIM)],
        out_specs=blk(n_lat, gw),
        compiler_params=_cparams(("parallel", "parallel")),
        name="attn",
    )(sink, q, k, v, kc, vc)


def _gla_kernel(q_ref, k_ref, v_ref, gf_ref, gb_ref, kc_ref, vc_ref, gfc_ref, gbc_ref, o_ref,
                sf_ref, sb_ref, ob_ref, *, n_lat, n_ctx):
    c = B_CHUNK

    def tri(nn, fn):
        r = lax.broadcasted_iota(jnp.int32, (nn, nn), 0)
        s = lax.broadcasted_iota(jnp.int32, (nn, nn), 1)
        return fn(r, s)

    kc = kc_ref[...]
    vcb = vc_ref[...].astype(BF16)
    up_strict = tri(n_ctx, lambda r, s: s > r).astype(BF16)
    lo_strict = tri(n_ctx, lambda r, s: s < r).astype(BF16)
    kd_f = (kc * jnp.exp(_tri_sum(up_strict, gfc_ref[...]))).astype(BF16)
    kd_b = (kc * jnp.exp(_tri_sum(lo_strict, gbc_ref[...]))).astype(BF16)
    sf_ref[...] = _dot_tn(vcb, kd_f)
    sb_ref[...] = _dot_tn(vcb, kd_b)

    r = c * GLA_GROUP
    ng = n_lat // r
    same = tri(r, lambda i, j: (i // c) == (j // c))
    lo_mask = jnp.logical_and(same, tri(r, lambda i, j: j <= i))
    up_mask = jnp.logical_and(same, tri(r, lambda i, j: j >= i))
    lo_incl = lo_mask.astype(BF16)
    up_incl = up_mask.astype(BF16)
    row_chunk = lax.broadcasted_iota(jnp.int32, (r, B_DK), 0) // c

    per_dir = 8 if ng % 8 == 0 else (4 if ng % 4 == 0 else (2 if ng % 2 == 0 else 1))

    def decay_sums(rows, g_ref, tri_incl, last):
        b = _tri_sum(tri_incl, g_ref[rows, :])
        bl = jnp.concatenate(
            [jnp.broadcast_to(b[n * c + last:n * c + last + 1, :], (c, B_DK)) for n in range(GLA_GROUP)], axis=0)
        return b, bl

    def scaled_operands(rows, b, bl):
        q = q_ref[rows, :]
        k = k_ref[rows, :]
        qe = (q * jnp.exp(b)).astype(BF16)
        ke = (k * jnp.exp(-b)).astype(BF16)
        kd = k * jnp.exp(bl - b)
        kd_blocks = jnp.concatenate(
            [jnp.where(row_chunk == n, kd, 0.0).astype(BF16) for n in range(GLA_GROUP)], axis=1)
        return qe, ke, kd_blocks, jnp.exp(bl), v_ref[rows, :].astype(BF16)

    def body(i, carry):
        fwd = [pl.ds(pl.multiple_of((i * per_dir + u) * r, r), r) for u in range(per_dir)]
        bwd = [pl.ds(pl.multiple_of((ng - 1 - i * per_dir - u) * r, r), r) for u in range(per_dir)]
        streams = ([(rows, gf_ref, lo_incl, lo_mask, c - 1) for rows in fwd]
                   + [(rows, gb_ref, up_incl, up_mask, 0) for rows in bwd])
        sums = [decay_sums(rows, g_ref, tri_incl, last) for rows, g_ref, tri_incl, _, last in streams]
        ops = [scaled_operands(strm[0], b, bl) for strm, (b, bl) in zip(streams, sums)]
        a = [jnp.where(strm[3], _dot_nt(qe, ke), 0.0).astype(BF16)
             for strm, (qe, ke, _, _, _) in zip(streams, ops)]
        o_intra = [_dot(a_s, vb) for a_s, (_, _, _, _, vb) in zip(a, ops)]
        kv = [_dot_tn(vb, kd_blocks) for _, _, kd_blocks, _, vb in ops]
        o_inter = [[None] * GLA_GROUP for _ in streams]
        st_f, st_b = sf_ref[...], sb_ref[...]
        for u in range(per_dir):
            for step in range(GLA_GROUP):
                for s, n in ((u, step), (per_dir + u, GLA_GROUP - 1 - step)):
                    qe, _, _, dec, _ = ops[s]
                    st = st_f if s < per_dir else st_b
                    o_inter[s][n] = _dot_nt(qe[n * c:(n + 1) * c], st.astype(BF16))
                    st = st * dec[n * c:n * c + 1, :] + kv[s][:, n * B_DK:(n + 1) * B_DK]
                    if s < per_dir:
                        st_f = st
                    else:
                        st_b = st
        sf_ref[...] = st_f
        sb_ref[...] = st_b
        for s, (rows, *_) in enumerate(streams):
            out_ref = o_ref if s < per_dir else ob_ref
            out_ref[rows, :] = o_intra[s] + jnp.concatenate(o_inter[s], axis=0)
        return carry

    lax.fori_loop(0, ng // per_dir, body, 0)
    o_ref[...] += ob_ref[...]


def _gla(bq, bk, bv, gf, gb, bkc, bvc, gfc, gbc, bsz, n_lat, n_ctx):
    lat = lambda width: pl.BlockSpec((n_lat, width), lambda b, h: (b, h))
    cx = lambda width: pl.BlockSpec((n_ctx, width), lambda b, h: (b, h))
    return pl.pallas_call(
        functools.partial(_gla_kernel, n_lat=n_lat, n_ctx=n_ctx),
        out_shape=jax.ShapeDtypeStruct((bsz * n_lat, B_V_W), F32),
        grid=(bsz, B_HEADS),
        in_specs=[lat(B_DK), lat(B_DK), lat(B_DV), lat(B_DK), lat(B_DK),
                  cx(B_DK), cx(B_DV), cx(B_DK), cx(B_DK)],
        out_specs=lat(B_DV),
        scratch_shapes=[pltpu.VMEM((B_DV, B_DK), F32), pltpu.VMEM((B_DV, B_DK), F32),
                        pltpu.VMEM((n_lat, B_DV), F32)],
        compiler_params=_cparams(("parallel", "parallel")),
        name="gla",
    )(bq, bk, bv, gf, gb, bkc, bvc, gfc, gbc)


def _mixout_kernel(oa_ref, ob_ref, gate_ref, gn_ref, wa_ref, wb_ref, x_ref, g1_ref, o_ref):
    y = _dot(oa_ref[...], wa_ref[...])
    for hd in range(B_HEADS):
        cols = slice(hd * B_DV, (hd + 1) * B_DV)
        gt = gate_ref[:, cols]
        t = _rms(ob_ref[:, cols], gn_ref[...]) * (gt * jax.nn.sigmoid(gt))
        y = y + _dot(t.astype(BF16), wb_ref[cols, :])
    o_ref[...] = x_ref[...] + g1_ref[...] * y


def _mixout(oa, ob, gate, gla_norm, w_a, w_b, x2, mod3, mod_row_fn, tm):
    t_tok = x2.shape[0]
    tok = lambda width: pl.BlockSpec((tm, width), lambda i: (i, 0))
    const = lambda shape: pl.BlockSpec(shape, lambda i: (0, 0))
    return pl.pallas_call(
        _mixout_kernel,
        out_shape=jax.ShapeDtypeStruct((t_tok, D_MODEL), F32),
        grid=(t_tok // tm,),
        in_specs=[tok(A_Q_W), tok(B_V_W), tok(B_V_W), const((1, B_DV)),
                  const((A_Q_W, D_MODEL)), const((B_V_W, D_MODEL)), tok(D_MODEL),
                  pl.BlockSpec((None, 1, D_MODEL), lambda i: (mod_row_fn(i), 0, 2))],
        out_specs=tok(D_MODEL),
        compiler_params=_cparams(("parallel",)),
        name="mixout",
    )(oa, ob, gate, gla_norm, w_a, w_b, x2, mod3)


def _mlp_kernel(x_ref, gn_ref, sh_ref, sc_ref, g2_ref, w1_ref, w2_ref, o_ref, h_ref):
    j = pl.program_id(1)

    @pl.when(j == 0)
    def _():
        h = _rms(x_ref[...], gn_ref[...] * (1.0 + sc_ref[...])) + sh_ref[...]
        h_ref[...] = h.astype(BF16)

    def step(acc_ref):
        part = None
        for s in range(w1_ref.shape[1] // MLP_COL_BLOCK):
            cols = slice(s * MLP_COL_BLOCK, (s + 1) * MLP_COL_BLOCK)
            u = jnp.maximum(_dot(h_ref[...], w1_ref[:, cols]), 0.0)
            p = _dot((u * u).astype(BF16), w2_ref[cols, :])
            part = p if part is None else part + p
        o_ref[...] = acc_ref[...] + g2_ref[...] * part

    pl.when(j == 0)(functools.partial(step, x_ref))
    pl.when(j > 0)(functools.partial(step, o_ref))


def _mlp(x2, gn, mod3, mod_row_fn, w1, w2, layer, tm, tf):
    t_tok = x2.shape[0]
    mod = lambda col: pl.BlockSpec((None, 1, D_MODEL), lambda i, j: (mod_row_fn(i), 0, col))
    return pl.pallas_call(
        _mlp_kernel,
        out_shape=jax.ShapeDtypeStruct((t_tok, D_MODEL), F32),
        grid=(t_tok // tm, D_FF // tf),
        in_specs=[pl.BlockSpec((tm, D_MODEL), lambda i, j: (i, 0)),
                  pl.BlockSpec((1, D_MODEL), lambda i, j: (0, 0)),
                  mod(3), mod(4), mod(5),
                  pl.BlockSpec((None, D_MODEL, tf), lambda i, j: (layer, 0, j)),
                  pl.BlockSpec((None, tf, D_MODEL), lambda i, j: (layer, j, 0))],
        out_specs=pl.BlockSpec((tm, D_MODEL), lambda i, j: (i, 0)),
        scratch_shapes=[pltpu.VMEM((tm, D_MODEL), BF16)],
        compiler_params=_cparams(("parallel", "arbitrary")),
        name="mlp",
    )(x2, gn, mod3, mod3, mod3, w1, w2)


F_HALF = C_GROUPS * (C_GROUP_DIM // 2)
F_MID = LANE
F_COLS = 2 * F_HALF + F_MID


def _fch_kernel(x_ref, gn_ref, sh_ref, sc_ref, t_ref, mid_ref, yc_ref, ys_ref):
    h = _rms(x_ref[...], gn_ref[...] * (1.0 + sc_ref[...])) + sh_ref[...]
    hb = h.astype(BF16)
    half = C_GROUP_DIM // 2
    for g in range(C_GROUPS):
        y = _dot(hb[:, g * C_GROUP_DIM:(g + 1) * C_GROUP_DIM], t_ref[...])
        yc_ref[:, g * half:(g + 1) * half] = y[:, :half].astype(BF16)
        ys_ref[:, g * half:(g + 1) * half] = y[:, half:].astype(BF16)
    yc_ref[:, F_HALF:] = _dot(hb, mid_ref[...]).astype(BF16)


def _fch(x2, gn, mod3, mod_row_fn, t_ch, t_mid, tm):
    t_tok = x2.shape[0]
    mod = lambda col: pl.BlockSpec((None, 1, D_MODEL), lambda i: (mod_row_fn(i), 0, col))
    tok = lambda width: pl.BlockSpec((tm, width), lambda i: (i, 0))
    return pl.pallas_call(
        _fch_kernel,
        out_shape=[jax.ShapeDtypeStruct((t_tok, F_HALF + F_MID), BF16),
                   jax.ShapeDtypeStruct((t_tok, F_HALF), BF16)],
        grid=(t_tok // tm,),
        in_specs=[tok(D_MODEL), pl.BlockSpec((1, D_MODEL), lambda i: (0, 0)), mod(0), mod(1),
                  pl.BlockSpec((C_GROUP_DIM, C_GROUP_DIM), lambda i: (0, 0)),
                  pl.BlockSpec((D_MODEL, F_MID), lambda i: (0, 0))],
        out_specs=[tok(F_HALF + F_MID), tok(F_HALF)],
        compiler_params=_cparams(("parallel",)),
        name="fch",
    )(x2, gn, mod3, mod3, t_ch, t_mid)


F_SUB = 256
F_COL_GROUP = 512


def _fpos_kernel(cn_ref, cnx_ref, sn_ref, snx_ref, yc_ref, ys_ref, o_ref, *, scale):
    tile = cn_ref.shape[0]
    cos_rows = jnp.concatenate([cn_ref[...], cnx_ref[...]], axis=0)
    sin_rows = jnp.concatenate([sn_ref[...], snx_ref[...]], axis=0)
    groups = [slice(c * F_COL_GROUP, (c + 1) * F_COL_GROUP) for c in range(F_HALF // F_COL_GROUP)]
    products = [(_dot(cos_rows, yc_ref[:, cs]), _dot(sin_rows, ys_ref[:, cs])) for cs in groups]
    p_mid = _dot(cos_rows, yc_ref[:, F_HALF:])
    pieces = []
    for cs, (p, q) in zip(groups, products):
        hi_cs = slice(F_HALF + cs.start, F_HALF + cs.stop)
        pieces.append((cs, hi_cs, ((p - q) * scale).astype(BF16)))
        pieces.append((hi_cs, cs, ((p + q) * scale).astype(BF16)))
    pieces.append((slice(2 * F_HALF, F_COLS), slice(2 * F_HALF, F_COLS), (p_mid * scale).astype(BF16)))
    for cols, _, blk in pieces:
        o_ref[0, :, cols] = blk[:tile]
    rev = (lax.broadcasted_iota(jnp.int32, (F_SUB, F_SUB), 0)
           + lax.broadcasted_iota(jnp.int32, (F_SUB, F_SUB), 1) == F_SUB).astype(BF16)
    n_sub = tile // F_SUB
    for _, cols, blk in pieces:
        for u in range(n_sub):
            src = n_sub - 1 - u
            after = blk[(src + 1) * F_SUB:(src + 1) * F_SUB + 1]
            flipped = _dot(rev, blk[src * F_SUB:(src + 1) * F_SUB])
            first = lax.broadcasted_iota(jnp.int32, flipped.shape, 0) == 0
            o_ref[1, u * F_SUB:(u + 1) * F_SUB, cols] = jnp.where(first, after.astype(F32), flipped).astype(BF16)


def _fpos(cn, sn, yc, ys, bsz, n_lat, tile_rows):
    n_half = n_lat // 2
    nk = n_half // tile_rows
    scale = float((n_lat * C_GROUP_DIM) ** -0.5)
    tile = pl.BlockSpec((tile_rows, n_lat), lambda b, k: (k, 0))
    nxt = pl.BlockSpec((BF16_SUBLANES, n_lat), lambda b, k: ((k + 1) * (tile_rows // BF16_SUBLANES), 0))
    return pl.pallas_call(
        functools.partial(_fpos_kernel, scale=scale),
        out_shape=jax.ShapeDtypeStruct((bsz, 2, n_half, F_COLS), BF16),
        grid=(bsz, nk),
        in_specs=[tile, nxt, tile, nxt,
                  pl.BlockSpec((n_lat, F_HALF + F_MID), lambda b, k: (b, 0)),
                  pl.BlockSpec((n_lat, F_HALF), lambda b, k: (b, 0))],
        out_specs=pl.BlockSpec((None, 2, tile_rows, F_COLS), lambda b, k: (b, 0, k, 0)),
        compiler_params=_cparams(("parallel", "parallel")),
        name="fpos",
    )(cn, cn, sn, sn, yc, ys)


def _linres_kernel(*refs, n_z):
    z_refs, (w_ref, b_ref, x_ref, g1_ref, o_ref, wm_ref) = refs[:n_z], refs[n_z:]

    @pl.when(pl.program_id(0) == 0)
    def _():
        half = C_GROUP_DIM // 2
        rev = (lax.broadcasted_iota(jnp.int32, (half, half), 0)
               + lax.broadcasted_iota(jnp.int32, (half, half), 1) == half).astype(BF16)
        mid = []
        for g in range(C_GROUPS):
            base = g * C_GROUP_DIM
            wm_ref[g * half:(g + 1) * half, :] = w_ref[base:base + half, :].astype(BF16)
            upper = w_ref[base + half:base + C_GROUP_DIM, :].astype(BF16)
            wm_ref[F_HALF + g * half:F_HALF + (g + 1) * half, :] = _dot(rev, upper).astype(BF16)
            mid.append(w_ref[base + half:base + half + 1, :])
        mid.append(jnp.zeros((F_MID - C_GROUPS, D_MODEL), F32))
        wm_ref[2 * F_HALF:, :] = jnp.concatenate(mid, axis=0).astype(BF16)

    z = jnp.concatenate([z_ref[...] for z_ref in z_refs], axis=0)
    y = _dot(z, wm_ref[...]) + b_ref[...]
    o_ref[...] = x_ref[...] + g1_ref[...] * y


def _linres(z, z_rows, w, bias, x2, mod3, mod_row_fn, tm, n_lat):
    t_tok = x2.shape[0]
    tiles_per_batch = n_lat // tm
    n_z = tm // z_rows
    nk = z.shape[2] // z_rows

    def z_spec(r):
        def index(i):
            pos_tile = (i % tiles_per_batch) * n_z + r
            half, k = pos_tile // nk, pos_tile % nk
            return i // tiles_per_batch, half, jnp.where(half == 0, k, nk - 1 - k), 0
        return pl.BlockSpec((None, None, z_rows, F_COLS), index)

    tok = lambda width: pl.BlockSpec((tm, width), lambda i: (i, 0))
    return pl.pallas_call(
        functools.partial(_linres_kernel, n_z=n_z),
        out_shape=jax.ShapeDtypeStruct((t_tok, D_MODEL), F32),
        grid=(t_tok // tm,),
        in_specs=[z_spec(r) for r in range(n_z)] + [
            pl.BlockSpec((D_MODEL, D_MODEL), lambda i: (0, 0), pipeline_mode=pl.Buffered(1)),
            pl.BlockSpec((1, D_MODEL), lambda i: (0, 0)), tok(D_MODEL),
            pl.BlockSpec((None, 1, D_MODEL), lambda i: (mod_row_fn(i), 0, 2))],
        out_specs=tok(D_MODEL),
        scratch_shapes=[pltpu.VMEM((F_COLS, D_MODEL), BF16)],
        compiler_params=_cparams(("arbitrary",)),
        name="linres",
    )(*([z] * n_z), w, bias, x2, mod3)


def _rope_tables(n_lat):
    t = np.arange(n_lat)
    row = (t // GRID_W).astype(np.float32)
    col = (t % GRID_W).astype(np.float32)
    n_freq = HEAD_DIM // 4
    inv_freq = np.float32(ROPE_BASE) ** (-np.arange(n_freq, dtype=np.float32) / np.float32(n_freq))
    ang_r, ang_c = row[:, None] * inv_freq, col[:, None] * inv_freq
    cr, sr, cc, sc = np.cos(ang_r), np.sin(ang_r), np.cos(ang_c), np.sin(ang_c)
    zero = np.zeros_like(sr)
    cos = np.concatenate([cr, cr, cc, cc], axis=-1)
    sin_a = np.concatenate([-sr, zero, -sc, zero], axis=-1)
    sin_b = np.concatenate([zero, sr, zero, sc], axis=-1)
    return tuple(jnp.asarray(tab, dtype=F32) for tab in (cos, sin_a, sin_b))


def _dft_tables(n_lat):
    def cs(n):
        idx = np.arange(n, dtype=np.int64)
        ang = 2.0 * np.pi * ((idx[:, None] * idx[None, :]) % n).astype(np.float64) / n
        return np.cos(ang), np.sin(ang)

    half = C_GROUP_DIM // 2
    cm, sm = cs(C_GROUP_DIM)
    cn, sn = cs(n_lat)
    t_ch = np.concatenate([cm[:, :half], sm[:, :half]], axis=1)
    t_mid = np.zeros((D_MODEL, F_MID))
    for g in range(C_GROUPS):
        t_mid[g * C_GROUP_DIM:(g + 1) * C_GROUP_DIM, g] = cm[:, half]
    as_bf16 = lambda t: jnp.asarray(t, dtype=F32).astype(BF16)
    return as_bf16(t_ch), as_bf16(t_mid), as_bf16(cn), as_bf16(sn)


def kernel(x, c, ctx, c_ctx, ada_w, ada_b, norm_mix, norm_mlp, mlp_w1, mlp_w2, ab_w_in, ab_q_norm, ab_k_norm,
           ab_sink, ab_gk_f, ab_gk_f_bias, ab_gk_b, ab_gk_b_bias, ab_gla_norm, ab_w_out, c_w_out, c_b_out):
    bsz, n_lat, _ = x.shape
    n_ctx = ctx.shape[1]
    depth = ada_w.shape[0]
    assert depth == 2 and bsz < MOD_ROWS
    t_tok = bsz * n_lat

    cvec = jnp.concatenate([c, c_ctx[None, :], jnp.zeros((MOD_ROWS - bsz - 1, D_MODEL), F32)], axis=0)
    mod = _ada(cvec, ada_w, ada_b)
    mod3 = mod.reshape(depth * MOD_ROWS, 1, 6 * D_MODEL)

    x2 = x.reshape(t_tok, D_MODEL)

    tm = min(PROJ_ROWS, n_lat)
    tpb = n_lat // tm
    lat_row0 = lambda i: i // tpb
    w_in = ab_w_in[0].astype(BF16)
    gn_mix0 = norm_mix[0].reshape(1, D_MODEL)
    qn = ab_q_norm[0].reshape(1, HEAD_DIM)
    kn = ab_k_norm[0].reshape(1, HEAD_DIM)
    gkf = jnp.pad(ab_gk_f[0], ((0, LR_COLS - B_GATE_RANK), (0, 0))).astype(BF16)
    gkb = jnp.pad(ab_gk_b[0], ((LR_COLS - B_GATE_RANK, 0), (0, 0))).astype(BF16)
    gkfb = ab_gk_f_bias[0].reshape(1, B_QK_W)
    gkbb = ab_gk_b_bias[0].reshape(1, B_QK_W)
    steps = t_tok // tm
    mlp_rows = depth * D_MODEL * D_FF // steps
    ride_along = (depth * D_MODEL) % (steps * BF16_SUBLANES) == 0 and mlp_rows * 4 <= CAST_BLOCK_BYTES
    cast = (mlp_w1.reshape(depth * D_MODEL, D_FF), mlp_w2.reshape(depth * D_FF, D_MODEL)) if ride_along else ()
    q, k, v, bq, bk, bv, gate, gf, gb, *cast_out = _proj(
        x2, gn_mix0, mod3, lat_row0, tpb, tm, w_in, qn, kn, _rope_tables(n_lat), gkf, gkfb, gkb, gkbb, True, cast)
    if ride_along:
        w1 = cast_out[0].reshape(depth, D_MODEL, D_FF)
        w2 = cast_out[1].reshape(depth, D_FF, D_MODEL)
    else:
        w1, w2 = mlp_w1.astype(BF16), mlp_w2.astype(BF16)
    tmc = min(PROJ_ROWS, n_ctx)
    kc, vc, bkc, bvc, gfc, gbc = _proj(
        ctx.reshape(bsz * n_ctx, D_MODEL), gn_mix0, mod3, lambda i: bsz, n_ctx // tmc, tmc, w_in, None, kn, None,
        gkf, gkfb, gkb, gkbb, False)

    oa = _attn(ab_sink[0], q, k, v, kc, vc, bsz, n_lat, n_ctx)
    ob = _gla(bq, bk, bv, gf, gb, bkc, bvc, gfc, gbc, bsz, n_lat, n_ctx)

    w_out = ab_w_out[0].astype(BF16)
    tmo = min(TOKEN_ROWS, n_lat)
    tpbo = n_lat // tmo
    x2 = _mixout(oa, ob, gate, ab_gla_norm[0].reshape(1, B_DV), w_out[:A_Q_W], w_out[A_Q_W:], x2, mod3,
                 lambda i: i // tpbo, tmo)
    x2 = _mlp(x2, norm_mlp[0].reshape(1, D_MODEL), mod3, lambda i: i // tpbo, w1, w2, 0, tmo, MLP_STEP_COLS)

    t_ch, t_mid, cn, sn = _dft_tables(n_lat)
    row1 = lambda i: MOD_ROWS + i // tpbo
    yc, ys = _fch(x2, norm_mix[1].reshape(1, D_MODEL), mod3, row1, t_ch, t_mid, tmo)
    z_rows = min(tmo, n_lat // 2)
    z = _fpos(cn, sn, yc, ys, bsz, n_lat, z_rows)
    x2 = _linres(z, z_rows, c_w_out[0], c_b_out[0].reshape(1, D_MODEL), x2, mod3, row1, tmo, n_lat)
    x2 = _mlp(x2, norm_mlp[1].reshape(1, D_MODEL), mod3, row1, w1, w2, 1, tmo, MLP_STEP_COLS)
    return x2.reshape(bsz, n_lat, D_MODEL)
```

```python
import functools

import numpy as np
import jax
import jax.numpy as jnp
from jax import lax
from jax.experimental import pallas as pl
from jax.experimental.pallas import tpu as pltpu

D_MODEL = 2048
GRID_W = 64
EPS = 1e-6
NEG_INF = -1e30
HEAD_DIM = 128
A_Q_HEADS = 8
A_KV_HEADS = 2
A_GROUP = A_Q_HEADS // A_KV_HEADS
WINDOW = 128
BLOCK_Q = 128
LOG2_E = 1.4426950408889634
Q_SCALE = HEAD_DIM ** -0.5 * LOG2_E
ROPE_BASE = 10000.0
B_HEADS = 4
B_DV = 256
B_DK = 128
B_GATE_RANK = 16
B_GATE_NORM = 16.0
B_CHUNK = 64
GLA_GROUP = 4
C_GROUPS = 8
C_GROUP_DIM = D_MODEL // C_GROUPS
D_FF = 4 * D_MODEL

A_Q_W = A_Q_HEADS * HEAD_DIM
A_KV_W = A_KV_HEADS * HEAD_DIM
B_QK_W = B_HEADS * B_DK
B_V_W = B_HEADS * B_DV
OFF_AQ = 0
OFF_AK = OFF_AQ + A_Q_W
OFF_AV = OFF_AK + A_KV_W
OFF_BQ = OFF_AV + A_KV_W
OFF_BK = OFF_BQ + B_QK_W
OFF_BV = OFF_BK + B_QK_W
OFF_GATE = OFF_BV + B_V_W
OFF_LR = OFF_GATE + B_V_W
LR_COLS = 2 * B_GATE_RANK
IN_COLS = OFF_LR + LR_COLS
LANE = 128
BF16_SUBLANES = 16
MOD_ROWS = BF16_SUBLANES
VMEM_LIMIT = 56 * 1024 * 1024
CAST_BLOCK_BYTES = 2 * 1024 * 1024
MLP_COL_BLOCK = 1024
MLP_STEP_COLS = 2048
PROJ_ROWS = 256
TOKEN_ROWS = 512
FCH_ROWS = 1024
BF16 = jnp.bfloat16
F32 = jnp.float32


def _cparams(sem):
    return pltpu.CompilerParams(dimension_semantics=sem, vmem_limit_bytes=VMEM_LIMIT)


def _dot(a, b):
    return jnp.dot(a, b, preferred_element_type=F32)


def _dot_nt(a, b):
    return lax.dot_general(a, b, (((1,), (1,)), ((), ())), preferred_element_type=F32)


def _dot_tn(a, b):
    return lax.dot_general(a, b, (((0,), (0,)), ((), ())), preferred_element_type=F32)


def _tri_sum(tri, g):
    g_hi = g.astype(BF16)
    g_lo = (g - g_hi.astype(F32)).astype(BF16)
    d = g.shape[1]
    both = _dot(tri, jnp.concatenate([g_hi, g_lo], axis=1))
    return both[:, :d] + both[:, d:]


def _rms(xf, gain):
    return xf * lax.rsqrt(jnp.mean(xf * xf, axis=-1, keepdims=True) + EPS) * gain


def _ada_kernel(c_ref, w_ref, b_ref, o_ref):
    cv = c_ref[...]
    s = (cv * jax.nn.sigmoid(cv)).astype(BF16)
    o_ref[...] = _dot(s, w_ref[...].astype(BF16)) + b_ref[...]


def _ada(cvec, ada_w, ada_b):
    depth = ada_w.shape[0]
    n = ada_w.shape[2]
    tn = 1024
    return pl.pallas_call(
        _ada_kernel,
        out_shape=jax.ShapeDtypeStruct((depth, MOD_ROWS, n), F32),
        grid=(depth, n // tn),
        in_specs=[
            pl.BlockSpec((MOD_ROWS, D_MODEL), lambda l, j: (0, 0)),
            pl.BlockSpec((None, D_MODEL, tn), lambda l, j: (l, 0, j)),
            pl.BlockSpec((None, 1, tn), lambda l, j: (l, 0, j)),
        ],
        out_specs=pl.BlockSpec((None, MOD_ROWS, tn), lambda l, j: (l, 0, j)),
        compiler_params=_cparams(("parallel", "parallel")),
        name="ada",
    )(cvec, ada_w, ada_b.reshape(depth, 1, n))


def _proj_kernel(*refs, latent, n_cast):
    for src_ref, dst_ref in zip(refs[:n_cast], refs[len(refs) - n_cast:]):
        dst_ref[...] = src_ref[...].astype(BF16)
    refs = refs[n_cast:len(refs) - n_cast]
    if latent:
        (x_ref, gn_ref, sh_ref, sc_ref, w_ref, qn_ref, kn_ref, cos_ref, sa_ref, sb_ref,
         gkf_ref, gkfb_ref, gkb_ref, gkbb_ref,
         q_ref, k_ref, v_ref, bq_ref, bk_ref, bv_ref, gate_ref, gf_ref, gb_ref) = refs
    else:
        (x_ref, gn_ref, sh_ref, sc_ref, w_ref, kn_ref,
         gkf_ref, gkfb_ref, gkb_ref, gkbb_ref,
         k_ref, v_ref, bk_ref, bv_ref, gf_ref, gb_ref) = refs

    xt = x_ref[...]
    h = _rms(xt, gn_ref[...]) * (1.0 + sc_ref[...]) + sh_ref[...]
    hb = h.astype(BF16)

    def proj(off, width):
        return _dot(hb, w_ref[:, off:off + width])

    def rope(t):
        return (t * cos_ref[...] + pltpu.roll(t, HEAD_DIM - HEAD_DIM // 4, 1) * sa_ref[...]
                + pltpu.roll(t, HEAD_DIM // 4, 1) * sb_ref[...])

    if latent:
        for half in range(2):
            seg = proj(OFF_AQ + half * (A_Q_W // 2), A_Q_W // 2)
            for j in range(A_Q_HEADS // 2):
                t = _rms(seg[:, j * HEAD_DIM:(j + 1) * HEAD_DIM], qn_ref[...])
                t = rope(t) * Q_SCALE
                hd = half * (A_Q_HEADS // 2) + j
                q_ref[:, hd * HEAD_DIM:(hd + 1) * HEAD_DIM] = t.astype(BF16)

    seg = proj(OFF_AK, 2 * A_KV_W)
    for j in range(A_KV_HEADS):
        t = _rms(seg[:, j * HEAD_DIM:(j + 1) * HEAD_DIM], kn_ref[...])
        if latent:
            t = rope(t)
        k_ref[:, j * HEAD_DIM:(j + 1) * HEAD_DIM] = t.astype(BF16)
    v_ref[...] = seg[:, A_KV_W:].astype(BF16)

    if latent:
        bq_ref[...] = proj(OFF_BQ, B_QK_W) * B_DK ** -0.5
    bk_ref[...] = proj(OFF_BK, B_QK_W)
    for half in range(2):
        bv_ref[:, half * (B_V_W // 2):(half + 1) * (B_V_W // 2)] = proj(OFF_BV + half * (B_V_W // 2), B_V_W // 2)
    if latent:
        for half in range(2):
            gate_ref[:, half * (B_V_W // 2):(half + 1) * (B_V_W // 2)] = proj(
                OFF_GATE + half * (B_V_W // 2), B_V_W // 2)

    lr = proj(OFF_LR, LR_COLS).astype(BF16)

    def log_decay(gk_ref, gkb_ref, out_ref):
        z = _dot(lr, gk_ref[...]) + gkb_ref[...]
        out_ref[...] = (jnp.minimum(z, 0.0) - jnp.log1p(jnp.exp(-jnp.abs(z)))) / B_GATE_NORM

    log_decay(gkf_ref, gkfb_ref, gf_ref)
    log_decay(gkb_ref, gkbb_ref, gb_ref)


def _proj(x2, gn, mod3, mod_row_fn, tiles_per_batch, tm, w_in, qn, kn, rope_tabs, gkf, gkfb, gkb, gkbb, latent,
          cast=()):
    t_tok = x2.shape[0]
    steps = t_tok // tm
    row = lambda width: pl.BlockSpec((1, width), lambda i: (0, 0))
    tok = lambda width: pl.BlockSpec((tm, width), lambda i: (i, 0))
    mod = lambda col: pl.BlockSpec((None, 1, D_MODEL), lambda i: (mod_row_fn(i), 0, col))
    in_specs = [tok(D_MODEL), row(D_MODEL), mod(0), mod(1),
                pl.BlockSpec((D_MODEL, IN_COLS), lambda i: (0, 0), pipeline_mode=pl.Buffered(1))]
    args = [x2, gn, mod3, mod3, w_in]
    if latent:
        in_specs += [row(HEAD_DIM), row(HEAD_DIM)]
        args += [qn, kn]
        in_specs += [pl.BlockSpec((tm, HEAD_DIM), lambda i: (i % tiles_per_batch, 0))] * 3
        args += list(rope_tabs)
    else:
        in_specs += [row(HEAD_DIM)]
        args += [kn]
    in_specs += [pl.BlockSpec((LR_COLS, B_QK_W), lambda i: (0, 0)), row(B_QK_W)] * 2
    args += [gkf, gkfb, gkb, gkbb]

    def out(width, dtype):
        return jax.ShapeDtypeStruct((t_tok, width), dtype), tok(width)

    if latent:
        outs = [out(A_Q_W, BF16), out(A_KV_W, BF16), out(A_KV_W, BF16), out(B_QK_W, F32), out(B_QK_W, F32),
                out(B_V_W, F32), out(B_V_W, F32), out(B_QK_W, F32), out(B_QK_W, F32)]
    else:
        outs = [out(A_KV_W, BF16), out(A_KV_W, BF16), out(B_QK_W, F32), out(B_V_W, F32),
                out(B_QK_W, F32), out(B_QK_W, F32)]
    cast_specs = [pl.BlockSpec((a.shape[0] // steps, a.shape[1]), lambda i: (i, 0)) for a in cast]
    outs += [(jax.ShapeDtypeStruct(a.shape, BF16), spec) for a, spec in zip(cast, cast_specs)]
    return pl.pallas_call(
        functools.partial(_proj_kernel, latent=latent, n_cast=len(cast)),
        out_shape=[o[0] for o in outs],
        grid=(steps,),
        in_specs=cast_specs + in_specs,
        out_specs=[o[1] for o in outs],
        compiler_params=_cparams(("parallel",)),
        name="proj_lat" if latent else "proj_ctx",
    )(*cast, *args)


def _attn_kernel(sink_ref, q_ref, k_ref, v_ref, kc_ref, vc_ref, o_ref, *, n_lat):
    kvh = pl.program_id(1)
    n_win = 3 * BLOCK_Q
    heads = range(A_GROUP)
    kc = kc_ref[...]
    vc = vc_ref[...]
    sk = jnp.concatenate(
        [jnp.full((1, BLOCK_Q), sink_ref[kvh * A_GROUP + g] * LOG2_E, F32) for g in heads], axis=1)
    key_minus_query = (lax.broadcasted_iota(jnp.int32, (n_win, BLOCK_Q), 0)
                       - lax.broadcasted_iota(jnp.int32, (n_win, BLOCK_Q), 1))

    def scores(n):
        start = jnp.clip(n * BLOCK_Q - BLOCK_Q, 0, n_lat - n_win)
        start = pl.multiple_of(start, BLOCK_Q)
        rows = pl.ds(pl.multiple_of(n * BLOCK_Q, BLOCK_Q), BLOCK_Q)
        kw = k_ref[pl.ds(start, n_win), :]
        valid = jnp.abs(key_minus_query + (start - n * BLOCK_Q)) <= WINDOW
        q = jnp.concatenate([q_ref[rows, g * HEAD_DIM:(g + 1) * HEAD_DIM] for g in heads], axis=0)
        s_lat = _dot_nt(kw, q)
        s_lat = jnp.concatenate(
            [jnp.where(valid, s_lat[:, g * BLOCK_Q:(g + 1) * BLOCK_Q], NEG_INF) for g in heads], axis=1)
        s_ctx = _dot_nt(kc, q)
        return start, rows, s_lat, s_ctx

    def softmax(s_lat, s_ctx):
        m = jnp.maximum(jnp.maximum(jnp.max(s_lat, axis=0, keepdims=True),
                                    jnp.max(s_ctx, axis=0, keepdims=True)), sk)
        p_lat = jnp.exp2(s_lat - m)
        p_ctx = jnp.exp2(s_ctx - m)
        den = (jnp.sum(p_lat, axis=0, keepdims=True) + jnp.sum(p_ctx, axis=0, keepdims=True)
               + jnp.exp2(sk - m))
        return p_lat.astype(BF16), p_ctx.astype(BF16), 1.0 / den

    def weighted_values(start, rows, p_lat, p_ctx, inv_den):
        vw = v_ref[pl.ds(start, n_win), :]
        o_t = (_dot_tn(vw, p_lat) + _dot_tn(vc, p_ctx)) * inv_den
        for g in heads:
            o_ref[rows, g * HEAD_DIM:(g + 1) * HEAD_DIM] = o_t[:, g * BLOCK_Q:(g + 1) * BLOCK_Q].T.astype(BF16)

    nb = n_lat // BLOCK_Q
    per_step = 16 if nb % 16 == 0 else (4 if nb % 4 == 0 else 1)

    def step(i, carry):
        sc = [scores(i * per_step + u) for u in range(per_step)]
        pr = [softmax(s_lat, s_ctx) for _, _, s_lat, s_ctx in sc]
        for (start, rows, _, _), (p_lat, p_ctx, inv_den) in zip(sc, pr):
            weighted_values(start, rows, p_lat, p_ctx, inv_den)
        return carry

    lax.fori_loop(0, nb // per_step, step, 0)


def _attn(sink, q, k, v, kc, vc, bsz, n_lat, n_ctx):
    gw = A_GROUP * HEAD_DIM
    blk = lambda rows, width: pl.BlockSpec((rows, width), lambda b, h: (b, h))
    return pl.pallas_call(
        functools.partial(_attn_kernel, n_lat=n_lat),
        out_shape=jax.ShapeDtypeStruct((bsz * n_lat, A_Q_W), BF16),
        grid=(bsz, A_KV_HEADS),
        in_specs=[pl.BlockSpec(memory_space=pltpu.SMEM), blk(n_lat, gw), blk(n_lat, HEAD_DIM),
                  blk(n_lat, HEAD_DIM), blk(n_ctx, HEAD_DIM), blk(n_ctx, HEAD_DIM)],
        out_specs=blk(n_lat, gw),
        compiler_params=_cparams(("parallel", "parallel")),
        name="attn",
    )(sink, q, k, v, kc, vc)


def _gla_kernel(q_ref, k_ref, v_ref, gf_ref, gb_ref, kc_ref, vc_ref, gfc_ref, gbc_ref, o_ref,
                sf_ref, sb_ref, ob_ref, *, n_lat, n_ctx):
    c = B_CHUNK

    def tri(nn, fn):
        r = lax.broadcasted_iota(jnp.int32, (nn, nn), 0)
        s = lax.broadcasted_iota(jnp.int32, (nn, nn), 1)
        return fn(r, s)

    kc = kc_ref[...]
    vcb = vc_ref[...].astype(BF16)
    up_strict = tri(n_ctx, lambda r, s: s > r).astype(BF16)
    lo_strict = tri(n_ctx, lambda r, s: s < r).astype(BF16)
    kd_f = (kc * jnp.exp(_tri_sum(up_strict, gfc_ref[...]))).astype(BF16)
    kd_b = (kc * jnp.exp(_tri_sum(lo_strict, gbc_ref[...]))).astype(BF16)
    sf_ref[...] = _dot_tn(vcb, kd_f)
    sb_ref[...] = _dot_tn(vcb, kd_b)

    r = c * GLA_GROUP
    ng = n_lat // r
    same = tri(r, lambda i, j: (i // c) == (j // c))
    lo_mask = jnp.logical_and(same, tri(r, lambda i, j: j <= i))
    up_mask = jnp.logical_and(same, tri(r, lambda i, j: j >= i))
    lo_incl = lo_mask.astype(BF16)
    up_incl = up_mask.astype(BF16)
    row_chunk = lax.broadcasted_iota(jnp.int32, (r, B_DK), 0) // c

    per_dir = 8 if ng % 8 == 0 else (4 if ng % 4 == 0 else (2 if ng % 2 == 0 else 1))

    def decay_sums(rows, g_ref, tri_incl, last):
        b = _tri_sum(tri_incl, g_ref[rows, :])
        bl = jnp.concatenate(
            [jnp.broadcast_to(b[n * c + last:n * c + last + 1, :], (c, B_DK)) for n in range(GLA_GROUP)], axis=0)
        return b, bl

    def scaled_operands(rows, b, bl):
        q = q_ref[rows, :]
        k = k_ref[rows, :]
        qe = (q * jnp.exp(b)).astype(BF16)
        ke = (k * jnp.exp(-b)).astype(BF16)
        kd = k * jnp.exp(bl - b)
        kd_blocks = jnp.concatenate(
            [jnp.where(row_chunk == n, kd, 0.0).astype(BF16) for n in range(GLA_GROUP)], axis=1)
        return qe, ke, kd_blocks, jnp.exp(bl), v_ref[rows, :].astype(BF16)

    def body(i, carry):
        fwd = [pl.ds(pl.multiple_of((i * per_dir + u) * r, r), r) for u in range(per_dir)]
        bwd = [pl.ds(pl.multiple_of((ng - 1 - i * per_dir - u) * r, r), r) for u in range(per_dir)]
        streams = ([(rows, gf_ref, lo_incl, lo_mask, c - 1) for rows in fwd]
                   + [(rows, gb_ref, up_incl, up_mask, 0) for rows in bwd])
        sums = [decay_sums(rows, g_ref, tri_incl, last) for rows, g_ref, tri_incl, _, last in streams]
        ops = [scaled_operands(strm[0], b, bl) for strm, (b, bl) in zip(streams, sums)]
        a = [jnp.where(strm[3], _dot_nt(qe, ke), 0.0).astype(BF16)
             for strm, (qe, ke, _, _, _) in zip(streams, ops)]
        o_intra = [_dot(a_s, vb) for a_s, (_, _, _, _, vb) in zip(a, ops)]
        kv = [_dot_tn(vb, kd_blocks) for _, _, kd_blocks, _, vb in ops]
        o_inter = [[None] * GLA_GROUP for _ in streams]
        st_f, st_b = sf_ref[...], sb_ref[...]
        for u in range(per_dir):
            for step in range(GLA_GROUP):
                for s, n in ((u, step), (per_dir + u, GLA_GROUP - 1 - step)):
                    qe, _, _, dec, _ = ops[s]
                    st = st_f if s < per_dir else st_b
                    o_inter[s][n] = _dot_nt(qe[n * c:(n + 1) * c], st.astype(BF16))
                    st = st * dec[n * c:n * c + 1, :] + kv[s][:, n * B_DK:(n + 1) * B_DK]
                    if s < per_dir:
                        st_f = st
                    else:
                        st_b = st
        sf_ref[...] = st_f
        sb_ref[...] = st_b
        for s, (rows, *_) in enumerate(streams):
            out_ref = o_ref if s < per_dir else ob_ref
            out_ref[rows, :] = o_intra[s] + jnp.concatenate(o_inter[s], axis=0)
        return carry

    lax.fori_loop(0, ng // per_dir, body, 0)
    o_ref[...] += ob_ref[...]


def _gla(bq, bk, bv, gf, gb, bkc, bvc, gfc, gbc, bsz, n_lat, n_ctx):
    lat = lambda width: pl.BlockSpec((n_lat, width), lambda b, h: (b, h))
    cx = lambda width: pl.BlockSpec((n_ctx, width), lambda b, h: (b, h))
    return pl.pallas_call(
        functools.partial(_gla_kernel, n_lat=n_lat, n_ctx=n_ctx),
        out_shape=jax.ShapeDtypeStruct((bsz * n_lat, B_V_W), F32),
        grid=(bsz, B_HEADS),
        in_specs=[lat(B_DK), lat(B_DK), lat(B_DV), lat(B_DK), lat(B_DK),
                  cx(B_DK), cx(B_DV), cx(B_DK), cx(B_DK)],
        out_specs=lat(B_DV),
        scratch_shapes=[pltpu.VMEM((B_DV, B_DK), F32), pltpu.VMEM((B_DV, B_DK), F32),
                        pltpu.VMEM((n_lat, B_DV), F32)],
        compiler_params=_cparams(("parallel", "parallel")),
        name="gla",
    )(bq, bk, bv, gf, gb, bkc, bvc, gfc, gbc)


def _mixout_kernel(oa_ref, ob_ref, gate_ref, gn_ref, wa_ref, wb_ref, x_ref, g1_ref, o_ref):
    y = _dot(oa_ref[...], wa_ref[...])
    for hd in range(B_HEADS):
        cols = slice(hd * B_DV, (hd + 1) * B_DV)
        gt = gate_ref[:, cols]
        t = _rms(ob_ref[:, cols], gn_ref[...]) * (gt * jax.nn.sigmoid(gt))
        y = y + _dot(t.astype(BF16), wb_ref[cols, :])
    o_ref[...] = x_ref[...] + g1_ref[...] * y


def _mixout(oa, ob, gate, gla_norm, w_a, w_b, x2, mod3, mod_row_fn, tm):
    t_tok = x2.shape[0]
    tok = lambda width: pl.BlockSpec((tm, width), lambda i: (i, 0))
    const = lambda shape: pl.BlockSpec(shape, lambda i: (0, 0))
    return pl.pallas_call(
        _mixout_kernel,
        out_shape=jax.ShapeDtypeStruct((t_tok, D_MODEL), F32),
        grid=(t_tok // tm,),
        in_specs=[tok(A_Q_W), tok(B_V_W), tok(B_V_W), const((1, B_DV)),
                  const((A_Q_W, D_MODEL)), const((B_V_W, D_MODEL)), tok(D_MODEL),
                  pl.BlockSpec((None, 1, D_MODEL), lambda i: (mod_row_fn(i), 0, 2))],
        out_specs=tok(D_MODEL),
        compiler_params=_cparams(("parallel",)),
        name="mixout",
    )(oa, ob, gate, gla_norm, w_a, w_b, x2, mod3)


def _mlp_kernel(x_ref, gn_ref, sh_ref, sc_ref, g2_ref, w1_ref, w2_ref, o_ref, h_ref):
    j = pl.program_id(1)

    @pl.when(j == 0)
    def _():
        h = _rms(x_ref[...], gn_ref[...] * (1.0 + sc_ref[...])) + sh_ref[...]
        h_ref[...] = h.astype(BF16)

    def step(acc_ref):
        part = None
        for s in range(w1_ref.shape[1] // MLP_COL_BLOCK):
            cols = slice(s * MLP_COL_BLOCK, (s + 1) * MLP_COL_BLOCK)
            u = jnp.maximum(_dot(h_ref[...], w1_ref[:, cols]), 0.0)
            p = _dot((u * u).astype(BF16), w2_ref[cols, :])
            part = p if part is None else part + p
        o_ref[...] = acc_ref[...] + g2_ref[...] * part

    pl.when(j == 0)(functools.partial(step, x_ref))
    pl.when(j > 0)(functools.partial(step, o_ref))


def _mlp(x2, gn, mod3, mod_row_fn, w1, w2, layer, tm, tf):
    t_tok = x2.shape[0]
    mod = lambda col: pl.BlockSpec((None, 1, D_MODEL), lambda i, j: (mod_row_fn(i), 0, col))
    return pl.pallas_call(
        _mlp_kernel,
        out_shape=jax.ShapeDtypeStruct((t_tok, D_MODEL), F32),
        grid=(t_tok // tm, D_FF // tf),
        in_specs=[pl.BlockSpec((tm, D_MODEL), lambda i, j: (i, 0)),
                  pl.BlockSpec((1, D_MODEL), lambda i, j: (0, 0)),
                  mod(3), mod(4), mod(5),
                  pl.BlockSpec((None, D_MODEL, tf), lambda i, j: (layer, 0, j)),
                  pl.BlockSpec((None, tf, D_MODEL), lambda i, j: (layer, j, 0))],
        out_specs=pl.BlockSpec((tm, D_MODEL), lambda i, j: (i, 0)),
        scratch_shapes=[pltpu.VMEM((tm, D_MODEL), BF16)],
        compiler_params=_cparams(("parallel", "arbitrary")),
        name="mlp",
    )(x2, gn, mod3, mod3, mod3, w1, w2)


F_HALF = C_GROUPS * (C_GROUP_DIM // 2)
F_MID = LANE
F_COLS = 2 * F_HALF + F_MID


def _fch_kernel(x_ref, gn_ref, sh_ref, sc_ref, t_ref, mid_ref, yc_ref, ys_ref):
    h = _rms(x_ref[...], gn_ref[...] * (1.0 + sc_ref[...])) + sh_ref[...]
    hb = h.astype(BF16)
    half = C_GROUP_DIM // 2
    for g in range(C_GROUPS):
        y = _dot(hb[:, g * C_GROUP_DIM:(g + 1) * C_GROUP_DIM], t_ref[...])
        yc_ref[:, g * half:(g + 1) * half] = y[:, :half].astype(BF16)
        ys_ref[:, g * half:(g + 1) * half] = y[:, half:].astype(BF16)
    yc_ref[:, F_HALF:] = _dot(hb, mid_ref[...]).astype(BF16)


def _fch(x2, gn, mod3, mod_row_fn, t_ch, t_mid, tm):
    t_tok = x2.shape[0]
    mod = lambda col: pl.BlockSpec((None, 1, D_MODEL), lambda i: (mod_row_fn(i), 0, col))
    tok = lambda width: pl.BlockSpec((tm, width), lambda i: (i, 0))
    return pl.pallas_call(
        _fch_kernel,
        out_shape=[jax.ShapeDtypeStruct((t_tok, F_HALF + F_MID), BF16),
                   jax.ShapeDtypeStruct((t_tok, F_HALF), BF16)],
        grid=(t_tok // tm,),
        in_specs=[tok(D_MODEL), pl.BlockSpec((1, D_MODEL), lambda i: (0, 0)), mod(0), mod(1),
                  pl.BlockSpec((C_GROUP_DIM, C_GROUP_DIM), lambda i: (0, 0)),
                  pl.BlockSpec((D_MODEL, F_MID), lambda i: (0, 0))],
        out_specs=[tok(F_HALF + F_MID), tok(F_HALF)],
        compiler_params=_cparams(("parallel",)),
        name="fch",
    )(x2, gn, mod3, mod3, t_ch, t_mid)


F_SUB = 256
F_COL_GROUP = 512


def _fpos_kernel(cn_ref, cnx_ref, sn_ref, snx_ref, yc_ref, ys_ref, o_ref, *, scale):
    tile = cn_ref.shape[0]
    cos_rows = jnp.concatenate([cn_ref[...], cnx_ref[...]], axis=0)
    sin_rows = jnp.concatenate([sn_ref[...], snx_ref[...]], axis=0)
    groups = [slice(c * F_COL_GROUP, (c + 1) * F_COL_GROUP) for c in range(F_HALF // F_COL_GROUP)]
    products = [(_dot(cos_rows, yc_ref[:, cs]), _dot(sin_rows, ys_ref[:, cs])) for cs in groups]
    p_mid = _dot(cos_rows, yc_ref[:, F_HALF:])
    pieces = []
    for cs, (p, q) in zip(groups, products):
        hi_cs = slice(F_HALF + cs.start, F_HALF + cs.stop)
        pieces.append((cs, hi_cs, ((p - q) * scale).astype(BF16)))
        pieces.append((hi_cs, cs, ((p + q) * scale).astype(BF16)))
    pieces.append((slice(2 * F_HALF, F_COLS), slice(2 * F_HALF, F_COLS), (p_mid * scale).astype(BF16)))
    for cols, _, blk in pieces:
        o_ref[0, :, cols] = blk[:tile]
    rev = (lax.broadcasted_iota(jnp.int32, (F_SUB, F_SUB), 0)
           + lax.broadcasted_iota(jnp.int32, (F_SUB, F_SUB), 1) == F_SUB).astype(BF16)
    n_sub = tile // F_SUB
    for _, cols, blk in pieces:
        for u in range(n_sub):
            src = n_sub - 1 - u
            after = blk[(src + 1) * F_SUB:(src + 1) * F_SUB + 1]
            flipped = _dot(rev, blk[src * F_SUB:(src + 1) * F_SUB])
            first = lax.broadcasted_iota(jnp.int32, flipped.shape, 0) == 0
            o_ref[1, u * F_SUB:(u + 1) * F_SUB, cols] = jnp.where(first, after.astype(F32), flipped).astype(BF16)


def _fpos(cn, sn, yc, ys, bsz, n_lat, tile_rows):
    n_half = n_lat // 2
    nk = n_half // tile_rows
    scale = float((n_lat * C_GROUP_DIM) ** -0.5)
    tile = pl.BlockSpec((tile_rows, n_lat), lambda b, k: (k, 0))
    nxt = pl.BlockSpec((BF16_SUBLANES, n_lat), lambda b, k: ((k + 1) * (tile_rows // BF16_SUBLANES), 0))
    return pl.pallas_call(
        functools.partial(_fpos_kernel, scale=scale),
        out_shape=jax.ShapeDtypeStruct((bsz, 2, n_half, F_COLS), BF16),
        grid=(bsz, nk),
        in_specs=[tile, nxt, tile, nxt,
                  pl.BlockSpec((n_lat, F_HALF + F_MID), lambda b, k: (b, 0)),
                  pl.BlockSpec((n_lat, F_HALF), lambda b, k: (b, 0))],
        out_specs=pl.BlockSpec((None, 2, tile_rows, F_COLS), lambda b, k: (b, 0, k, 0)),
        compiler_params=_cparams(("parallel", "parallel")),
        name="fpos",
    )(cn, cn, sn, sn, yc, ys)


def _linres_kernel(*refs, n_z):
    z_refs, (w_ref, b_ref, x_ref, g1_ref, o_ref, wm_ref) = refs[:n_z], refs[n_z:]

    @pl.when(pl.program_id(0) == 0)
    def _():
        half = C_GROUP_DIM // 2
        rev = (lax.broadcasted_iota(jnp.int32, (half, half), 0)
               + lax.broadcasted_iota(jnp.int32, (half, half), 1) == half).astype(BF16)
        mid = []
        for g in range(C_GROUPS):
            base = g * C_GROUP_DIM
            wm_ref[g * half:(g + 1) * half, :] = w_ref[base:base + half, :].astype(BF16)
            upper = w_ref[base + half:base + C_GROUP_DIM, :].astype(BF16)
            wm_ref[F_HALF + g * half:F_HALF + (g + 1) * half, :] = _dot(rev, upper).astype(BF16)
            mid.append(w_ref[base + half:base + half + 1, :])
        mid.append(jnp.zeros((F_MID - C_GROUPS, D_MODEL), F32))
        wm_ref[2 * F_HALF:, :] = jnp.concatenate(mid, axis=0).astype(BF16)

    z = jnp.concatenate([z_ref[...] for z_ref in z_refs], axis=0)
    y = _dot(z, wm_ref[...]) + b_ref[...]
    o_ref[...] = x_ref[...] + g1_ref[...] * y


def _linres(z, z_rows, w, bias, x2, mod3, mod_row_fn, tm, n_lat):
    t_tok = x2.shape[0]
    tiles_per_batch = n_lat // tm
    n_z = tm // z_rows
    nk = z.shape[2] // z_rows

    def z_spec(r):
        def index(i):
            pos_tile = (i % tiles_per_batch) * n_z + r
            half, k = pos_tile // nk, pos_tile % nk
            return i // tiles_per_batch, half, jnp.where(half == 0, k, nk - 1 - k), 0
        return pl.BlockSpec((None, None, z_rows, F_COLS), index)

    tok = lambda width: pl.BlockSpec((tm, width), lambda i: (i, 0))
    return pl.pallas_call(
        functools.partial(_linres_kernel, n_z=n_z),
        out_shape=jax.ShapeDtypeStruct((t_tok, D_MODEL), F32),
        grid=(t_tok // tm,),
        in_specs=[z_spec(r) for r in range(n_z)] + [
            pl.BlockSpec((D_MODEL, D_MODEL), lambda i: (0, 0), pipeline_mode=pl.Buffered(1)),
            pl.BlockSpec((1, D_MODEL), lambda i: (0, 0)), tok(D_MODEL),
            pl.BlockSpec((None, 1, D_MODEL), lambda i: (mod_row_fn(i), 0, 2))],
        out_specs=tok(D_MODEL),
        scratch_shapes=[pltpu.VMEM((F_COLS, D_MODEL), BF16)],
        compiler_params=_cparams(("arbitrary",)),
        name="linres",
    )(*([z] * n_z), w, bias, x2, mod3)


def _rope_tables(n_lat):
    t = np.arange(n_lat)
    row = (t // GRID_W).astype(np.float32)
    col = (t % GRID_W).astype(np.float32)
    n_freq = HEAD_DIM // 4
    inv_freq = np.float32(ROPE_BASE) ** (-np.arange(n_freq, dtype=np.float32) / np.float32(n_freq))
    ang_r, ang_c = row[:, None] * inv_freq, col[:, None] * inv_freq
    cr, sr, cc, sc = np.cos(ang_r), np.sin(ang_r), np.cos(ang_c), np.sin(ang_c)
    zero = np.zeros_like(sr)
    cos = np.concatenate([cr, cr, cc, cc], axis=-1)
    sin_a = np.concatenate([-sr, zero, -sc, zero], axis=-1)
    sin_b = np.concatenate([zero, sr, zero, sc], axis=-1)
    return tuple(jnp.asarray(tab, dtype=F32) for tab in (cos, sin_a, sin_b))


def _dft_tables(n_lat):
    def cs(n):
        idx = np.arange(n, dtype=np.int64)
        ang = 2.0 * np.pi * ((idx[:, None] * idx[None, :]) % n).astype(np.float64) / n
        return np.cos(ang), np.sin(ang)

    half = C_GROUP_DIM // 2
    cm, sm = cs(C_GROUP_DIM)
    cn, sn = cs(n_lat)
    t_ch = np.concatenate([cm[:, :half], sm[:, :half]], axis=1)
    t_mid = np.zeros((D_MODEL, F_MID))
    for g in range(C_GROUPS):
        t_mid[g * C_GROUP_DIM:(g + 1) * C_GROUP_DIM, g] = cm[:, half]
    as_bf16 = lambda t: jnp.asarray(t, dtype=F32).astype(BF16)
    return as_bf16(t_ch), as_bf16(t_mid), as_bf16(cn), as_bf16(sn)


def kernel(x, c, ctx, c_ctx, ada_w, ada_b, norm_mix, norm_mlp, mlp_w1, mlp_w2, ab_w_in, ab_q_norm, ab_k_norm,
           ab_sink, ab_gk_f, ab_gk_f_bias, ab_gk_b, ab_gk_b_bias, ab_gla_norm, ab_w_out, c_w_out, c_b_out):
    bsz, n_lat, _ = x.shape
    n_ctx = ctx.shape[1]
    depth = ada_w.shape[0]
    assert depth == 2 and bsz < MOD_ROWS
    t_tok = bsz * n_lat

    cvec = jnp.concatenate([c, c_ctx[None, :], jnp.zeros((MOD_ROWS - bsz - 1, D_MODEL), F32)], axis=0)
    mod = _ada(cvec, ada_w, ada_b)
    mod3 = mod.reshape(depth * MOD_ROWS, 1, 6 * D_MODEL)

    x2 = x.reshape(t_tok, D_MODEL)

    tm = min(PROJ_ROWS, n_lat)
    tpb = n_lat // tm
    lat_row0 = lambda i: i // tpb
    w_in = ab_w_in[0].astype(BF16)
    gn_mix0 = norm_mix[0].reshape(1, D_MODEL)
    qn = ab_q_norm[0].reshape(1, HEAD_DIM)
    kn = ab_k_norm[0].reshape(1, HEAD_DIM)
    gkf = jnp.pad(ab_gk_f[0], ((0, LR_COLS - B_GATE_RANK), (0, 0))).astype(BF16)
    gkb = jnp.pad(ab_gk_b[0], ((LR_COLS - B_GATE_RANK, 0), (0, 0))).astype(BF16)
    gkfb = ab_gk_f_bias[0].reshape(1, B_QK_W)
    gkbb = ab_gk_b_bias[0].reshape(1, B_QK_W)
    steps = t_tok // tm
    mlp_rows = depth * D_MODEL * D_FF // steps
    ride_along = (depth * D_MODEL) % (steps * BF16_SUBLANES) == 0 and mlp_rows * 4 <= CAST_BLOCK_BYTES
    cast = (mlp_w1.reshape(depth * D_MODEL, D_FF), mlp_w2.reshape(depth * D_FF, D_MODEL)) if ride_along else ()
    q, k, v, bq, bk, bv, gate, gf, gb, *cast_out = _proj(
        x2, gn_mix0, mod3, lat_row0, tpb, tm, w_in, qn, kn, _rope_tables(n_lat), gkf, gkfb, gkb, gkbb, True, cast)
    if ride_along:
        w1 = cast_out[0].reshape(depth, D_MODEL, D_FF)
        w2 = cast_out[1].reshape(depth, D_FF, D_MODEL)
    else:
        w1, w2 = mlp_w1.astype(BF16), mlp_w2.astype(BF16)
    tmc = min(PROJ_ROWS, n_ctx)
    kc, vc, bkc, bvc, gfc, gbc = _proj(
        ctx.reshape(bsz * n_ctx, D_MODEL), gn_mix0, mod3, lambda i: bsz, n_ctx // tmc, tmc, w_in, None, kn, None,
        gkf, gkfb, gkb, gkbb, False)

    oa = _attn(ab_sink[0], q, k, v, kc, vc, bsz, n_lat, n_ctx)
    ob = _gla(bq, bk, bv, gf, gb, bkc, bvc, gfc, gbc, bsz, n_lat, n_ctx)

    w_out = ab_w_out[0].astype(BF16)
    tmo = min(TOKEN_ROWS, n_lat)
    tpbo = n_lat // tmo
    x2 = _mixout(oa, ob, gate, ab_gla_norm[0].reshape(1, B_DV), w_out[:A_Q_W], w_out[A_Q_W:], x2, mod3,
                 lambda i: i // tpbo, tmo)
    x2 = _mlp(x2, norm_mlp[0].reshape(1, D_MODEL), mod3, lambda i: i // tpbo, w1, w2, 0, tmo, MLP_STEP_COLS)

    t_ch, t_mid, cn, sn = _dft_tables(n_lat)
    row1 = lambda i: MOD_ROWS + i // tpbo
    tmf = min(FCH_ROWS, n_lat)
    yc, ys = _fch(x2, norm_mix[1].reshape(1, D_MODEL), mod3, lambda i: MOD_ROWS + i // (n_lat // tmf), t_ch, t_mid,
                  tmf)
    z_rows = min(tmo, n_lat // 2)
    z = _fpos(cn, sn, yc, ys, bsz, n_lat, z_rows)
    x2 = _linres(z, z_rows, c_w_out[0], c_b_out[0].reshape(1, D_MODEL), x2, mod3, row1, tmo, n_lat)
    x2 = _mlp(x2, norm_mlp[1].reshape(1, D_MODEL), mod3, row1, w1, w2, 1, tmo, MLP_STEP_COLS)
    return x2.reshape(bsz, n_lat, D_MODEL)
```
